```python
import functools
import jax, jax.numpy as jnp
from jax import lax
import numpy as np

D_MODEL = 1024
BATCH = 2
SEQ = 8192
DEPTH = 2
DEC_BATCH = 32
DEC_SEQ = 4
PAST_LEN = 8192
PAGE_SIZE = 128

GLA_HEADS = 4
GLA_DK = 64
GLA_DV = 128
GLA_GATE_RANK = 16
GLA_GATE_TEMP = 16.0
FOX_HEADS = 4
FOX_DH = 128
Q_BLOCK = 128
FOX_GATE_BIAS = 6.0
CACHE_LOGF_BIAS = 8.0
HG_HEADS = 4
HG_DK = 128
HG_DV = 128
CHUNK = 64
FFN_HIDDEN = -(-8 * D_MODEL // (3 * 256)) * 256
EPS = 1e-6
IN_WIDTH = (2 * GLA_HEADS * GLA_DK + 2 * GLA_HEADS * GLA_DV + GLA_GATE_RANK
            + 3 * FOX_HEADS * FOX_DH + FOX_HEADS
            + 2 * HG_HEADS * HG_DK + 2 * HG_HEADS * HG_DV + 3 * D_MODEL)

kernel_name = 'hybrid_gla_fox_hgrn2_step'


def _split_points():
    widths = (GLA_HEADS * GLA_DK, GLA_HEADS * GLA_DK, GLA_HEADS * GLA_DV, GLA_GATE_RANK, GLA_HEADS * GLA_DV,
              FOX_HEADS * FOX_DH, FOX_HEADS * FOX_DH, FOX_HEADS * FOX_DH, FOX_HEADS,
              HG_HEADS * HG_DK, HG_HEADS * HG_DK, HG_HEADS * HG_DV, HG_HEADS * HG_DV,
              D_MODEL, D_MODEL, D_MODEL)
    return [int(c) for c in np.cumsum(widths)[:-1]]


def _rms_norm(x, g):
    xf = x.astype(jnp.float32)
    y = xf * lax.rsqrt(jnp.mean(xf * xf, axis=-1, keepdims=True) + EPS)
    return (y * g.astype(jnp.float32)).astype(x.dtype)


def _gated_linear_scan(q, k, v, log_a, s0):
    B, T, H, K = q.shape
    V = v.shape[-1]
    C = min(CHUNK, T)
    n = -(-T // C)
    pad = n * C - T

    def prep(a):
        a = jnp.pad(a.astype(jnp.float32), ((0, 0), (0, pad), (0, 0), (0, 0)))
        return a.reshape(B, n, C, H, a.shape[-1]).transpose(1, 0, 3, 2, 4)

    qs, ks, vs, ls = prep(q), prep(k), prep(v), prep(log_a)
    causal = jnp.tril(jnp.ones((C, C), dtype=bool))

    def step(S, inp):
        qc, kc, vc, lc = inp
        b = jnp.cumsum(lc, axis=2)
        diff = b[:, :, :, None, :] - b[:, :, None, :, :]
        decay = jnp.exp(jnp.where(causal[:, :, None], diff, -jnp.inf))
        A = jnp.einsum('bhtk,bhsk,bhtsk->bhts', qc, kc, decay)
        o = (jnp.einsum('bhts,bhsv->bhtv', A, vc)
             + jnp.einsum('bhtk,bhkv->bhtv', qc * jnp.exp(b), S))
        b_last = b[:, :, -1:, :]
        S = (jnp.exp(b_last[:, :, 0, :])[..., None] * S
             + jnp.einsum('bhsk,bhsv->bhkv', kc * jnp.exp(b_last - b), vc))
        return S, o

    S, o = lax.scan(step, s0.astype(jnp.float32), (qs, ks, vs, ls))
    o = o.transpose(1, 0, 3, 2, 4).reshape(B, n * C, H, V)[:, :T]
    return o.astype(q.dtype), S.astype(s0.dtype)


def _fox_prompt_attend(q, k, v, logf):
    B, T, H, Dh = q.shape
    nb = T // Q_BLOCK
    c = jnp.cumsum(logf.astype(jnp.float32), axis=1)
    ck = c.transpose(0, 2, 1)[:, :, None, :]
    kpos = jnp.arange(T)
    scale = Dh ** -0.5
    qb = q.reshape(B, nb, Q_BLOCK, H, Dh).transpose(1, 0, 2, 3, 4)
    cq = c.reshape(B, nb, Q_BLOCK, H).transpose(1, 0, 3, 2)

    def block(args):
        i, qi, cqi = args
        s = jnp.einsum('bqhd,bkhd->bhqk', qi, k, preferred_element_type=jnp.float32) * scale
        qpos = i * Q_BLOCK + jnp.arange(Q_BLOCK)
        logits = jnp.where(kpos[None, :] <= qpos[:, None], s + cqi[..., None] - ck, -jnp.inf)
        p = jax.nn.softmax(logits, axis=-1)
        return jnp.einsum('bhqk,bkhd->bqhd', p.astype(v.dtype), v)

    o = lax.map(block, (jnp.arange(nb), qb, cq))
    return o.transpose(1, 0, 2, 3, 4).reshape(B, T, H, Dh)


def _fox_sample_attend(q, k, v, logf, cache_k, cache_v, cache_lf, page_table):
    DB, Tn, H, Dh = q.shape
    kp = cache_k[page_table].reshape(DB, -1, H, Dh)
    vp = cache_v[page_table].reshape(DB, -1, H, Dh)
    lp = cache_lf[page_table].reshape(DB, -1, H).astype(jnp.float32)
    r = jnp.cumsum(lp[:, ::-1], axis=1)[:, ::-1] - lp
    cn = jnp.cumsum(logf.astype(jnp.float32), axis=1).transpose(0, 2, 1)
    bias_past = cn[..., None] + r.transpose(0, 2, 1)[:, :, None, :]
    causal = jnp.tril(jnp.ones((Tn, Tn), dtype=bool))
    bias_new = jnp.where(causal, cn[..., None] - cn[:, :, None, :], -jnp.inf)
    kk = jnp.concatenate([kp, k.astype(kp.dtype)], axis=1)
    vv = jnp.concatenate([vp, v.astype(vp.dtype)], axis=1)
    s = jnp.einsum('bqhd,bkhd->bhqk', q, kk, preferred_element_type=jnp.float32) * Dh ** -0.5
    p = jax.nn.softmax(s + jnp.concatenate([bias_past, bias_new], axis=-1), axis=-1)
    return jnp.einsum('bhqk,bkhd->bqhd', p.astype(vv.dtype), vv).astype(q.dtype)


def _mixer(xn, w_in, gla_wg2, gla_bg, gla_norm_g, fox_bf, hg_lb, hg_norm_g,
           w_branch_a, w_branch_b, w_branch_c, w_out, gla_s0, hg_s0, fox_attend):
    B, T, _ = xn.shape
    z = xn @ w_in
    (gq, gk, gv, glr, gr, fq, fk, fv, ff, hq, hf, hi, hg, ga, gb, gc) = jnp.split(z, _split_points(), axis=-1)
    heads = lambda a, h: a.reshape(B, T, h, -1)
    log_a = jax.nn.log_sigmoid((glr @ gla_wg2 + gla_bg).astype(jnp.float32)) / GLA_GATE_TEMP
    o_a, gla_s = _gated_linear_scan(heads(gq, GLA_HEADS) * GLA_DK ** -0.5, heads(gk, GLA_HEADS),
                                    heads(gv, GLA_HEADS), heads(log_a, GLA_HEADS), gla_s0)
    o_a = (_rms_norm(o_a, gla_norm_g) * jax.nn.silu(heads(gr, GLA_HEADS))).reshape(B, T, -1)
    logf = jax.nn.log_sigmoid((ff + fox_bf).astype(jnp.float32))
    fk_h, fv_h = heads(fk, FOX_HEADS), heads(fv, FOX_HEADS)
    o_b = fox_attend(heads(fq, FOX_HEADS), fk_h, fv_h, logf).reshape(B, T, -1)
    hf32 = hf.astype(jnp.float32)
    log_f = jnp.logaddexp(jnp.log(hg_lb), jnp.log1p(-hg_lb) + jax.nn.log_sigmoid(hf32))
    one_minus_f = (1.0 - hg_lb) * jax.nn.sigmoid(-hf32)
    o_c, hg_s = _gated_linear_scan(jax.nn.silu(heads(hq, HG_HEADS)), heads(one_minus_f, HG_HEADS),
                                   heads(hi, HG_HEADS), heads(log_f, HG_HEADS), hg_s0)
    o_c = (_rms_norm(o_c, hg_norm_g) * jax.nn.silu(heads(hg, HG_HEADS))).reshape(B, T, -1)
    merged = (jax.nn.sigmoid(ga) * (o_a @ w_branch_a)
              + jax.nn.sigmoid(gb) * (o_b @ w_branch_b)
              + jax.nn.sigmoid(gc) * (o_c @ w_branch_c))
    return merged @ w_out, gla_s, hg_s, fk_h, fv_h, logf


def _decoder(x, gla_s0, hg_s0, fox_attends, hg_lb, norm1_g, w_in, gla_wg2, gla_bg, gla_norm_g, fox_bf,
             hg_norm_g, w_branch_a, w_branch_b, w_branch_c, w_out, norm2_g, w_ffn_gate, w_ffn_up,
             w_ffn_down, final_norm_g):
    gla_st, hg_st, ks, vs, lfs = [], [], [], [], []
    for l in range(DEPTH):
        y, g_s, h_s, k, v, lf = _mixer(_rms_norm(x, norm1_g[l]), w_in[l], gla_wg2[l], gla_bg[l], gla_norm_g[l],
                                       fox_bf[l], hg_lb[l], hg_norm_g[l], w_branch_a[l], w_branch_b[l],
                                       w_branch_c[l], w_out[l], gla_s0[l], hg_s0[l], fox_attends[l])
        x = x + y
        h = _rms_norm(x, norm2_g[l])
        x = x + (jax.nn.silu(h @ w_ffn_gate[l]) * (h @ w_ffn_up[l])) @ w_ffn_down[l]
        gla_st.append(g_s)
        hg_st.append(h_s)
        ks.append(k)
        vs.append(v)
        lfs.append(lf)
    return (_rms_norm(x, final_norm_g), jnp.stack(gla_st), jnp.stack(hg_st),
            jnp.stack(ks), jnp.stack(vs), jnp.stack(lfs))


def setup_inputs(seed: int = 0) -> dict:
    key = jax.random.key(seed)
    ks = jax.random.split(key, 32)
    n_pages = PAST_LEN // PAGE_SIZE
    n_used = DEC_BATCH * n_pages
    n_pool = n_used + n_used // 4
    nrm = lambda k, shape, s: jax.random.normal(k, shape, jnp.float32) * s
    fox_w = FOX_HEADS * FOX_DH
    return {
        'x_prompt': nrm(ks[0], (BATCH, SEQ, D_MODEL), 1.0),
        'x_sample': nrm(ks[1], (DEC_BATCH, DEC_SEQ, D_MODEL), 1.0),
        'state_gla': nrm(ks[2], (DEPTH, DEC_BATCH, GLA_HEADS, GLA_DK, GLA_DV), 1.0),
        'cache_fox_k': nrm(ks[3], (DEPTH, n_pool, PAGE_SIZE, FOX_HEADS, FOX_DH), 1.0),
        'cache_fox_v': nrm(ks[4], (DEPTH, n_pool, PAGE_SIZE, FOX_HEADS, FOX_DH), 1.0),
        'cache_fox_logf': jax.nn.log_sigmoid(nrm(ks[5], (DEPTH, n_pool, PAGE_SIZE, FOX_HEADS), 0.5) + CACHE_LOGF_BIAS),
        'state_hgrn': nrm(ks[6], (DEPTH, DEC_BATCH, HG_HEADS, HG_DK, HG_DV), 1.0),
        'page_table': jax.random.permutation(ks[7], n_pool)[:n_used].reshape(DEC_BATCH, n_pages).astype(jnp.int32),
        'norm1_g': 1.0 + nrm(ks[8], (DEPTH, D_MODEL), 0.01),
        'w_in': nrm(ks[9], (DEPTH, D_MODEL, IN_WIDTH), D_MODEL ** -0.5),
        'gla_wg2': nrm(ks[10], (DEPTH, GLA_GATE_RANK, GLA_HEADS * GLA_DK), GLA_GATE_RANK ** -0.5),
        'gla_bg': nrm(ks[11], (DEPTH, GLA_HEADS * GLA_DK), 0.1),
        'gla_norm_g': 1.0 + nrm(ks[12], (DEPTH, GLA_DV), 0.01),
        'fox_bf': FOX_GATE_BIAS + nrm(ks[13], (DEPTH, FOX_HEADS), 0.1),
        'hg_lb_logits': nrm(ks[14], (DEPTH, HG_HEADS * HG_DK), 0.1),
        'hg_norm_g': 1.0 + nrm(ks[15], (DEPTH, HG_DV), 0.01),
        'w_branch_a': nrm(ks[16], (DEPTH, GLA_HEADS * GLA_DV, D_MODEL), (GLA_HEADS * GLA_DV) ** -0.5),
        'w_branch_b': nrm(ks[17], (DEPTH, fox_w, D_MODEL), fox_w ** -0.5),
        'w_branch_c': nrm(ks[18], (DEPTH, HG_HEADS * HG_DV, D_MODEL), (HG_HEADS * HG_DV) ** -0.5),
        'w_out': nrm(ks[19], (DEPTH, D_MODEL, D_MODEL), D_MODEL ** -0.5),
        'norm2_g': 1.0 + nrm(ks[20], (DEPTH, D_MODEL), 0.01),
        'w_ffn_gate': nrm(ks[21], (DEPTH, D_MODEL, FFN_HIDDEN), D_MODEL ** -0.5),
        'w_ffn_up': nrm(ks[22], (DEPTH, D_MODEL, FFN_HIDDEN), D_MODEL ** -0.5),
        'w_ffn_down': nrm(ks[23], (DEPTH, FFN_HIDDEN, D_MODEL), FFN_HIDDEN ** -0.5),
        'final_norm_g': 1.0 + nrm(ks[24], (D_MODEL,), 0.01),
    }


def reference(x_prompt, x_sample, state_gla, cache_fox_k, cache_fox_v, cache_fox_logf, state_hgrn, page_table,
              norm1_g, w_in, gla_wg2, gla_bg, gla_norm_g, fox_bf, hg_lb_logits, hg_norm_g,
              w_branch_a, w_branch_b, w_branch_c, w_out, norm2_g, w_ffn_gate, w_ffn_up, w_ffn_down,
              final_norm_g):
    lb_cum = jnp.cumsum(jax.nn.softmax(hg_lb_logits.astype(jnp.float32), axis=0), axis=0)
    hg_lb = lb_cum - lb_cum[:1]
    b_p = x_prompt.shape[0]
    gla0 = jnp.zeros((DEPTH, b_p, GLA_HEADS, GLA_DK, GLA_DV), x_prompt.dtype)
    hg0 = jnp.zeros((DEPTH, b_p, HG_HEADS, HG_DK, HG_DV), x_prompt.dtype)
    y_p, gla_p, hg_p, k_p, v_p, lf_p = _decoder(
        x_prompt, gla0, hg0, [_fox_prompt_attend] * DEPTH, hg_lb, norm1_g, w_in, gla_wg2, gla_bg, gla_norm_g,
        fox_bf, hg_norm_g, w_branch_a, w_branch_b, w_branch_c, w_out, norm2_g, w_ffn_gate, w_ffn_up,
        w_ffn_down, final_norm_g)
    sample_attends = [functools.partial(_fox_sample_attend, cache_k=cache_fox_k[l], cache_v=cache_fox_v[l],
                                        cache_lf=cache_fox_logf[l], page_table=page_table)
                      for l in range(DEPTH)]
    y_s, gla_s, hg_s, k_s, v_s, lf_s = _decoder(
        x_sample, state_gla, state_hgrn, sample_attends, hg_lb, norm1_g, w_in, gla_wg2, gla_bg, gla_norm_g,
        fox_bf, hg_norm_g, w_branch_a, w_branch_b, w_branch_c, w_out, norm2_g, w_ffn_gate, w_ffn_up,
        w_ffn_down, final_norm_g)
    return (y_p, y_s, gla_p, gla_s, k_p, v_p, lf_p, k_s, v_s, lf_s, hg_p, hg_s)
```

```python
import functools

import numpy as np
import jax
import jax.numpy as jnp
from jax import lax
from jax.experimental import pallas as pl
from jax.experimental.pallas import tpu as pltpu

F32 = jnp.float32
BF16 = jnp.bfloat16
EPS = 1e-6
NEG = -1e30
LANES = 128
SUB = 8
HI = lax.Precision.HIGHEST
VMEM_LIMIT = 56 * 1024 * 1024

GLA_GATE_TEMP = 16.0
GLA_GATE_RANK = 16
FF_OFF = 0
GLR_OFF = 16


def _cparams(sem):
    return pltpu.CompilerParams(dimension_semantics=sem, vmem_limit_bytes=VMEM_LIMIT)


def _rms(x, g):
    return x * lax.rsqrt(jnp.mean(x * x, axis=-1, keepdims=True) + EPS) * g


def _sigmoid(x):
    return 1.0 / (1.0 + jnp.exp(-x))


def _log_sigmoid(x):
    return jnp.minimum(x, 0.0) - jnp.log(1.0 + jnp.exp(-jnp.abs(x)))


def _silu(x):
    return x * _sigmoid(x)


def _dot_t(a, b):
    return lax.dot_general(a, b, (((1,), (1,)), ((), ())), preferred_element_type=F32)


def _proj_kernel(x_ref, g_ref, w_ref, z_ref, xn_ref):
    @pl.when(pl.program_id(1) == 0)
    def _():
        xn_ref[...] = _rms(x_ref[...], g_ref[...]).astype(BF16)

    z_ref[...] = jnp.dot(xn_ref[...], w_ref[...], preferred_element_type=F32)


def _proj(x, g, w, tm, tn):
    m, d = x.shape
    n = w.shape[1]
    return pl.pallas_call(
        _proj_kernel,
        grid=(m // tm, n // tn),
        in_specs=[pl.BlockSpec((tm, d), lambda i, j: (i, 0)),
                  pl.BlockSpec((1, d), lambda i, j: (0, 0)),
                  pl.BlockSpec((d, tn), lambda i, j: (0, j))],
        out_specs=pl.BlockSpec((tm, tn), lambda i, j: (i, j)),
        out_shape=jax.ShapeDtypeStruct((m, n), F32),
        scratch_shapes=[pltpu.VMEM((tm, d), BF16)],
        compiler_params=_cparams(("parallel", "arbitrary")),
        name="proj",
    )(x, g, w)


def _fox_prep_kernel(zs_ref, bf_ref, sel_ref, triu_ref, lf_ref, c_ref, carry_ref, *, nh):
    @pl.when(pl.program_id(1) == 0)
    def _():
        carry_ref[...] = jnp.zeros_like(carry_ref)

    lf = _log_sigmoid(zs_ref[...] + bf_ref[...])
    lf_ref[0] = lf[:, FF_OFF:FF_OFF + nh]
    lft = lax.dot_general(sel_ref[...], lf, (((1,), (1,)), ((), ())),
                          precision=HI, preferred_element_type=F32)
    c = jnp.dot(lft, triu_ref[...], precision=HI, preferred_element_type=F32) + carry_ref[...][:, :1]
    c_ref[0] = c
    carry_ref[...] = jnp.broadcast_to(c[:, -1:], carry_ref.shape)


def _fox_prep(z, bf_row, b, t, small_blk, nh, tt):
    sel = np.zeros((SUB, LANES), np.float32)
    for h in range(nh):
        sel[h, FF_OFF + h] = 1.0
    triu = np.triu(np.ones((tt, tt), np.float32))
    nt = t // tt
    return pl.pallas_call(
        functools.partial(_fox_prep_kernel, nh=nh),
        grid=(b, nt),
        in_specs=[pl.BlockSpec((tt, LANES), lambda i, j: (i * nt + j, small_blk)),
                  pl.BlockSpec((1, LANES), lambda i, j: (0, 0)),
                  pl.BlockSpec((SUB, LANES), lambda i, j: (0, 0)),
                  pl.BlockSpec((tt, tt), lambda i, j: (0, 0))],
        out_specs=[pl.BlockSpec((1, tt, nh), lambda i, j: (i, j, 0)),
                   pl.BlockSpec((1, SUB, tt), lambda i, j: (i, 0, j))],
        out_shape=[jax.ShapeDtypeStruct((b, t, nh), F32),
                   jax.ShapeDtypeStruct((b, SUB, t), F32)],
        scratch_shapes=[pltpu.VMEM((SUB, LANES), F32)],
        compiler_params=_cparams(("parallel", "arbitrary")),
        name="fox_prep",
    )(z, bf_row, jnp.asarray(sel), jnp.asarray(triu))


def _flash_kernel(q_ref, k_ref, v_ref, c_ref, o_ref, kb_ref, vb_ref, *, blk, scale):
    h = pl.program_id(1)
    qi = pl.program_id(2)

    @pl.when(qi == 0)
    def _():
        kb_ref[...] = k_ref[...].astype(BF16)
        vb_ref[...] = v_ref[...].astype(BF16)

    q = (q_ref[...] * scale).astype(BF16)

    def step(j, carry, masked):
        m, l, acc = carry
        off = pl.multiple_of(j * blk, blk)
        kj = kb_ref[pl.ds(off, blk), :]
        vj = vb_ref[pl.ds(off, blk), :]
        s = _dot_t(q, kj) - c_ref[0, pl.ds(h, 1), pl.ds(off, blk)]
        if masked:
            row = lax.broadcasted_iota(jnp.int32, (blk, blk), 0)
            col = lax.broadcasted_iota(jnp.int32, (blk, blk), 1)
            s = jnp.where(col <= row, s, NEG)
        m_new = jnp.maximum(m, jnp.max(s, axis=-1, keepdims=True))
        alpha = jnp.exp(m - m_new)
        p = jnp.exp(s - m_new)
        l = alpha * l + jnp.sum(p, axis=-1, keepdims=True)
        acc = alpha * acc + jnp.dot(p.astype(BF16), vj, preferred_element_type=F32)
        return m_new, l, acc

    init = (jnp.full((blk, 1), NEG, F32), jnp.zeros((blk, 1), F32), jnp.zeros((blk, q.shape[1]), F32))
    carry = lax.fori_loop(0, qi, lambda j, c: step(j, c, False), init)
    m, l, acc = step(qi, carry, True)
    o_ref[0] = (acc / l).astype(o_ref.dtype)


def _flash(z, crow, b, t, nh, dh, q_blk0, k_blk0, v_blk0, blk):
    nq = t // blk
    return pl.pallas_call(
        functools.partial(_flash_kernel, blk=blk, scale=dh ** -0.5),
        grid=(b, nh, nq),
        in_specs=[pl.BlockSpec((blk, dh), lambda i, h, j: (i * nq + j, q_blk0 + h)),
                  pl.BlockSpec((t, dh), lambda i, h, j: (i, k_blk0 + h)),
                  pl.BlockSpec((t, dh), lambda i, h, j: (i, v_blk0 + h)),
                  pl.BlockSpec((1, SUB, t), lambda i, h, j: (i, 0, 0))],
        out_specs=pl.BlockSpec((1, blk, dh), lambda i, h, j: (i, j, h)),
        out_shape=jax.ShapeDtypeStruct((b, t, nh * dh), BF16),
        scratch_shapes=[pltpu.VMEM((t, dh), BF16), pltpu.VMEM((t, dh), BF16)],
        compiler_params=_cparams(("parallel", "parallel", "arbitrary")),
        name="flash",
    )(z, z, z, crow)


def _local_cumsum(x):
    row = lax.broadcasted_iota(jnp.int32, x.shape, 0)
    for sh in (1, 2, 4):
        x = x + jnp.where(row >= sh, pltpu.roll(x, sh, 0), 0.0)
    return x


def _scan_chunk(q, k, la, vs, masks, states, ones_ws, sel):
    c = q.shape[0]
    n = c // SUB
    nh = len(vs)
    sub_iota = lax.broadcasted_iota(jnp.int32, (SUB, LANES), 0)
    zero_blk = jnp.zeros((SUB, LANES), F32)

    r = [jnp.zeros((1, LANES), F32)]
    qt, kh, p_rows = [], [], []
    for i in range(n):
        sl = slice(i * SUB, (i + 1) * SUB)
        qi, ki = q[sl], k[sl]
        li = _local_cumsum(la[sl])
        tot = li[SUB - 1:SUB]
        r.append(r[i] + tot)
        qt.append(qi * jnp.exp(li))
        kh.append(ki * jnp.exp(tot - li))
        for t in range(SUB):
            d = jnp.where(sub_iota <= t, li[t:t + 1] - li, NEG)
            p_rows.append(jnp.exp(d) * (qi[t:t + 1] * ki))
    p_all = jnp.concatenate(p_rows, axis=0).astype(BF16)

    qt_all = jnp.concatenate(qt, axis=0) if n > 1 else qt[0]
    qbar = jnp.concatenate([qt[i] * jnp.exp(r[i]) for i in range(n)], axis=0) if n > 1 else qt[0]

    def rhs(i):
        blocks = [kh[j] * jnp.exp(r[i] - r[j + 1]) if j + 1 < i else kh[j] for j in range(min(i, n))]
        blocks += [zero_blk] * (n - len(blocks))
        return jnp.concatenate(blocks, axis=0) if n > 1 else blocks[0]

    def mask(x, h):
        return x if masks[h] is None else x * masks[h]

    outs = []
    for h in range(nh):
        rr = jnp.dot(p_all, ones_ws[h], preferred_element_type=F32)
        vrep = jnp.concatenate([vs[h][i * SUB:(i + 1) * SUB] for i in range(n) for _ in range(SUB)], axis=0)
        o = jnp.dot(sel, (rr * vrep).astype(BF16), preferred_element_type=F32)
        o = o + _dot_t(mask(qbar, h).astype(BF16), states[h].astype(BF16))
        outs.append(o)

    if n > 1:
        a_rows = [[zero_blk[:, :c]] for _ in range(nh)]
        for i in range(1, n):
            lhs = jnp.concatenate([mask(qt[i], h) for h in range(nh)], axis=0).astype(BF16)
            a_i = _dot_t(lhs, rhs(i).astype(BF16))
            for h in range(nh):
                a_rows[h].append(a_i[h * SUB:(h + 1) * SUB])
        for h in range(nh):
            a = jnp.concatenate(a_rows[h], axis=0).astype(BF16)
            outs[h] = outs[h] + jnp.dot(a, vs[h].astype(BF16), preferred_element_type=F32)

    k_end = rhs(n)
    new_states = []
    for h in range(nh):
        upd = lax.dot_general(vs[h].astype(BF16), mask(k_end, h).astype(BF16),
                              (((0,), (0,)), ((), ())), preferred_element_type=F32)
        new_states.append(states[h] * jnp.exp(r[n]) + upd)
    return outs, new_states


def _scan_kernel(*refs, mode, chunk, nh):
    if mode == "gla":
        (zq_ref, zk_ref, zv_ref, zs_ref, zr_ref, wg2_ref, bg_ref, ng_ref, ones_ref, sel_ref,
         o_ref, sout_ref, q_s, k_s, la_s, st_s) = refs
    else:
        (zq_ref, zf_ref, zv_ref, zr_ref, llb_ref, l1m_ref, oml_ref, ng_ref, ones_ref, sel_ref,
         o_ref, sout_ref, q_s, k_s, la_s, st_s) = refs
    ti = pl.program_id(2)
    tblk = q_s.shape[0]

    @pl.when(ti == 0)
    def _():
        st_s[...] = jnp.zeros_like(st_s)

    if mode == "gla":
        dk = LANES // nh
        q_s[...] = zq_ref[...] * dk ** -0.5
        k_s[...] = zk_ref[...]
        zg = jnp.dot(zs_ref[...].astype(BF16), wg2_ref[...], preferred_element_type=F32) + bg_ref[...]
        la_s[...] = _log_sigmoid(zg) * (1.0 / GLA_GATE_TEMP)
        lane = lax.broadcasted_iota(jnp.int32, (1, LANES), 1)
        masks = [((lane >= h * dk) & (lane < (h + 1) * dk)).astype(F32) for h in range(nh)]
    else:
        hf = zf_ref[...]
        e = jnp.exp(-jnp.abs(hf))
        l1pe = jnp.log(1.0 + e)
        inv = 1.0 / (1.0 + e)
        sig_neg = jnp.where(hf > 0, e * inv, inv)
        lsig = jnp.minimum(hf, 0.0) - l1pe
        a = llb_ref[...]
        bb = l1m_ref[...] + lsig
        la_s[...] = jnp.maximum(a, bb) + jnp.log(1.0 + jnp.exp(-jnp.abs(a - bb)))
        k_s[...] = oml_ref[...] * sig_neg
        q_s[...] = _silu(zq_ref[...])
        masks = [None]

    sel = sel_ref[...]
    ones_ws = [ones_ref[h] for h in range(nh)]

    def body(ci, _):
        off = pl.multiple_of(ci * chunk, chunk)
        sl = pl.ds(off, chunk)
        vs = [zv_ref[sl, h * LANES:(h + 1) * LANES] for h in range(nh)]
        states = [st_s[h] for h in range(nh)]
        outs, new_states = _scan_chunk(q_s[sl, :], k_s[sl, :], la_s[sl, :], vs, masks, states, ones_ws, sel)
        for h in range(nh):
            st_s[h] = new_states[h]
            o = outs[h]
            g = zr_ref[sl, h * LANES:(h + 1) * LANES]
            o = _rms(o, ng_ref[...]) * _silu(g)
            o_ref[0, sl, h * LANES:(h + 1) * LANES] = o.astype(o_ref.dtype)
        return 0

    lax.fori_loop(0, tblk // chunk, body, 0)

    @pl.when(ti == pl.num_programs(2) - 1)
    def _():
        dk_out = LANES // nh
        for h in range(nh):
            sout_ref[0, h] = st_s[h].T[h * dk_out:(h + 1) * dk_out, :]


def _scan_consts(chunk, nh):
    n = chunk // SUB
    sel = np.zeros((chunk, n * SUB * SUB), np.float32)
    for i in range(n):
        for t in range(SUB):
            sel[i * SUB + t, i * 64 + t * SUB:i * 64 + (t + 1) * SUB] = 1.0
    ones = np.zeros((nh, LANES, LANES), np.float32)
    dk = LANES // nh
    for h in range(nh):
        ones[h, h * dk:(h + 1) * dk, :] = 1.0
    return jnp.asarray(ones, BF16), jnp.asarray(sel, BF16)


def _scan_gla(z, b, t, units, col, wg2p, bg, ng, tblk, chunk):
    nh = 2
    nt = t // tblk
    ones, sel = _scan_consts(chunk, nh)
    row = lambda i, u, j: i * nt + j
    in_specs = [
        pl.BlockSpec((tblk, LANES), lambda i, u, j: (row(i, u, j), col["gq"] + u)),
        pl.BlockSpec((tblk, LANES), lambda i, u, j: (row(i, u, j), col["gk"] + u)),
        pl.BlockSpec((tblk, 2 * LANES), lambda i, u, j: (row(i, u, j), col["gv"] // 2 + u)),
        pl.BlockSpec((tblk, LANES), lambda i, u, j: (row(i, u, j), col["small"])),
        pl.BlockSpec((tblk, 2 * LANES), lambda i, u, j: (row(i, u, j), col["gr"] // 2 + u)),
        pl.BlockSpec((None, LANES, LANES), lambda i, u, j: (u, 0, 0)),
        pl.BlockSpec((None, 1, LANES), lambda i, u, j: (u, 0, 0)),
        pl.BlockSpec((1, LANES), lambda i, u, j: (0, 0)),
        pl.BlockSpec(ones.shape, lambda i, u, j: (0, 0, 0)),
        pl.BlockSpec(sel.shape, lambda i, u, j: (0, 0)),
    ]
    dk = LANES // nh
    return pl.pallas_call(
        functools.partial(_scan_kernel, mode="gla", chunk=chunk, nh=nh),
        grid=(b, units, nt),
        in_specs=in_specs,
        out_specs=[pl.BlockSpec((1, tblk, 2 * LANES), lambda i, u, j: (i, j, u)),
                   pl.BlockSpec((1, nh, dk, LANES), lambda i, u, j: (i, u, 0, 0))],
        out_shape=[jax.ShapeDtypeStruct((b, t, units * 2 * LANES), BF16),
                   jax.ShapeDtypeStruct((b, units * nh, dk, LANES), F32)],
        scratch_shapes=[pltpu.VMEM((tblk, LANES), F32)] * 3 + [pltpu.VMEM((nh, LANES, LANES), F32)],
        compiler_params=_cparams(("parallel", "parallel", "arbitrary")),
        name="scan_gla",
    )(z, z, z, z, z, wg2p, bg, ng, ones, sel)


def _scan_hgrn(z, b, t, units, col, llb, l1m, oml, ng, tblk, chunk):
    nh = 1
    nt = t // tblk
    ones, sel = _scan_consts(chunk, nh)
    row = lambda i, u, j: i * nt + j
    zspec = lambda name: pl.BlockSpec((tblk, LANES), lambda i, u, j: (row(i, u, j), col[name] + u))
    pspec = pl.BlockSpec((None, 1, LANES), lambda i, u, j: (u, 0, 0))
    in_specs = [zspec("hq"), zspec("hf"), zspec("hi"), zspec("hg"), pspec, pspec, pspec,
                pl.BlockSpec((1, LANES), lambda i, u, j: (0, 0)),
                pl.BlockSpec(ones.shape, lambda i, u, j: (0, 0, 0)),
                pl.BlockSpec(sel.shape, lambda i, u, j: (0, 0))]
    return pl.pallas_call(
        functools.partial(_scan_kernel, mode="hgrn", chunk=chunk, nh=nh),
        grid=(b, units, nt),
        in_specs=in_specs,
        out_specs=[pl.BlockSpec((1, tblk, LANES), lambda i, u, j: (i, j, u)),
                   pl.BlockSpec((1, nh, LANES, LANES), lambda i, u, j: (i, u, 0, 0))],
        out_shape=[jax.ShapeDtypeStruct((b, t, units * LANES), BF16),
                   jax.ShapeDtypeStruct((b, units, LANES, LANES), F32)],
        scratch_shapes=[pltpu.VMEM((tblk, LANES), F32)] * 3 + [pltpu.VMEM((nh, LANES, LANES), F32)],
        compiler_params=_cparams(("parallel", "parallel", "arbitrary")),
        name="scan_hgrn",
    )(z, z, z, z, llb, l1m, oml, ng, ones, sel)


def _rec_kernel(*refs, mode, nseq, ntok, dk):
    if mode == "gla":
        (qt_ref, kt_ref, st_ref, wg2t_ref, bgt_ref, zv_ref, zr_ref, ng_ref, s0_ref, o_ref, sout_ref, o_s) = refs
        qc = qt_ref[...] * dk ** -0.5
        kc = kt_ref[...]
        zg = jnp.dot(wg2t_ref[...], st_ref[...].astype(BF16), preferred_element_type=F32) + bgt_ref[...]
        ac = jnp.exp(_log_sigmoid(zg) * (1.0 / GLA_GATE_TEMP))
    else:
        (qt_ref, ft_ref, llb_ref, l1m_ref, oml_ref, zv_ref, zr_ref, ng_ref, s0_ref, o_ref, sout_ref, o_s) = refs
        hf = ft_ref[...]
        lsig = _log_sigmoid(hf)
        a = llb_ref[...]
        bb = l1m_ref[...] + lsig
        ac = jnp.exp(jnp.maximum(a, bb) + jnp.log(1.0 + jnp.exp(-jnp.abs(a - bb))))
        kc = oml_ref[...] * _sigmoid(-hf)
        qc = _silu(qt_ref[...])
    for sq in range(nseq):
        s = s0_ref[sq, 0]
        for t in range(ntok):
            j = sq * ntok + t
            vrow = zv_ref[j:j + 1, :]
            s = s * ac[:, j:j + 1] + kc[:, j:j + 1] * vrow
            o_s[j:j + 1, :] = jnp.sum(s * qc[:, j:j + 1], axis=0, keepdims=True)
        sout_ref[sq, 0] = s
    o = _rms(o_s[...], ng_ref[...]) * _silu(zr_ref[...])
    o_ref[...] = o.astype(o_ref.dtype)


def _rec(mode, zt, z, col, params, ng, s0, nb, ntok, heads, dk, nseq):
    m = nb * ntok
    rows = nseq * ntok
    ng_groups = nb // nseq
    if mode == "gla":
        wg2t, bgt = params
        per = LANES // dk
        tspec = lambda name: pl.BlockSpec((None, dk, rows), lambda h, g: (g, col[name] * per + h, 0))
        in_specs = [tspec("gq"), tspec("gk"),
                    pl.BlockSpec((None, LANES, rows), lambda h, g: (g, col["small"], 0)),
                    pl.BlockSpec((dk, LANES), lambda h, g: (h, 0)),
                    pl.BlockSpec((dk, 1), lambda h, g: (h, 0)),
                    pl.BlockSpec((rows, LANES), lambda h, g: (g, col["gv"] + h)),
                    pl.BlockSpec((rows, LANES), lambda h, g: (g, col["gr"] + h))]
        args = (zt, zt, zt, wg2t, bgt, z, z)
    else:
        llb, l1m, oml = params
        tspec = lambda name: pl.BlockSpec((None, dk, rows), lambda h, g: (g, col[name] + h, 0))
        pspec = pl.BlockSpec((dk, 1), lambda h, g: (h, 0))
        in_specs = [tspec("hq"), tspec("hf"), pspec, pspec, pspec,
                    pl.BlockSpec((rows, LANES), lambda h, g: (g, col["hi"] + h)),
                    pl.BlockSpec((rows, LANES), lambda h, g: (g, col["hg"] + h))]
        args = (zt, zt, llb, l1m, oml, z, z)
    in_specs += [pl.BlockSpec((1, LANES), lambda h, g: (0, 0)),
                 pl.BlockSpec((nseq, 1, dk, LANES), lambda h, g: (g, h, 0, 0))]
    return pl.pallas_call(
        functools.partial(_rec_kernel, mode=mode, nseq=nseq, ntok=ntok, dk=dk),
        grid=(heads, ng_groups),
        in_specs=in_specs,
        out_specs=[pl.BlockSpec((rows, LANES), lambda h, g: (g, h)),
                   pl.BlockSpec((nseq, 1, dk, LANES), lambda h, g: (g, h, 0, 0))],
        out_shape=[jax.ShapeDtypeStruct((m, heads * LANES), BF16),
                   jax.ShapeDtypeStruct(s0.shape, F32)],
        scratch_shapes=[pltpu.VMEM((rows, LANES), F32)],
        compiler_params=_cparams(("parallel", "parallel")),
        name="rec_" + mode,
    )(*args, ng, s0)


def _paged_kernel(pt_ref, q_ref, kn_ref, vn_ref, zs_ref, bf_ref, tsuf_ref, *rest, npg, nh, dh, ntok, scale):
    k_refs = rest[:npg]
    v_refs = rest[npg:2 * npg]
    lf_refs = rest[2 * npg:3 * npg]
    o_ref, lf_ref = rest[3 * npg:3 * npg + 2]
    m_s, l_s, acc_s, car_s = rest[3 * npg + 2:]
    g = pl.program_id(1)

    @pl.when(g == 0)
    def _():
        m_s[...] = jnp.full(m_s.shape, NEG, F32)
        l_s[...] = jnp.zeros_like(l_s)
        acc_s[...] = jnp.zeros_like(acc_s)
        car_s[...] = jnp.zeros_like(car_s)

    q = q_ref[0] * scale
    carry = car_s[...][:, :1]
    r_pages = []
    for i in range(npg):
        lp = lf_refs[i][...]
        r_pages.append(jnp.dot(lp, tsuf_ref[...], precision=HI, preferred_element_type=F32) + carry)
        carry = carry + jnp.sum(lp, axis=1, keepdims=True)
    car_s[...] = jnp.broadcast_to(carry, car_s.shape)

    for h in range(nh):
        hs = slice(h * dh, (h + 1) * dh)
        qh = q[:, hs].astype(BF16)
        s = jnp.concatenate([_dot_t(qh, k_refs[i][:, hs].astype(BF16)) + r_pages[i][h:h + 1]
                             for i in range(npg)], axis=1)
        m = m_s[h]
        m_new = jnp.maximum(m, jnp.max(s, axis=-1, keepdims=True))
        alpha = jnp.exp(m - m_new)
        p = jnp.exp(s - m_new)
        l_s[h] = alpha * l_s[h] + jnp.sum(p, axis=-1, keepdims=True)
        pg = s.shape[1] // npg
        pv = jnp.zeros((q.shape[0], dh), F32)
        for i in range(npg):
            pv = pv + jnp.dot(p[:, i * pg:(i + 1) * pg].astype(BF16), v_refs[i][:, hs].astype(BF16),
                              preferred_element_type=F32)
        acc_s[h] = alpha * acc_s[h] + pv
        m_s[h] = m_new

    @pl.when(g == pl.num_programs(1) - 1)
    def _():
        lfn = _log_sigmoid(zs_ref[0] + bf_ref[...])
        lf_ref[0] = lfn
        row =lax.broadcasted_iota(jnp.int32, lfn.shape, 0)
        cn = _local_cumsum(jnp.where(row < ntok, lfn, 0.0))
        trow = lax.broadcasted_iota(jnp.int32, (q.shape[0], 1), 0)
        for h in range(nh):
            hs = slice(h * dh, (h + 1) * dh)
            m, l, acc = m_s[h], l_s[h], acc_s[h]
            for sp in range(ntok):
                logit = jnp.sum(q[:, hs] * kn_ref[0, sp:sp + 1, hs], axis=-1, keepdims=True)
                logit = logit - cn[sp:sp + 1, FF_OFF + h:FF_OFF + h + 1]
                logit = jnp.where(trow >= sp, logit, NEG)
                m_new = jnp.maximum(m, logit)
                alpha = jnp.exp(m - m_new)
                p = jnp.exp(logit - m_new)
                l = alpha * l + p
                acc = alpha * acc + p * vn_ref[0, sp:sp + 1, hs]
                m = m_new
            o_ref[0, :, hs] = (acc / l).astype(o_ref.dtype)


def _paged(layer, page_table, q, kn, vn, zs, bf_row, cache_k, cache_v, cache_lft, nh, dh, ntok, npg):
    nb = q.shape[0]
    page = cache_k.shape[2]
    npages = page_table.shape[1]
    ngrp = npages // npg
    tsuf = jnp.asarray(np.tril(np.ones((page, page), np.float32), -1))

    def pidx(i):
        return lambda b, g, pt: (layer, pt[b, npages - 1 - (g * npg + i)], 0, 0)

    seq3 = lambda b, g, pt: (b, 0, 0)
    in_specs = [pl.BlockSpec((1, SUB, nh * dh), seq3)] * 3 + [
        pl.BlockSpec((1, SUB, LANES), seq3),
        pl.BlockSpec((1, LANES), lambda b, g, pt: (0, 0)),
        pl.BlockSpec((page, page), lambda b, g, pt: (0, 0))]
    in_specs += [pl.BlockSpec((None, None, page, nh * dh), pidx(i)) for i in range(npg)]
    in_specs += [pl.BlockSpec((None, None, page, nh * dh), pidx(i)) for i in range(npg)]
    in_specs += [pl.BlockSpec((None, None, SUB, page), pidx(i)) for i in range(npg)]
    grid_spec = pltpu.PrefetchScalarGridSpec(
        num_scalar_prefetch=1, grid=(nb, ngrp), in_specs=in_specs,
        out_specs=[pl.BlockSpec((1, SUB, nh * dh), seq3), pl.BlockSpec((1, SUB, LANES), seq3)],
        scratch_shapes=[pltpu.VMEM((nh, SUB, 1), F32), pltpu.VMEM((nh, SUB, 1), F32),
                        pltpu.VMEM((nh, SUB, dh), F32), pltpu.VMEM((SUB, LANES), F32)])
    return pl.pallas_call(
        functools.partial(_paged_kernel, npg=npg, nh=nh, dh=dh, ntok=ntok, scale=dh ** -0.5),
        grid_spec=grid_spec,
        out_shape=[jax.ShapeDtypeStruct((nb, SUB, nh * dh), F32),
                   jax.ShapeDtypeStruct((nb, SUB, LANES), F32)],
        compiler_params=_cparams(("parallel", "arbitrary")),
        name="paged",
    )(page_table, q, kn, vn, zs, bf_row, tsuf, *([cache_k] * npg), *([cache_v] * npg), *([cache_lft] * npg))


def _merge_kernel(x_ref, oa_ref, ob_ref, oc_ref, g1_ref, wg_ref, wa_ref, wb_ref, wc_ref, wo_ref, y_ref):
    x = x_ref[...]
    d = x.shape[1]
    xn = _rms(x, g1_ref[...]).astype(BF16)
    merged = jnp.zeros(x.shape, F32)
    for i, (o_ref, w_ref) in enumerate(((oa_ref, wa_ref), (ob_ref, wb_ref), (oc_ref, wc_ref))):
        gate = _sigmoid(jnp.dot(xn, wg_ref[:, i * d:(i + 1) * d], preferred_element_type=F32))
        merged = merged + gate * jnp.dot(o_ref[...], w_ref[...], preferred_element_type=F32)
    y_ref[...] = x + jnp.dot(merged.astype(BF16), wo_ref[...], preferred_element_type=F32)


def _merge(x, oa, ob, oc, g1, wg, wa, wb, wc, wo, tm):
    m, d = x.shape
    const = lambda a: pl.BlockSpec(a.shape, lambda i: (0,) * a.ndim)
    rows = lambda a: pl.BlockSpec((tm, a.shape[1]), lambda i: (i, 0))
    return pl.pallas_call(
        _merge_kernel,
        grid=(m // tm,),
        in_specs=[rows(x), rows(oa), rows(ob), rows(oc), const(g1), const(wg), const(wa), const(wb),
                  const(wc), const(wo)],
        out_specs=pl.BlockSpec((tm, d), lambda i: (i, 0)),
        out_shape=jax.ShapeDtypeStruct((m, d), F32),
        compiler_params=_cparams(("parallel",)),
        name="merge",
    )(x, oa, ob, oc, g1, wg, wa, wb, wc, wo)


def _ffn_kernel(x_ref, g2_ref, wg_ref, wu_ref, wd_ref, gf_ref, y_ref, *, final):
    x = x_ref[...]
    h = _rms(x, g2_ref[...]).astype(BF16)
    acc = x
    for c in range(wg_ref.shape[0]):
        a = jnp.dot(h, wg_ref[c], preferred_element_type=F32)
        u = jnp.dot(h, wu_ref[c], preferred_element_type=F32)
        acc = acc + jnp.dot((_silu(a) * u).astype(BF16), wd_ref[c], preferred_element_type=F32)
    y_ref[...] = _rms(acc, gf_ref[...]) if final else acc


def _ffn(x, g2, wg, wu, wd, gf, tm, final):
    m, d = x.shape
    const = lambda a: pl.BlockSpec(a.shape, lambda i: (0,) * a.ndim)
    return pl.pallas_call(
        functools.partial(_ffn_kernel, final=final),
        grid=(m // tm,),
        in_specs=[pl.BlockSpec((tm, d), lambda i: (i, 0)), const(g2), const(wg), const(wu), const(wd), const(gf)],
        out_specs=pl.BlockSpec((tm, d), lambda i: (i, 0)),
        out_shape=jax.ShapeDtypeStruct((m, d), F32),
        compiler_params=_cparams(("parallel",)),
        name="ffn",
    )(x, g2, wg, wu, wd, gf)


def _pick(n, pref):
    for c in pref:
        if n % c == 0:
            return c
    return n


def kernel(x_prompt, x_sample, state_gla, cache_fox_k, cache_fox_v, cache_fox_logf, state_hgrn, page_table,
           norm1_g, w_in, gla_wg2, gla_bg, gla_norm_g, fox_bf, hg_lb_logits, hg_norm_g,
           w_branch_a, w_branch_b, w_branch_c, w_out, norm2_g, w_ffn_gate, w_ffn_up, w_ffn_down,
           final_norm_g):
    depth, d_model, _ = w_in.shape
    bp, tp, _ = x_prompt.shape
    nb, ntok, _ = x_sample.shape
    _, _, gh, gdk, gdv = state_gla.shape
    _, _, hh, hdk, hdv = state_hgrn.shape
    _, n_pool, page, fh, fdh = cache_fox_k.shape
    hidden = w_ffn_gate.shape[2]
    assert gdv == LANES and hdk == LANES and hdv == LANES and fdh == LANES and 2 * gdk == LANES
    assert ntok <= SUB and gh % 2 == 0

    gq_w, gv_w, f_w, h_w = gh * gdk, gh * gdv, fh * fdh, hh * hdk
    names = ["gq", "gk", "gv", "glr", "gr", "fq", "fk", "fv", "ff", "hq", "hf", "hi", "hg", "ga", "gb", "gc"]
    widths = [gq_w, gq_w, gv_w, GLA_GATE_RANK, gv_w, f_w, f_w, f_w, fh, h_w, h_w, h_w, h_w,
              d_model, d_model, d_model]
    starts = dict(zip(names, np.concatenate([[0], np.cumsum(widths)[:-1]]).tolist()))
    wid = dict(zip(names, widths))
    order = ["gq", "gk", "gv", "gr", "fq", "fk", "fv", "hq", "hf", "hi", "hg"]
    col, off = {}, 0
    for nm in order:
        col[nm] = off // LANES
        off += wid[nm]
    col["small"] = off // LANES
    n_used = off + LANES
    tn = min(14, n_used // LANES) * LANES
    n_pad = -(-n_used // tn) * tn

    def build_w(l):
        wl = w_in[l]
        parts = [wl[:, starts[nm]:starts[nm] + wid[nm]] for nm in order]
        small = jnp.zeros((d_model, LANES), F32)
        small = small.at[:, FF_OFF:FF_OFF + fh].set(wl[:, starts["ff"]:starts["ff"] + fh])
        small = small.at[:, GLR_OFF:GLR_OFF + GLA_GATE_RANK].set(wl[:, starts["glr"]:starts["glr"] + GLA_GATE_RANK])
        parts += [small, jnp.zeros((d_model, n_pad - n_used), F32)]
        wg = wl[:, starts["ga"]:starts["ga"] + 3 * d_model]
        return jnp.concatenate(parts, axis=1).astype(BF16), wg.astype(BF16)

    lb_cum = jnp.cumsum(jax.nn.softmax(hg_lb_logits.astype(F32), axis=0), axis=0)
    hg_lb = lb_cum - lb_cum[:1]
    log_lb, log1m_lb, one_m_lb = jnp.log(hg_lb), jnp.log1p(-hg_lb), 1.0 - hg_lb

    hc = 256
    nhc = hidden // hc

    xp = x_prompt.reshape(bp * tp, d_model)
    xs = jnp.pad(x_sample, ((0, 0), (0, SUB - ntok), (0, 0))).reshape(nb * SUB, d_model)
    ms = nb * SUB

    tm_p = _pick(bp * tp, (1024, 512, 256, 128))
    tt = _pick(tp, (512, 256, 128))
    fblk = _pick(tp, (512, 256, 128))
    tblk = _pick(tp, (512, 256, 128))
    chunk = min(128, tblk)
    tm_e = _pick(bp * tp, (512, 256, 128))
    nseq = _pick(nb, (8, 4, 2, 1))
    npg = _pick(page_table.shape[1], (8, 4, 2, 1))

    ck = cache_fox_k.reshape(depth, n_pool, page, fh * fdh)
    cv = cache_fox_v.reshape(depth, n_pool, page, fh * fdh)
    clft = jnp.pad(cache_fox_logf.astype(F32).transpose(0, 1, 3, 2), ((0, 0), (0, 0), (0, SUB - fh), (0, 0)))

    outs = {k: [] for k in ("gla_p", "gla_s", "k_p", "v_p", "lf_p", "k_s", "v_s", "lf_s", "hg_p", "hg_s")}

    for l in range(depth):
        w_pad, w_gates = build_w(l)
        g1 = norm1_g[l].reshape(1, d_model)
        g2 = norm2_g[l].reshape(1, d_model)
        bf_row = jnp.zeros((1, LANES), F32).at[0, FF_OFF:FF_OFF + fh].set(fox_bf[l])
        wg2p = jnp.zeros((LANES, gq_w), F32).at[GLR_OFF:GLR_OFF + GLA_GATE_RANK].set(gla_wg2[l])
        wg2_units = wg2p.reshape(LANES, gh // 2, LANES).transpose(1, 0, 2).astype(BF16)
        bg_units = gla_bg[l].reshape(gh // 2, 1, LANES)
        gng = gla_norm_g[l].reshape(1, LANES)
        hng = hg_norm_g[l].reshape(1, LANES)
        llb_u, l1m_u, oml_u = (a[l].reshape(hh, 1, LANES) for a in (log_lb, log1m_lb, one_m_lb))
        wa, wb, wc, wo = (w[l].astype(BF16) for w in (w_branch_a, w_branch_b, w_branch_c, w_out))
        wfg = w_ffn_gate[l].reshape(d_model, nhc, hc).transpose(1, 0, 2).astype(BF16)
        wfu = w_ffn_up[l].reshape(d_model, nhc, hc).transpose(1, 0, 2).astype(BF16)
        wfd = w_ffn_down[l].reshape(nhc, hc, d_model).astype(BF16)
        gf = final_norm_g.reshape(1, d_model)
        final = l == depth - 1

        z = _proj(xp, g1, w_pad, tm_p, tn)
        fk = z[:, col["fk"] * LANES:col["fk"] * LANES + f_w]
        fv = z[:, col["fv"] * LANES:col["fv"] * LANES + f_w]
        outs["k_p"].append(fk.reshape(bp, tp, fh, fdh))
        outs["v_p"].append(fv.reshape(bp, tp, fh, fdh))
        lf, crow = _fox_prep(z, bf_row, bp, tp, col["small"], fh, tt)
        outs["lf_p"].append(lf)
        ob = _flash(z, crow, bp, tp, fh, fdh, col["fq"], col["fk"], col["fv"], fblk)
        oa, sg = _scan_gla(z, bp, tp, gh // 2, col, wg2_units, bg_units, gng, tblk, chunk)
        outs["gla_p"].append(sg)
        oc, sh = _scan_hgrn(z, bp, tp, hh, col, llb_u, l1m_u, oml_u, hng, tblk, chunk)
        outs["hg_p"].append(sh)
        x1 = _merge(xp, oa.reshape(bp * tp, -1), ob.reshape(bp * tp, -1), oc.reshape(bp * tp, -1),
                    g1, w_gates, wa, wb, wc, wo, tm_e)
        xp = _ffn(x1, g2, wfg, wfu, wfd, gf, tm_e, final)

        zs = _proj(xs, g1, w_pad, ms, tn)
        zs3 = zs.reshape(nb, SUB, n_pad)
        take = lambda nm, w: zs3[:, :, col[nm] * LANES:col[nm] * LANES + w]
        fks, fvs = take("fk", f_w), take("fv", f_w)
        outs["k_s"].append(fks[:, :ntok].reshape(nb, ntok, fh, fdh))
        outs["v_s"].append(fvs[:, :ntok].reshape(nb, ntok, fh, fdh))
        small_s = take("small", LANES)
        obs, lfs = _paged(l, page_table, take("fq", f_w), fks, fvs, small_s, bf_row, ck, cv, clft,
                          fh, fdh, ntok, npg)
        outs["lf_s"].append(lfs[:, :ntok, FF_OFF:FF_OFF + fh])
        obs = obs.astype(BF16)
        zc = zs3[:, :ntok].reshape(nb * ntok, n_pad)
        zt = zc.reshape(nb // nseq, nseq * ntok, n_pad).transpose(0, 2, 1)
        wg2t = wg2p.T.astype(BF16)
        oas, sgs = _rec("gla", zt, zc, col, (wg2t, gla_bg[l].reshape(-1, 1)), gng, state_gla[l],
                        nb, ntok, gh, gdk, nseq)
        outs["gla_s"].append(sgs)
        ocs, shs = _rec("hgrn", zt, zc, col, tuple(a[l].reshape(-1, 1) for a in (log_lb, log1m_lb, one_m_lb)),
                        hng, state_hgrn[l], nb, ntok, hh, hdk, nseq)
        outs["hg_s"].append(shs)
        pad_tok = lambda o: jnp.pad(o.reshape(nb, ntok, -1), ((0, 0), (0, SUB - ntok), (0, 0))).reshape(ms, -1)
        x1s = _merge(xs, pad_tok(oas), obs.reshape(ms, -1), pad_tok(ocs), g1, w_gates, wa, wb, wc, wo, ms)
        xs = _ffn(x1s, g2, wfg, wfu, wfd, gf, ms, final)

    st = lambda k: jnp.stack(outs[k])
    y_p = xp.reshape(bp, tp, d_model)
    y_s = xs.reshape(nb, SUB, d_model)[:, :ntok]
    return (y_p, y_s, st("gla_p"), st("gla_s"), st("k_p"), st("v_p"), st("lf_p"),
            st("k_s"), st("v_s"), st("lf_s"), st("hg_p"), st("hg_s"))
```

```python
import functools

import numpy as np
import jax
import jax.numpy as jnp
from jax import lax
from jax.experimental import pallas as pl
from jax.experimental.pallas import tpu as pltpu

F32 = jnp.float32
BF16 = jnp.bfloat16
EPS = 1e-6
NEG = -1e30
LANES = 128
SUB = 8
HI = lax.Precision.HIGHEST
VMEM_LIMIT = 56 * 1024 * 1024

FAST_BLOCK_DECAY = 60.0
GLA_UNITS_PER_STEP = 2
HGRN_UNITS_PER_STEP = 2
GLA_GATE_TEMP = 16.0
GLA_GATE_RANK = 16
FF_OFF = 0
GLR_OFF = 16


def _cparams(sem):
    return pltpu.CompilerParams(dimension_semantics=sem, vmem_limit_bytes=VMEM_LIMIT)


def _rms(x, g):
    return x * lax.rsqrt(jnp.mean(x * x, axis=-1, keepdims=True) + EPS) * g


def _sigmoid(x):
    return 1.0 / (1.0 + jnp.exp(-x))


def _log_sigmoid(x):
    return jnp.minimum(x, 0.0) - jnp.log(1.0 + jnp.exp(-jnp.abs(x)))


def _silu(x):
    return x * _sigmoid(x)


def _dot_t(a, b):
    return lax.dot_general(a, b, (((1,), (1,)), ((), ())), preferred_element_type=F32)


def _proj_kernel(x_ref, g_ref, w_ref, z_ref, xn_ref):
    @pl.when(pl.program_id(1) == 0)
    def _():
        xn_ref[...] = _rms(x_ref[...], g_ref[...]).astype(BF16)

    z_ref[...] = jnp.dot(xn_ref[...], w_ref[...], preferred_element_type=F32)


def _proj(x, g, w, tm, tn):
    m, d = x.shape
    n = w.shape[1]
    return pl.pallas_call(
        _proj_kernel,
        grid=(m // tm, n // tn),
        in_specs=[pl.BlockSpec((tm, d), lambda i, j: (i, 0)),
                  pl.BlockSpec((1, d), lambda i, j: (0, 0)),
                  pl.BlockSpec((d, tn), lambda i, j: (0, j))],
        out_specs=pl.BlockSpec((tm, tn), lambda i, j: (i, j)),
        out_shape=jax.ShapeDtypeStruct((m, n), F32),
        scratch_shapes=[pltpu.VMEM((tm, d), BF16)],
        compiler_params=_cparams(("parallel", "arbitrary")),
        name="proj",
    )(x, g, w)


def _fox_prep_kernel(zs_ref, bf_ref, tril_ref, lf_ref, c_ref, carry_ref, *, nh):
    @pl.when(pl.program_id(1) == 0)
    def _():
        carry_ref[...] = jnp.zeros_like(carry_ref)

    lf = _log_sigmoid(zs_ref[...] + bf_ref[...])
    lf_ref[0] = lf[:, FF_OFF:FF_OFF + nh]
    c = jnp.dot(tril_ref[...], lf, precision=HI, preferred_element_type=F32) + carry_ref[...][:1]
    c_ref[0] = c
    carry_ref[...] = jnp.broadcast_to(c[-1:], carry_ref.shape)


def _fox_prep(z, bf_row, b, t, small_blk, nh, tt):
    tril = np.tril(np.ones((tt, tt), np.float32))
    nt = t // tt
    return pl.pallas_call(
        functools.partial(_fox_prep_kernel, nh=nh),
        grid=(b, nt),
        in_specs=[pl.BlockSpec((tt, LANES), lambda i, j: (i * nt + j, small_blk)),
                  pl.BlockSpec((1, LANES), lambda i, j: (0, 0)),
                  pl.BlockSpec((tt, tt), lambda i, j: (0, 0))],
        out_specs=[pl.BlockSpec((1, tt, nh), lambda i, j: (i, j, 0)),
                   pl.BlockSpec((1, tt, LANES), lambda i, j: (i, j, 0))],
        out_shape=[jax.ShapeDtypeStruct((b, t, nh), F32),
                   jax.ShapeDtypeStruct((b, t, LANES), F32)],
        scratch_shapes=[pltpu.VMEM((SUB, LANES), F32)],
        compiler_params=_cparams(("parallel", "arbitrary")),
        name="fox_prep",
    )(z, bf_row, jnp.asarray(tril))


BIAS_PIECES = 3
LOG2E = 1.4426950408889634


def _flash_kernel(q_ref, k_ref, v_ref, c_ref, o_ref, kb_ref, vb_ref, sa_ref, sb_ref, *, blk, scale, nh):
    h = pl.program_id(1)
    qi = pl.program_id(2)
    dh = q_ref.shape[1]

    @pl.when(qi == 0)
    def _():
        def prep(j, _):
            rows = pl.ds(pl.multiple_of(j * blk, blk), blk)
            kb_ref[rows, :dh] = k_ref[rows, :].astype(BF16)
            vb_ref[rows, :] = v_ref[rows, :].astype(BF16)
            x = c_ref[0, rows, :] * (-LOG2E)
            lane = lax.broadcasted_iota(jnp.int32, x.shape, 1)
            extra = jnp.zeros(x.shape, F32)
            for hh in range(nh):
                rem = x[:, FF_OFF + hh:FF_OFF + hh + 1]
                for piece in range(BIAS_PIECES):
                    part = rem.astype(BF16).astype(F32)
                    extra = jnp.where(lane == hh * BIAS_PIECES + piece, part, extra)
                    rem = rem - part
            kb_ref[rows, dh:] = extra.astype(BF16)
            return 0

        lax.fori_loop(0, k_ref.shape[0] // blk, prep, 0)

    lane_q = lax.broadcasted_iota(jnp.int32, (blk, dh), 1)
    own = (lane_q >= h * BIAS_PIECES) & (lane_q < (h + 1) * BIAS_PIECES)
    q = jnp.concatenate([(q_ref[...] * (scale * LOG2E)).astype(BF16),
                         jnp.where(own, 1.0, 0.0).astype(BF16)], axis=1)

    def scores(j):
        return _dot_t(q, kb_ref[pl.ds(pl.multiple_of(j * blk, blk), blk), :])

    def absorb(s_ref, j, carry, masked):
        m, l, acc = carry
        vj = vb_ref[pl.ds(pl.multiple_of(j * blk, blk), blk), :]
        s = s_ref[...]
        if masked:
            row = lax.broadcasted_iota(jnp.int32, (blk, blk), 0)
            col = lax.broadcasted_iota(jnp.int32, (blk, blk), 1)
            s = jnp.where(col <= row, s, NEG)
        m_new = jnp.maximum(m, jnp.max(s, axis=-1, keepdims=True))
        alpha = jnp.exp2(m - m_new)
        p = jnp.exp2(s - m_new)
        l = alpha * l + jnp.sum(p, axis=-1, keepdims=True)
        acc = alpha * acc + jnp.dot(p.astype(BF16), vj, preferred_element_type=F32)
        return m_new, l, acc

    sa_ref[...] = scores(0)

    def pair(p, carry):
        j = 2 * p
        sb_ref[...] = scores(j + 1)
        carry = absorb(sa_ref, j, carry, False)
        sa_ref[...] = scores(j + 2)
        return absorb(sb_ref, j + 1, carry, False)

    init = (jnp.full((blk, 1), NEG, F32), jnp.zeros((blk, 1), F32), jnp.zeros((blk, dh), F32))
    carry = lax.fori_loop(0, qi // 2, pair, init)

    def odd_tail(carry):
        sb_ref[...] = scores(qi)
        carry = absorb(sa_ref, qi - 1, carry, False)
        return absorb(sb_ref, qi, carry, True)

    def even_tail(carry):
        return absorb(sa_ref, qi, carry, True)

    m, l, acc = lax.cond(qi % 2 == 1, odd_tail, even_tail, carry)
    o_ref[0] = (acc / l).astype(o_ref.dtype)


def _flash(z, ctok, b, t, nh, dh, q_blk0, k_blk0, v_blk0, blk):
    assert FF_OFF + nh <= LANES and nh * BIAS_PIECES <= dh
    nq = t // blk
    return pl.pallas_call(
        functools.partial(_flash_kernel, blk=blk, scale=dh ** -0.5, nh=nh),
        grid=(b, nh, nq),
        in_specs=[pl.BlockSpec((blk, dh), lambda i, h, j: (i * nq + j, q_blk0 + h)),
                  pl.BlockSpec((t, dh), lambda i, h, j: (i, k_blk0 + h)),
                  pl.BlockSpec((t, dh), lambda i, h, j: (i, v_blk0 + h)),
                  pl.BlockSpec((1, t, LANES), lambda i, h, j: (i, 0, 0))],
        out_specs=pl.BlockSpec((1, blk, dh), lambda i, h, j: (i, j, h)),
        out_shape=jax.ShapeDtypeStruct((b, t, nh * dh), BF16),
        scratch_shapes=[pltpu.VMEM((t, 2 * dh), BF16), pltpu.VMEM((t, dh), BF16),
                        pltpu.VMEM((blk, blk), F32), pltpu.VMEM((blk, blk), F32)],
        compiler_params=_cparams(("parallel", "parallel", "arbitrary")),
        name="flash",
    )(z, z, z, ctok)


def _local_cumsum(x):
    row = lax.broadcasted_iota(jnp.int32, x.shape, 0)
    for sh in (1, 2, 4):
        x = x + jnp.where(row >= sh, pltpu.roll(x, sh, 0), 0.0)
    return x


def _scan_chunk(q, k, la, vs, masks, states, ones_ws, sel, fast):
    c = q.shape[0]
    n = c // SUB
    assert n > 1
    nh = len(vs)
    sub_iota = lax.broadcasted_iota(jnp.int32, (SUB, LANES), 0)
    zero_blk = jnp.zeros((SUB, LANES), F32)

    r = [jnp.zeros((1, LANES), F32)]
    qt, kh, kt, p_rows = [], [], [], []
    for i in range(n):
        sl = slice(i * SUB, (i + 1) * SUB)
        qi, ki = q[sl], k[sl]
        li = _local_cumsum(la[sl])
        tot = li[SUB - 1:SUB]
        r.append(r[i] + tot)
        qt.append(qi * jnp.exp(li))
        if fast:
            kt.append(ki * jnp.exp(-li))
            kh.append(kt[i] * jnp.exp(tot))
        else:
            kh.append(ki * jnp.exp(tot - li))
            for t in range(SUB):
                d = jnp.where(sub_iota <= t, li[t:t + 1] - li, NEG)
                p_rows.append(jnp.exp(d) * (qi[t:t + 1] * ki))

    qbar = jnp.concatenate([qt[i] * jnp.exp(r[i]) for i in range(n)], axis=0)
    r_ends = jnp.concatenate(r[1:], axis=0)

    def rhs(i):
        g = jnp.exp(jnp.minimum(r[i] - r_ends, 0.0))
        blocks = [kh[j] * g[j:j + 1] if j + 1 < i else kh[j] for j in range(min(i, n))]
        if fast and i < n:
            blocks.append(kt[i])
        blocks += [zero_blk] * (n - len(blocks))
        return jnp.concatenate(blocks, axis=0)

    def mask(x, h):
        return x if masks[h] is None else x * masks[h]

    outs = []
    for h in range(nh):
        o = _dot_t(mask(qbar, h).astype(BF16), states[h].astype(BF16))
        if not fast:
            p_all = jnp.concatenate(p_rows, axis=0).astype(BF16)
            rr = jnp.dot(p_all, ones_ws[h], preferred_element_type=F32)
            vrep = jnp.concatenate([vs[h][i * SUB:(i + 1) * SUB] for i in range(n) for _ in range(SUB)], axis=0)
            o = o + jnp.dot(sel, (rr * vrep).astype(BF16), preferred_element_type=F32)
        outs.append(o)

    first = 0 if fast else 1
    a_rows = [[zero_blk[:, :c]] * first for _ in range(nh)]
    for i in range(first, n):
        lhs = jnp.concatenate([mask(qt[i], h) for h in range(nh)], axis=0).astype(BF16)
        a_i = _dot_t(lhs, rhs(i).astype(BF16))
        for h in range(nh):
            a_rows[h].append(a_i[h * SUB:(h + 1) * SUB])
    if fast:
        causal = lax.broadcasted_iota(jnp.int32, (c, c), 1) <= lax.broadcasted_iota(jnp.int32, (c, c), 0)
    for h in range(nh):
        a = jnp.concatenate(a_rows[h], axis=0)
        if fast:
            a = jnp.where(causal, a, 0.0)
        outs[h] = outs[h] + jnp.dot(a.astype(BF16), vs[h].astype(BF16), preferred_element_type=F32)

    k_end = rhs(n)
    new_states = []
    for h in range(nh):
        upd = lax.dot_general(vs[h].astype(BF16), mask(k_end, h).astype(BF16),
                              (((0,), (0,)), ((), ())), preferred_element_type=F32)
        new_states.append(states[h] * jnp.exp(r[n]) + upd)
    return outs, new_states


def _scan_kernel(*refs, mode, chunk, nh):
    if mode == "gla":
        (zq_ref, zk_ref, zv_ref, zs_ref, zr_ref, wg2_ref, bg_ref, ng_ref, ones_ref, sel_ref,
         o_ref, sout_ref, q_s, k_s, la_s, st_s) = refs
    else:
        (zq_ref, zf_ref, zv_ref, zr_ref, llb_ref, l1m_ref, oml_ref, ng_ref, ones_ref, sel_ref,
         o_ref, sout_ref, q_s, k_s, la_s, st_s) = refs
    ti = pl.program_id(2)
    tblk = q_s.shape[0]

    @pl.when(ti == 0)
    def _():
        st_s[...] = jnp.zeros_like(st_s)

    if mode == "gla":
        dk = LANES // nh
        q_s[...] = zq_ref[...] * dk ** -0.5
        k_s[...] = zk_ref[...]
        zg = jnp.dot(zs_ref[...].astype(BF16), wg2_ref[...], preferred_element_type=F32) + bg_ref[...]
        la_s[...] = _log_sigmoid(zg) * (1.0 / GLA_GATE_TEMP)
        lane = lax.broadcasted_iota(jnp.int32, (1, LANES), 1)
        masks = [((lane >= h * dk) & (lane < (h + 1) * dk)).astype(F32) for h in range(nh)]
    else:
        hf = zf_ref[...]
        e = jnp.exp(-jnp.abs(hf))
        l1pe = jnp.log(1.0 + e)
        inv = 1.0 / (1.0 + e)
        sig_neg = jnp.where(hf > 0, e * inv, inv)
        lsig = jnp.minimum(hf, 0.0) - l1pe
        a = llb_ref[...]
        bb = l1m_ref[...] + lsig
        la_s[...] = jnp.maximum(a, bb) + jnp.log(1.0 + jnp.exp(-jnp.abs(a - bb)))
        k_s[...] = oml_ref[...] * sig_neg
        q_s[...] = _silu(zq_ref[...])
        masks = [None]

    sel = sel_ref[...]
    ones_ws = [ones_ref[h] for h in range(nh)]

    nu = q_s.shape[1] // LANES

    def body(ci, _, fast):
        off = pl.multiple_of(ci * chunk, chunk)
        sl = pl.ds(off, chunk)
        for u in range(nu):
            ul = slice(u * LANES, (u + 1) * LANES)
            hl = [slice((u * nh + h) * LANES, (u * nh + h + 1) * LANES) for h in range(nh)]
            vs = [zv_ref[sl, hl[h]] for h in range(nh)]
            states = [st_s[u * nh + h] for h in range(nh)]
            outs, new_states = _scan_chunk(q_s[sl, ul], k_s[sl, ul], la_s[sl, ul], vs, masks, states, ones_ws,
                                           sel, fast)
            for h in range(nh):
                st_s[u * nh + h] = new_states[h]
                o = _rms(outs[h], ng_ref[...]) * _silu(zr_ref[sl, hl[h]])
                o_ref[0, sl, hl[h]] = o.astype(o_ref.dtype)
        return 0

    safe = jnp.min(la_s[...]) * SUB >= -FAST_BLOCK_DECAY

    @pl.when(safe)
    def _():
        lax.fori_loop(0, tblk // chunk, functools.partial(body, fast=True), 0)

    @pl.when(jnp.logical_not(safe))
    def _():
        lax.fori_loop(0, tblk // chunk, functools.partial(body, fast=False), 0)

    @pl.when(ti == pl.num_programs(2) - 1)
    def _():
        dk_out = LANES // nh
        for u in range(nu):
            for h in range(nh):
                sout_ref[0, u * nh + h] = st_s[u * nh + h].T[h * dk_out:(h + 1) * dk_out, :]


def _scan_consts(chunk, nh):
    n = chunk // SUB
    sel = np.zeros((chunk, n * SUB * SUB), np.float32)
    for i in range(n):
        for t in range(SUB):
            sel[i * SUB + t, i * 64 + t * SUB:i * 64 + (t + 1) * SUB] = 1.0
    ones = np.zeros((nh, LANES, LANES), np.float32)
    dk = LANES // nh
    for h in range(nh):
        ones[h, h * dk:(h + 1) * dk, :] = 1.0
    return jnp.asarray(ones, BF16), jnp.asarray(sel, BF16)


def _lane_block(col, name, width):
    assert (col[name] * LANES) % width == 0
    return col[name] * LANES // width


def _scan_gla(z, b, t, units, nu, col, wg2p, bg, ng, tblk, chunk):
    nh = 2
    nt = t // tblk
    ones, sel = _scan_consts(chunk, nh)
    kw, vw = nu * LANES, nu * nh * LANES
    zspec = lambda name, w: pl.BlockSpec((tblk, w), lambda i, u, j: (i * nt + j, _lane_block(col, name, w) + u))
    in_specs = [
        zspec("gq", kw), zspec("gk", kw), zspec("gv", vw),
        pl.BlockSpec((tblk, LANES), lambda i, u, j: (i * nt + j, col["small"])),
        zspec("gr", vw),
        pl.BlockSpec((LANES, kw), lambda i, u, j: (0, u)),
        pl.BlockSpec((1, kw), lambda i, u, j: (0, u)),
        pl.BlockSpec((1, LANES), lambda i, u, j: (0, 0)),
        pl.BlockSpec(ones.shape, lambda i, u, j: (0, 0, 0)),
        pl.BlockSpec(sel.shape, lambda i, u, j: (0, 0)),
    ]
    dk = LANES // nh
    return pl.pallas_call(
        functools.partial(_scan_kernel, mode="gla", chunk=chunk, nh=nh),
        grid=(b, units // nu, nt),
        in_specs=in_specs,
        out_specs=[pl.BlockSpec((1, tblk, vw), lambda i, u, j: (i, j, u)),
                   pl.BlockSpec((1, nu * nh, dk, LANES), lambda i, u, j: (i, u, 0, 0))],
        out_shape=[jax.ShapeDtypeStruct((b, t, units * nh * LANES), BF16),
                   jax.ShapeDtypeStruct((b, units * nh, dk, LANES), F32)],
        scratch_shapes=[pltpu.VMEM((tblk, kw), F32)] * 3 + [pltpu.VMEM((nu * nh, LANES, LANES), F32)],
        compiler_params=_cparams(("parallel", "parallel", "arbitrary")),
        name="scan_gla",
    )(z, z, z, z, z, wg2p, bg, ng, ones, sel)


def _scan_hgrn(z, b, t, units, nu, col, llb, l1m, oml, ng, tblk, chunk):
    nh = 1
    nt = t // tblk
    ones, sel = _scan_consts(chunk, nh)
    kw = nu * LANES
    zspec = lambda name: pl.BlockSpec((tblk, kw), lambda i, u, j: (i * nt + j, _lane_block(col, name, kw) + u))
    pspec = pl.BlockSpec((1, kw), lambda i, u, j: (0, u))
    in_specs = [zspec("hq"), zspec("hf"), zspec("hi"), zspec("hg"), pspec, pspec, pspec,
                pl.BlockSpec((1, LANES), lambda i, u, j: (0, 0)),
                pl.BlockSpec(ones.shape, lambda i, u, j: (0, 0, 0)),
                pl.BlockSpec(sel.shape, lambda i, u, j: (0, 0))]
    return pl.pallas_call(
        functools.partial(_scan_kernel, mode="hgrn", chunk=chunk, nh=nh),
        grid=(b, units // nu, nt),
        in_specs=in_specs,
        out_specs=[pl.BlockSpec((1, tblk, kw), lambda i, u, j: (i, j, u)),
                   pl.BlockSpec((1, nu, LANES, LANES), lambda i, u, j: (i, u, 0, 0))],
        out_shape=[jax.ShapeDtypeStruct((b, t, units * LANES), BF16),
                   jax.ShapeDtypeStruct((b, units, LANES, LANES), F32)],
        scratch_shapes=[pltpu.VMEM((tblk, kw), F32)] * 3 + [pltpu.VMEM((nu, LANES, LANES), F32)],
        compiler_params=_cparams(("parallel", "parallel", "arbitrary")),
        name="scan_hgrn",
    )(z, z, z, z, llb, l1m, oml, ng, ones, sel)


def _rec_kernel(*refs, mode, nseq, ntok, dk):
    if mode == "gla":
        (qt_ref, kt_ref, st_ref, wg2t_ref, bgt_ref, zv_ref, zr_ref, ng_ref, s0_ref, o_ref, sout_ref, o_s) = refs
        qc = qt_ref[...] * dk ** -0.5
        kc = kt_ref[...]
        zg = jnp.dot(wg2t_ref[...], st_ref[...].astype(BF16), preferred_element_type=F32) + bgt_ref[...]
        ac = jnp.exp(_log_sigmoid(zg) * (1.0 / GLA_GATE_TEMP))
    else:
        (qt_ref, ft_ref, llb_ref, l1m_ref, oml_ref, zv_ref, zr_ref, ng_ref, s0_ref, o_ref, sout_ref, o_s) = refs
        hf = ft_ref[...]
        lsig = _log_sigmoid(hf)
        a = llb_ref[...]
        bb = l1m_ref[...] + lsig
        ac = jnp.exp(jnp.maximum(a, bb) + jnp.log(1.0 + jnp.exp(-jnp.abs(a - bb))))
        kc = oml_ref[...] * _sigmoid(-hf)
        qc = _silu(qt_ref[...])
    for sq in range(nseq):
        s = s0_ref[sq, 0]
        for t in range(ntok):
            j = sq * ntok + t
            vrow = zv_ref[j:j + 1, :]
            s = s * ac[:, j:j + 1] + kc[:, j:j + 1] * vrow
            o_s[j:j + 1, :] = jnp.sum(s * qc[:, j:j + 1], axis=0, keepdims=True)
        sout_ref[sq, 0] = s
    o = _rms(o_s[...], ng_ref[...]) * _silu(zr_ref[...])
    o_ref[...] = o.astype(o_ref.dtype)


def _rec(mode, zt, z, col, params, ng, s0, nb, ntok, heads, dk, nseq):
    m = nb * ntok
    rows = nseq * ntok
    ng_groups = nb // nseq
    if mode == "gla":
        wg2t, bgt = params
        per = LANES // dk
        tspec = lambda name: pl.BlockSpec((None, dk, rows), lambda h, g: (g, col[name] * per + h, 0))
        in_specs = [tspec("gq"), tspec("gk"),
                    pl.BlockSpec((None, LANES, rows), lambda h, g: (g, col["small"], 0)),
                    pl.BlockSpec((dk, LANES), lambda h, g: (h, 0)),
                    pl.BlockSpec((dk, 1), lambda h, g: (h, 0)),
                    pl.BlockSpec((rows, LANES), lambda h, g: (g, col["gv"] + h)),
                    pl.BlockSpec((rows, LANES), lambda h, g: (g, col["gr"] + h))]
        args = (zt, zt, zt, wg2t, bgt, z, z)
    else:
        llb, l1m, oml = params
        tspec = lambda name: pl.BlockSpec((None, dk, rows), lambda h, g: (g, col[name] + h, 0))
        pspec = pl.BlockSpec((dk, 1), lambda h, g: (h, 0))
        in_specs = [tspec("hq"), tspec("hf"), pspec, pspec, pspec,
                    pl.BlockSpec((rows, LANES), lambda h, g: (g, col["hi"] + h)),
                    pl.BlockSpec((rows, LANES), lambda h, g: (g, col["hg"] + h))]
        args = (zt, zt, llb, l1m, oml, z, z)
    in_specs += [pl.BlockSpec((1, LANES), lambda h, g: (0, 0)),
                 pl.BlockSpec((nseq, 1, dk, LANES), lambda h, g: (g, h, 0, 0))]
    return pl.pallas_call(
        functools.partial(_rec_kernel, mode=mode, nseq=nseq, ntok=ntok, dk=dk),
        grid=(heads, ng_groups),
        in_specs=in_specs,
        out_specs=[pl.BlockSpec((rows, LANES), lambda h, g: (g, h)),
                   pl.BlockSpec((nseq, 1, dk, LANES), lambda h, g: (g, h, 0, 0))],
        out_shape=[jax.ShapeDtypeStruct((m, heads * LANES), BF16),
                   jax.ShapeDtypeStruct(s0.shape, F32)],
        scratch_shapes=[pltpu.VMEM((rows, LANES), F32)],
        compiler_params=_cparams(("parallel", "parallel")),
        name="rec_" + mode,
    )(*args, ng, s0)


def _lfpool_kernel(lf_ref, o_ref, *, nh):
    x = lf_ref[...]
    w = x.shape[1]
    lane = lax.broadcasted_iota(jnp.int32, x.shape, 1)
    incl = x
    sh = nh
    while sh < w:
        incl = incl + jnp.where(lane < w - sh, pltpu.roll(incl, w - sh, 1), 0.0)
        sh *= 2
    tot = jnp.where(lane < nh, incl, 0.0)
    sh = nh
    while sh < w:
        tot = tot + pltpu.roll(tot, sh, 1)
        sh *= 2
    o_ref[:, :w] = incl - x
    o_ref[:, w:] = tot


def _lfpool(clf, nh, rows):
    depth, n_pool, w = clf.shape
    return pl.pallas_call(
        functools.partial(_lfpool_kernel, nh=nh),
        grid=(depth, n_pool // rows),
        in_specs=[pl.BlockSpec((None, rows, w), lambda l, i: (l, i, 0))],
        out_specs=pl.BlockSpec((None, rows, 2 * w), lambda l, i: (l, i, 0)),
        out_shape=jax.ShapeDtypeStruct((depth, n_pool, 2 * w), F32),
        compiler_params=_cparams(("parallel", "parallel")),
        name="lfpool",
    )(clf)


def _paged_kernel(pt_ref, q_ref, kn_ref, vn_ref, zs_ref, bf_ref, mask_ref, *rest, npg, nh, dh, ntok, scale,
                  npages):
    k_refs = rest[:npg]
    v_refs = rest[npg:2 * npg]
    rt_refs = rest[2 * npg:3 * npg]
    o_ref, lf_ref = rest[3 * npg:3 * npg + 2]
    m_s, l_s, acc_s, car_s = rest[3 * npg + 2:]
    b = pl.program_id(0)
    g = pl.program_id(1)

    @pl.when(g == 0)
    def _():
        m_s[...] = jnp.full(m_s.shape, NEG, F32)
        l_s[...] = jnp.zeros_like(l_s)
        acc_s[...] = jnp.zeros_like(acc_s)
        car_s[...] = jnp.zeros_like(car_s)

    q = q_ref[0] * scale
    q_all = jnp.concatenate([q[:, h * dh:(h + 1) * dh] for h in range(nh)], axis=0).astype(BF16)
    w = mask_ref.shape[1]
    carry = car_s[...]
    s_pages = []
    for i in range(npg):
        pid = pt_ref[b, npages - 1 - (g * npg + i)]
        rt = rt_refs[i][pl.ds(pid % SUB, 1), :]
        bias = rt[:, :w] + carry
        carry = carry + rt[:, w:]
        s_pages.append(_dot_t(q_all, k_refs[i][...].astype(BF16)) + (mask_ref[...] + bias))
    car_s[...] = carry
    s = jnp.concatenate(s_pages, axis=1)
    m = m_s[...]
    m_new = jnp.maximum(m, jnp.max(s, axis=-1, keepdims=True))
    alpha = jnp.exp(m - m_new)
    p = jnp.exp(s - m_new)
    l_s[...] = alpha * l_s[...] + jnp.sum(p, axis=-1, keepdims=True)
    pv = jnp.zeros(acc_s.shape, F32)
    for i in range(npg):
        pv = pv + jnp.dot(p[:, i * w:(i + 1) * w].astype(BF16), v_refs[i][...].astype(BF16),
                          preferred_element_type=F32)
    acc_s[...] = alpha * acc_s[...] + pv
    m_s[...] = m_new

    @pl.when(g == pl.num_programs(1) - 1)
    def _():
        lfn = _log_sigmoid(zs_ref[0] + bf_ref[...])
        lf_ref[0] = lfn
        row = lax.broadcasted_iota(jnp.int32, lfn.shape, 0)
        cn = _local_cumsum(jnp.where(row < ntok, lfn, 0.0))
        trow = lax.broadcasted_iota(jnp.int32, (SUB, 1), 0)
        for h in range(nh):
            hs = slice(h * dh, (h + 1) * dh)
            rs = slice(h * SUB, (h + 1) * SUB)
            m, l, acc = m_s[rs, :], l_s[rs, :], acc_s[rs, :]
            for sp in range(ntok):
                logit = jnp.sum(q[:, hs] * kn_ref[0, sp:sp + 1, hs], axis=-1, keepdims=True)
                logit = logit - cn[sp:sp + 1, FF_OFF + h:FF_OFF + h + 1]
                logit = jnp.where(trow >= sp, logit, NEG)
                m_new = jnp.maximum(m, logit)
                alpha = jnp.exp(m - m_new)
                p = jnp.exp(logit - m_new)
                l = alpha * l + p
                acc = alpha * acc + p * vn_ref[0, sp:sp + 1, hs]
                m = m_new
            o_ref[0, :, hs] = (acc / l).astype(o_ref.dtype)


def _paged(layer, page_table, q, kn, vn, zs, bf_row, cache_k, cache_v, rtot, nh, dh, ntok, npg):
    nb = q.shape[0]
    w = cache_k.shape[2]
    npages = page_table.shape[1]
    ngrp = npages // npg
    mask = np.full((nh * SUB, w), NEG, np.float32)
    for h in range(nh):
        mask[h * SUB:(h + 1) * SUB, h::nh] = 0.0

    def page_of(b, g, pt, i):
        return pt[b, npages - 1 - (g * npg + i)]

    seq3 = lambda b, g, pt: (b, 0, 0)
    in_specs = [pl.BlockSpec((1, SUB, nh * dh), seq3)] * 3 + [
        pl.BlockSpec((1, SUB, LANES), seq3),
        pl.BlockSpec((1, LANES), lambda b, g, pt: (0, 0)),
        pl.BlockSpec(mask.shape, lambda b, g, pt: (0, 0))]
    kv_specs = [pl.BlockSpec((None, None, w, dh), lambda b, g, pt, i=i: (layer, page_of(b, g, pt, i), 0, 0))
                for i in range(npg)]
    in_specs += kv_specs + kv_specs
    in_specs += [pl.BlockSpec((None, SUB, 2 * w), lambda b, g, pt, i=i: (layer, page_of(b, g, pt, i) // SUB, 0))
                 for i in range(npg)]
    grid_spec = pltpu.PrefetchScalarGridSpec(
        num_scalar_prefetch=1, grid=(nb, ngrp), in_specs=in_specs,
        out_specs=[pl.BlockSpec((1, SUB, nh * dh), seq3), pl.BlockSpec((1, SUB, LANES), seq3)],
        scratch_shapes=[pltpu.VMEM((nh * SUB, 1), F32), pltpu.VMEM((nh * SUB, 1), F32),
                        pltpu.VMEM((nh * SUB, dh), F32), pltpu.VMEM((1, w), F32)])
    return pl.pallas_call(
        functools.partial(_paged_kernel, npg=npg, nh=nh, dh=dh, ntok=ntok, scale=dh ** -0.5, npages=npages),
        grid_spec=grid_spec,
        out_shape=[jax.ShapeDtypeStruct((nb, SUB, nh * dh), F32),
                   jax.ShapeDtypeStruct((nb, SUB, LANES), F32)],
        compiler_params=_cparams(("parallel", "arbitrary")),
        name="paged",
    )(page_table, q, kn, vn, zs, bf_row, jnp.asarray(mask), *([cache_k] * npg), *([cache_v] * npg),
      *([rtot] * npg))


def _merge_kernel(x_ref, oa_ref, ob_ref, oc_ref, g1_ref, wg_ref, wa_ref, wb_ref, wc_ref, wo_ref, y_ref):
    x = x_ref[...]
    d = x.shape[1]
    xn = _rms(x, g1_ref[...]).astype(BF16)
    merged = jnp.zeros(x.shape, F32)
    for i, (o_ref, w_ref) in enumerate(((oa_ref, wa_ref), (ob_ref, wb_ref), (oc_ref, wc_ref))):
        gate = _sigmoid(jnp.dot(xn, wg_ref[:, i * d:(i + 1) * d], preferred_element_type=F32))
        merged = merged + gate * jnp.dot(o_ref[...], w_ref[...], preferred_element_type=F32)
    y_ref[...] = x + jnp.dot(merged.astype(BF16), wo_ref[...], preferred_element_type=F32)


def _merge(x, oa, ob, oc, g1, wg, wa, wb, wc, wo, tm):
    m, d = x.shape
    const = lambda a: pl.BlockSpec(a.shape, lambda i: (0,) * a.ndim)
    rows = lambda a: pl.BlockSpec((tm, a.shape[1]), lambda i: (i, 0))
    return pl.pallas_call(
        _merge_kernel,
        grid=(m // tm,),
        in_specs=[rows(x), rows(oa), rows(ob), rows(oc), const(g1), const(wg), const(wa), const(wb),
                  const(wc), const(wo)],
        out_specs=pl.BlockSpec((tm, d), lambda i: (i, 0)),
        out_shape=jax.ShapeDtypeStruct((m, d), F32),
        compiler_params=_cparams(("parallel",)),
        name="merge",
    )(x, oa, ob, oc, g1, wg, wa, wb, wc, wo)


def _ffn_kernel(x_ref, g2_ref, wg_ref, wu_ref, wd_ref, gf_ref, y_ref, *, final):
    x = x_ref[...]
    h = _rms(x, g2_ref[...]).astype(BF16)
    acc = x
    for c in range(wg_ref.shape[0]):
        a = jnp.dot(h, wg_ref[c], preferred_element_type=F32)
        u = jnp.dot(h, wu_ref[c], preferred_element_type=F32)
        acc = acc + jnp.dot((_silu(a) * u).astype(BF16), wd_ref[c], preferred_element_type=F32)
    y_ref[...] = _rms(acc, gf_ref[...]) if final else acc


def _ffn(x, g2, wg, wu, wd, gf, tm, final):
    m, d = x.shape
    const = lambda a: pl.BlockSpec(a.shape, lambda i: (0,) * a.ndim)
    return pl.pallas_call(
        functools.partial(_ffn_kernel, final=final),
        grid=(m // tm,),
        in_specs=[pl.BlockSpec((tm, d), lambda i: (i, 0)), const(g2), const(wg), const(wu), const(wd), const(gf)],
        out_specs=pl.BlockSpec((tm, d), lambda i: (i, 0)),
        out_shape=jax.ShapeDtypeStruct((m, d), F32),
        compiler_params=_cparams(("parallel",)),
        name="ffn",
    )(x, g2, wg, wu, wd, gf)


def _pick(n, pref):
    for c in pref:
        if n % c == 0:
            return c
    return n


def kernel(x_prompt, x_sample, state_gla, cache_fox_k, cache_fox_v, cache_fox_logf, state_hgrn, page_table,
           norm1_g, w_in, gla_wg2, gla_bg, gla_norm_g, fox_bf, hg_lb_logits, hg_norm_g,
           w_branch_a, w_branch_b, w_branch_c, w_out, norm2_g, w_ffn_gate, w_ffn_up, w_ffn_down,
           final_norm_g):
    depth, d_model, _ = w_in.shape
    bp, tp, _ = x_prompt.shape
    nb, ntok, _ = x_sample.shape
    _, _, gh, gdk, gdv = state_gla.shape
    _, _, hh, hdk, hdv = state_hgrn.shape
    _, n_pool, page, fh, fdh = cache_fox_k.shape
    hidden = w_ffn_gate.shape[2]
    assert gdv == LANES and hdk == LANES and hdv == LANES and fdh == LANES and 2 * gdk == LANES
    assert ntok <= SUB and gh % 2 == 0

    gq_w, gv_w, f_w, h_w = gh * gdk, gh * gdv, fh * fdh, hh * hdk
    names = ["gq", "gk", "gv", "glr", "gr", "fq", "fk", "fv", "ff", "hq", "hf", "hi", "hg", "ga", "gb", "gc"]
    widths = [gq_w, gq_w, gv_w, GLA_GATE_RANK, gv_w, f_w, f_w, f_w, fh, h_w, h_w, h_w, h_w,
              d_model, d_model, d_model]
    starts = dict(zip(names, np.concatenate([[0], np.cumsum(widths)[:-1]]).tolist()))
    wid = dict(zip(names, widths))
    order = ["gq", "gk", "gv", "gr", "fq", "fk", "fv", "hq", "hf", "hi", "hg"]
    col, off = {}, 0
    for nm in order:
        col[nm] = off // LANES
        off += wid[nm]
    col["small"] = off // LANES
    n_used = off + LANES
    tn = min(14, n_used // LANES) * LANES
    n_pad = -(-n_used // tn) * tn

    def build_w(l):
        wl = w_in[l]
        parts = [wl[:, starts[nm]:starts[nm] + wid[nm]] for nm in order]
        small = jnp.zeros((d_model, LANES), F32)
        small = small.at[:, FF_OFF:FF_OFF + fh].set(wl[:, starts["ff"]:starts["ff"] + fh])
        small = small.at[:, GLR_OFF:GLR_OFF + GLA_GATE_RANK].set(wl[:, starts["glr"]:starts["glr"] + GLA_GATE_RANK])
        parts += [small, jnp.zeros((d_model, n_pad - n_used), F32)]
        wg = wl[:, starts["ga"]:starts["ga"] + 3 * d_model]
        return jnp.concatenate(parts, axis=1).astype(BF16), wg.astype(BF16)

    lb_cum = jnp.cumsum(jax.nn.softmax(hg_lb_logits.astype(F32), axis=0), axis=0)
    hg_lb = lb_cum - lb_cum[:1]
    log_lb, log1m_lb, one_m_lb = jnp.log(hg_lb), jnp.log1p(-hg_lb), 1.0 - hg_lb

    hc = 256
    nhc = hidden // hc

    xp = x_prompt.reshape(bp * tp, d_model)
    xs = jnp.pad(x_sample, ((0, 0), (0, SUB - ntok), (0, 0))).reshape(nb * SUB, d_model)
    ms = nb * SUB

    tm_p = _pick(bp * tp, (1024, 512, 256, 128))
    tt = _pick(tp, (512, 256, 128))
    fblk = _pick(tp, (512, 256, 128))
    tblk = _pick(tp, (512, 256, 128))
    chunk = min(128, tblk)
    tm_e = _pick(bp * tp, (512, 256, 128))
    nseq = _pick(nb, (8, 4, 2, 1))
    npg = _pick(page_table.shape[1], (8, 4, 2, 1))

    ck = cache_fox_k.reshape(depth, n_pool, page * fh, fdh)
    cv = cache_fox_v.reshape(depth, n_pool, page * fh, fdh)
    rtot = _lfpool(cache_fox_logf.astype(F32).reshape(depth, n_pool, page * fh), fh, _pick(n_pool, (256, 128, 64, 32, 16, 8)))

    outs = {k: [] for k in ("gla_p", "gla_s", "k_p", "v_p", "lf_p", "k_s", "v_s", "lf_s", "hg_p", "hg_s")}

    for l in range(depth):
        w_pad, w_gates = build_w(l)
        g1 = norm1_g[l].reshape(1, d_model)
        g2 = norm2_g[l].reshape(1, d_model)
        bf_row = jnp.zeros((1, LANES), F32).at[0, FF_OFF:FF_OFF + fh].set(fox_bf[l])
        wg2p = jnp.zeros((LANES, gq_w), F32).at[GLR_OFF:GLR_OFF + GLA_GATE_RANK].set(gla_wg2[l])
        wg2_b = wg2p.astype(BF16)
        bg_row = gla_bg[l].reshape(1, gq_w)
        gng = gla_norm_g[l].reshape(1, LANES)
        hng = hg_norm_g[l].reshape(1, LANES)
        llb_u, l1m_u, oml_u = (a[l].reshape(1, h_w) for a in (log_lb, log1m_lb, one_m_lb))
        wa, wb, wc, wo = (w[l].astype(BF16) for w in (w_branch_a, w_branch_b, w_branch_c, w_out))
        wfg = w_ffn_gate[l].reshape(d_model, nhc, hc).transpose(1, 0, 2).astype(BF16)
        wfu = w_ffn_up[l].reshape(d_model, nhc, hc).transpose(1, 0, 2).astype(BF16)
        wfd = w_ffn_down[l].reshape(nhc, hc, d_model).astype(BF16)
        gf = final_norm_g.reshape(1, d_model)
        final = l == depth - 1

        z = _proj(xp, g1, w_pad, tm_p, tn)
        fk = z[:, col["fk"] * LANES:col["fk"] * LANES + f_w]
        fv = z[:, col["fv"] * LANES:col["fv"] * LANES + f_w]
        outs["k_p"].append(fk.reshape(bp, tp, fh, fdh))
        outs["v_p"].append(fv.reshape(bp, tp, fh, fdh))
        lf, ctok = _fox_prep(z, bf_row, bp, tp, col["small"], fh, tt)
        outs["lf_p"].append(lf)
        ob = _flash(z, ctok, bp, tp, fh, fdh, col["fq"], col["fk"], col["fv"], fblk)
        oa, sg = _scan_gla(z, bp, tp, gh // 2, GLA_UNITS_PER_STEP, col, wg2_b, bg_row, gng, tblk, chunk)
        outs["gla_p"].append(sg)
        oc, sh = _scan_hgrn(z, bp, tp, hh, HGRN_UNITS_PER_STEP, col, llb_u, l1m_u, oml_u, hng, tblk, chunk)
        outs["hg_p"].append(sh)
        x1 = _merge(xp, oa.reshape(bp * tp, -1), ob.reshape(bp * tp, -1), oc.reshape(bp * tp, -1),
                    g1, w_gates, wa, wb, wc, wo, tm_e)
        xp = _ffn(x1, g2, wfg, wfu, wfd, gf, tm_e, final)

        zs = _proj(xs, g1, w_pad, ms, tn)
        zs3 = zs.reshape(nb, SUB, n_pad)
        take = lambda nm, w: zs3[:, :, col[nm] * LANES:col[nm] * LANES + w]
        fks, fvs = take("fk", f_w), take("fv", f_w)
        outs["k_s"].append(fks[:, :ntok].reshape(nb, ntok, fh, fdh))
        outs["v_s"].append(fvs[:, :ntok].reshape(nb, ntok, fh, fdh))
        small_s = take("small", LANES)
        obs, lfs = _paged(l, page_table, take("fq", f_w), fks, fvs, small_s, bf_row, ck, cv, rtot,
                          fh, fdh, ntok, npg)
        outs["lf_s"].append(lfs[:, :ntok, FF_OFF:FF_OFF + fh])
        obs = obs.astype(BF16)
        zc = zs3[:, :ntok].reshape(nb * ntok, n_pad)
        zt = zc.reshape(nb // nseq, nseq * ntok, n_pad).transpose(0, 2, 1)
        wg2t = wg2p.T.astype(BF16)
        oas, sgs = _rec("gla", zt, zc, col, (wg2t, gla_bg[l].reshape(-1, 1)), gng, state_gla[l],
                        nb, ntok, gh, gdk, nseq)
        outs["gla_s"].append(sgs)
        ocs, shs = _rec("hgrn", zt, zc, col, tuple(a[l].reshape(-1, 1) for a in (log_lb, log1m_lb, one_m_lb)),
                        hng, state_hgrn[l], nb, ntok, hh, hdk, nseq)
        outs["hg_s"].append(shs)
        pad_tok = lambda o: jnp.pad(o.reshape(nb, ntok, -1), ((0, 0), (0, SUB - ntok), (0, 0))).reshape(ms, -1)
        x1s = _merge(xs, pad_tok(oas), obs.reshape(ms, -1), pad_tok(ocs), g1, w_gates, wa, wb, wc, wo, ms)
        xs = _ffn(x1s, g2, wfg, wfu, wfd, gf, ms, final)

    st = lambda k: jnp.stack(outs[k])
    y_p = xp.reshape(bp, tp, d_model)
    y_s = xs.reshape(nb, SUB, d_model)[:, :ntok]
    return (y_p, y_s, st("gla_p"), st("gla_s"), st("k_p"), st("v_p"), st("lf_p"),
            st("k_s"), st("v_s"), st("lf_s"), st("hg_p"), st("hg_s"))
```

```python
import functools

import numpy as np
import jax
import jax.numpy as jnp
from jax import lax
from jax.experimental import pallas as pl
from jax.experimental.pallas import tpu as pltpu

F32 = jnp.float32
BF16 = jnp.bfloat16
EPS = 1e-6
NEG = -1e30
LANES = 128
SUB = 8
HI = lax.Precision.HIGHEST
VMEM_LIMIT = 56 * 1024 * 1024

FAST_BLOCK_DECAY = 60.0
GLA_UNITS_PER_STEP = 2
HGRN_UNITS_PER_STEP = 2
FFN_HIDDEN_CHUNK = 256
GLA_GATE_TEMP = 16.0
GLA_GATE_RANK = 16
FF_OFF = 0
GLR_OFF = 16


def _cparams(sem):
    return pltpu.CompilerParams(dimension_semantics=sem, vmem_limit_bytes=VMEM_LIMIT)


def _rms(x, g):
    return x * lax.rsqrt(jnp.mean(x * x, axis=-1, keepdims=True) + EPS) * g


def _sigmoid(x):
    return 1.0 / (1.0 + jnp.exp(-x))


def _log_sigmoid(x):
    return jnp.minimum(x, 0.0) - jnp.log(1.0 + jnp.exp(-jnp.abs(x)))


def _silu(x):
    return x * _sigmoid(x)


def _dot_t(a, b):
    return lax.dot_general(a, b, (((1,), (1,)), ((), ())), preferred_element_type=F32)


def _wprep_kernel(w_ref, wp_ref, wg_ref, *, segs, gate_src):
    wp_ref[...] = jnp.zeros_like(wp_ref)
    for src, dst, width in segs:
        wp_ref[:, dst:dst + width] = w_ref[:, src:src + width].astype(BF16)
    wg_ref[...] = w_ref[:, gate_src:gate_src + wg_ref.shape[1]].astype(BF16)


def _wprep(w_in, segs, gate_src, gate_w, n_pad, tr):
    depth, d, n_in = w_in.shape
    return pl.pallas_call(
        functools.partial(_wprep_kernel, segs=segs, gate_src=gate_src),
        grid=(depth, d // tr),
        in_specs=[pl.BlockSpec((None, tr, n_in), lambda l, i: (l, i, 0))],
        out_specs=[pl.BlockSpec((None, tr, n_pad), lambda l, i: (l, i, 0)),
                   pl.BlockSpec((None, tr, gate_w), lambda l, i: (l, i, 0))],
        out_shape=[jax.ShapeDtypeStruct((depth, d, n_pad), BF16),
                   jax.ShapeDtypeStruct((depth, d, gate_w), BF16)],
        compiler_params=_cparams(("parallel", "parallel")),
        name="wprep",
    )(w_in)


def _proj_kernel(x_ref, g_ref, w_ref, z_ref, *rest, kv):
    xn_ref = rest[-1]

    @pl.when(pl.program_id(1) == 0)
    def _():
        xn_ref[...] = _rms(x_ref[...], g_ref[...]).astype(BF16)

    zt = jnp.dot(xn_ref[...], w_ref[...], preferred_element_type=F32)
    z_ref[...] = zt
    if kv is not None:
        fk_ref, fv_ref = rest[:2]
        jkv, koff, voff, nh, dh = kv
        tm = zt.shape[0]

        @pl.when(pl.program_id(1) == jkv)
        def _():
            for h in range(nh):
                fk_ref[pl.ds(h, tm, stride=nh), :] = zt[:, koff + h * dh:koff + (h + 1) * dh]
                fv_ref[pl.ds(h, tm, stride=nh), :] = zt[:, voff + h * dh:voff + (h + 1) * dh]


def _proj(x, g, w_all, layer, tm, tn, kv_cols=None):
    m, d = x.shape
    n = w_all.shape[2]
    out_specs = [pl.BlockSpec((tm, tn), lambda i, j: (i, j))]
    out_shape = [jax.ShapeDtypeStruct((m, n), F32)]
    kv = None
    if kv_cols is not None:
        kc, vc, nh, dh = kv_cols
        assert kc // tn == (vc + nh * dh - 1) // tn
        kv = (kc // tn, kc % tn, vc % tn, nh, dh)
        out_specs += [pl.BlockSpec((tm * nh, dh), lambda i, j: (i, 0))] * 2
        out_shape += [jax.ShapeDtypeStruct((m * nh, dh), F32)] * 2
    return pl.pallas_call(
        functools.partial(_proj_kernel, kv=kv),
        grid=(m // tm, n // tn),
        in_specs=[pl.BlockSpec((tm, d), lambda i, j: (i, 0)),
                  pl.BlockSpec((1, d), lambda i, j: (0, 0)),
                  pl.BlockSpec((None, d, tn), lambda i, j: (layer, 0, j))],
        out_specs=out_specs,
        out_shape=out_shape,
        scratch_shapes=[pltpu.VMEM((tm, d), BF16)],
        compiler_params=_cparams(("parallel", "arbitrary")),
        name="proj",
    )(x, g, w_all)


def _fox_prep_kernel(zs_ref, bf_ref, tril_ref, lf_ref, c_ref, carry_ref, *, nh):
    @pl.when(pl.program_id(1) == 0)
    def _():
        carry_ref[...] = jnp.zeros_like(carry_ref)

    lf = _log_sigmoid(zs_ref[...] + bf_ref[...])
    lf_ref[0] = lf[:, FF_OFF:FF_OFF + nh]
    c = jnp.dot(tril_ref[...], lf, precision=HI, preferred_element_type=F32) + carry_ref[...][:1]
    c_ref[0] = c
    carry_ref[...] = jnp.broadcast_to(c[-1:], carry_ref.shape)


def _fox_prep(z, bf_row, b, t, small_blk, nh, tt):
    tril = np.tril(np.ones((tt, tt), np.float32))
    nt = t // tt
    return pl.pallas_call(
        functools.partial(_fox_prep_kernel, nh=nh),
        grid=(b, nt),
        in_specs=[pl.BlockSpec((tt, LANES), lambda i, j: (i * nt + j, small_blk)),
                  pl.BlockSpec((1, LANES), lambda i, j: (0, 0)),
                  pl.BlockSpec((tt, tt), lambda i, j: (0, 0))],
        out_specs=[pl.BlockSpec((1, tt, nh), lambda i, j: (i, j, 0)),
                   pl.BlockSpec((1, tt, LANES), lambda i, j: (i, j, 0))],
        out_shape=[jax.ShapeDtypeStruct((b, t, nh), F32),
                   jax.ShapeDtypeStruct((b, t, LANES), F32)],
        scratch_shapes=[pltpu.VMEM((SUB, LANES), F32)],
        compiler_params=_cparams(("parallel", "arbitrary")),
        name="fox_prep",
    )(z, bf_row, jnp.asarray(tril))


BIAS_PIECES = 3
LOG2E = 1.4426950408889634


def _flash_kernel(q_ref, k_ref, v_ref, c_ref, o_ref, kb_ref, vb_ref, sa_ref, sb_ref, *, blk, scale, nh):
    h = pl.program_id(1)
    qi = pl.program_id(2)
    dh = q_ref.shape[1]

    @pl.when(qi == 0)
    def _():
        def prep(j, _):
            rows = pl.ds(pl.multiple_of(j * blk, blk), blk)
            kb_ref[rows, :dh] = k_ref[rows, :].astype(BF16)
            vb_ref[rows, :] = v_ref[rows, :].astype(BF16)
            x = c_ref[0, rows, :] * (-LOG2E)
            lane = lax.broadcasted_iota(jnp.int32, x.shape, 1)
            extra = jnp.zeros(x.shape, F32)
            for hh in range(nh):
                rem = x[:, FF_OFF + hh:FF_OFF + hh + 1]
                for piece in range(BIAS_PIECES):
                    part = rem.astype(BF16).astype(F32)
                    extra = jnp.where(lane == hh * BIAS_PIECES + piece, part, extra)
                    rem = rem - part
            kb_ref[rows, dh:] = extra.astype(BF16)
            return 0

        lax.fori_loop(0, k_ref.shape[0] // blk, prep, 0)

    lane_q = lax.broadcasted_iota(jnp.int32, (blk, dh), 1)
    own = (lane_q >= h * BIAS_PIECES) & (lane_q < (h + 1) * BIAS_PIECES)
    q = jnp.concatenate([(q_ref[...] * (scale * LOG2E)).astype(BF16),
                         jnp.where(own, 1.0, 0.0).astype(BF16)], axis=1)

    def scores(j):
        return _dot_t(q, kb_ref[pl.ds(pl.multiple_of(j * blk, blk), blk), :])

    def absorb(s_ref, j, carry, masked):
        m, l, acc = carry
        vj = vb_ref[pl.ds(pl.multiple_of(j * blk, blk), blk), :]
        s = s_ref[...]
        if masked:
            row = lax.broadcasted_iota(jnp.int32, (blk, blk), 0)
            col = lax.broadcasted_iota(jnp.int32, (blk, blk), 1)
            s = jnp.where(col <= row, s, NEG)
        m_new = jnp.maximum(m, jnp.max(s, axis=-1, keepdims=True))
        alpha = jnp.exp2(m - m_new)
        p = jnp.exp2(s - m_new)
        l = alpha * l + jnp.sum(p, axis=-1, keepdims=True)
        acc = alpha * acc + jnp.dot(p.astype(BF16), vj, preferred_element_type=F32)
        return m_new, l, acc

    sa_ref[...] = scores(0)

    def pair(p, carry):
        j = 2 * p
        sb_ref[...] = scores(j + 1)
        carry = absorb(sa_ref, j, carry, False)
        sa_ref[...] = scores(j + 2)
        return absorb(sb_ref, j + 1, carry, False)

    init = (jnp.full((blk, 1), NEG, F32), jnp.zeros((blk, 1), F32), jnp.zeros((blk, dh), F32))
    carry = lax.fori_loop(0, qi // 2, pair, init)

    def odd_tail(carry):
        sb_ref[...] = scores(qi)
        carry = absorb(sa_ref, qi - 1, carry, False)
        return absorb(sb_ref, qi, carry, True)

    def even_tail(carry):
        return absorb(sa_ref, qi, carry, True)

    m, l, acc = lax.cond(qi % 2 == 1, odd_tail, even_tail, carry)
    o_ref[0] = (acc / l).astype(o_ref.dtype)


def _flash(z, ctok, b, t, nh, dh, q_blk0, k_blk0, v_blk0, blk):
    assert FF_OFF + nh <= LANES and nh * BIAS_PIECES <= dh
    nq = t // blk
    return pl.pallas_call(
        functools.partial(_flash_kernel, blk=blk, scale=dh ** -0.5, nh=nh),
        grid=(b, nh, nq),
        in_specs=[pl.BlockSpec((blk, dh), lambda i, h, j: (i * nq + j, q_blk0 + h)),
                  pl.BlockSpec((t, dh), lambda i, h, j: (i, k_blk0 + h)),
                  pl.BlockSpec((t, dh), lambda i, h, j: (i, v_blk0 + h)),
                  pl.BlockSpec((1, t, LANES), lambda i, h, j: (i, 0, 0))],
        out_specs=pl.BlockSpec((1, blk, dh), lambda i, h, j: (i, j, h)),
        out_shape=jax.ShapeDtypeStruct((b, t, nh * dh), BF16),
        scratch_shapes=[pltpu.VMEM((t, 2 * dh), BF16), pltpu.VMEM((t, dh), BF16),
                        pltpu.VMEM((blk, blk), F32), pltpu.VMEM((blk, blk), F32)],
        compiler_params=_cparams(("parallel", "parallel", "arbitrary")),
        name="flash",
    )(z, z, z, ctok)


def _local_cumsum(x):
    row = lax.broadcasted_iota(jnp.int32, x.shape, 0)
    for sh in (1, 2, 4):
        x = x + jnp.where(row >= sh, pltpu.roll(x, sh, 0), 0.0)
    return x


def _scan_chunk(q, k, la, vs, masks, states, ones_ws, sel, fast):
    c = q.shape[0]
    n = c // SUB
    assert n > 1
    nh = len(vs)
    sub_iota = lax.broadcasted_iota(jnp.int32, (SUB, LANES), 0)
    zero_blk = jnp.zeros((SUB, LANES), F32)

    r = [jnp.zeros((1, LANES), F32)]
    qt, kh, kt, p_rows = [], [], [], []
    for i in range(n):
        sl = slice(i * SUB, (i + 1) * SUB)
        qi, ki = q[sl], k[sl]
        li = _local_cumsum(la[sl])
        tot = li[SUB - 1:SUB]
        r.append(r[i] + tot)
        qt.append(qi * jnp.exp(li))
        if fast:
            kt.append(ki * jnp.exp(-li))
            kh.append(kt[i] * jnp.exp(tot))
        else:
            kh.append(ki * jnp.exp(tot - li))
            for t in range(SUB):
                d = jnp.where(sub_iota <= t, li[t:t + 1] - li, NEG)
                p_rows.append(jnp.exp(d) * (qi[t:t + 1] * ki))

    qbar = jnp.concatenate([qt[i] * jnp.exp(r[i]) for i in range(n)], axis=0)
    r_ends = jnp.concatenate(r[1:], axis=0)

    def rhs(i):
        g = jnp.exp(jnp.minimum(r[i] - r_ends, 0.0))
        blocks = [kh[j] * g[j:j + 1] if j + 1 < i else kh[j] for j in range(min(i, n))]
        if fast and i < n:
            blocks.append(kt[i])
        blocks += [zero_blk] * (n - len(blocks))
        return jnp.concatenate(blocks, axis=0)

    def mask(x, h):
        return x if masks[h] is None else x * masks[h]

    outs = []
    for h in range(nh):
        o = _dot_t(mask(qbar, h).astype(BF16), states[h].astype(BF16))
        if not fast:
            p_all = jnp.concatenate(p_rows, axis=0).astype(BF16)
            rr = jnp.dot(p_all, ones_ws[h], preferred_element_type=F32)
            vrep = jnp.concatenate([vs[h][i * SUB:(i + 1) * SUB] for i in range(n) for _ in range(SUB)], axis=0)
            o = o + jnp.dot(sel, (rr * vrep).astype(BF16), preferred_element_type=F32)
        outs.append(o)

    first = 0 if fast else 1
    a_rows = [[zero_blk[:, :c]] * first for _ in range(nh)]
    for i in range(first, n):
        lhs = jnp.concatenate([mask(qt[i], h) for h in range(nh)], axis=0).astype(BF16)
        a_i = _dot_t(lhs, rhs(i).astype(BF16))
        for h in range(nh):
            a_rows[h].append(a_i[h * SUB:(h + 1) * SUB])
    if fast:
        causal = lax.broadcasted_iota(jnp.int32, (c, c), 1) <= lax.broadcasted_iota(jnp.int32, (c, c), 0)
    for h in range(nh):
        a = jnp.concatenate(a_rows[h], axis=0)
        if fast:
            a = jnp.where(causal, a, 0.0)
        outs[h] = outs[h] + jnp.dot(a.astype(BF16), vs[h].astype(BF16), preferred_element_type=F32)

    k_end = rhs(n)
    new_states = []
    for h in range(nh):
        upd = lax.dot_general(vs[h].astype(BF16), mask(k_end, h).astype(BF16),
                              (((0,), (0,)), ((), ())), preferred_element_type=F32)
        new_states.append(states[h] * jnp.exp(r[n]) + upd)
    return outs, new_states


def _scan_kernel(*refs, mode, chunk, nh):
    if mode == "gla":
        (zq_ref, zk_ref, zv_ref, zs_ref, zr_ref, wg2_ref, bg_ref, ng_ref, ones_ref, sel_ref,
         o_ref, sout_ref, q_s, k_s, la_s, st_s) = refs
    else:
        (zq_ref, zf_ref, zv_ref, zr_ref, llb_ref, l1m_ref, oml_ref, ng_ref, ones_ref, sel_ref,
         o_ref, sout_ref, q_s, k_s, la_s, st_s) = refs
    ti = pl.program_id(2)
    tblk = q_s.shape[0]

    @pl.when(ti == 0)
    def _():
        st_s[...] = jnp.zeros_like(st_s)

    if mode == "gla":
        dk = LANES // nh
        q_s[...] = zq_ref[...] * dk ** -0.5
        k_s[...] = zk_ref[...]
        zg = jnp.dot(zs_ref[...].astype(BF16), wg2_ref[...], preferred_element_type=F32) + bg_ref[...]
        la_s[...] = _log_sigmoid(zg) * (1.0 / GLA_GATE_TEMP)
        lane = lax.broadcasted_iota(jnp.int32, (1, LANES), 1)
        masks = [((lane >= h * dk) & (lane < (h + 1) * dk)).astype(F32) for h in range(nh)]
    else:
        hf = zf_ref[...]
        e = jnp.exp(-jnp.abs(hf))
        l1pe = jnp.log(1.0 + e)
        inv = 1.0 / (1.0 + e)
        sig_neg = jnp.where(hf > 0, e * inv, inv)
        lsig = jnp.minimum(hf, 0.0) - l1pe
        a = llb_ref[...]
        bb = l1m_ref[...] + lsig
        la_s[...] = jnp.maximum(a, bb) + jnp.log(1.0 + jnp.exp(-jnp.abs(a - bb)))
        k_s[...] = oml_ref[...] * sig_neg
        q_s[...] = _silu(zq_ref[...])
        masks = [None]

    sel = sel_ref[...]
    ones_ws = [ones_ref[h] for h in range(nh)]

    nu = q_s.shape[1] // LANES

    def body(ci, _, fast):
        off = pl.multiple_of(ci * chunk, chunk)
        sl = pl.ds(off, chunk)
        for u in range(nu):
            ul = slice(u * LANES, (u + 1) * LANES)
            hl = [slice((u * nh + h) * LANES, (u * nh + h + 1) * LANES) for h in range(nh)]
            vs = [zv_ref[sl, hl[h]] for h in range(nh)]
            states = [st_s[u * nh + h] for h in range(nh)]
            outs, new_states = _scan_chunk(q_s[sl, ul], k_s[sl, ul], la_s[sl, ul], vs, masks, states, ones_ws,
                                           sel, fast)
            for h in range(nh):
                st_s[u * nh + h] = new_states[h]
                o = _rms(outs[h], ng_ref[...]) * _silu(zr_ref[sl, hl[h]])
                o_ref[0, sl, hl[h]] = o.astype(o_ref.dtype)
        return 0

    safe = jnp.min(la_s[...]) * SUB >= -FAST_BLOCK_DECAY

    @pl.when(safe)
    def _():
        lax.fori_loop(0, tblk // chunk, functools.partial(body, fast=True), 0)

    @pl.when(jnp.logical_not(safe))
    def _():
        lax.fori_loop(0, tblk // chunk, functools.partial(body, fast=False), 0)

    @pl.when(ti == pl.num_programs(2) - 1)
    def _():
        dk_out = LANES // nh
        for u in range(nu):
            for h in range(nh):
                sout_ref[0, u * nh + h] = st_s[u * nh + h].T[h * dk_out:(h + 1) * dk_out, :]


def _scan_consts(chunk, nh):
    n = chunk // SUB
    sel = np.zeros((chunk, n * SUB * SUB), np.float32)
    for i in range(n):
        for t in range(SUB):
            sel[i * SUB + t, i * 64 + t * SUB:i * 64 + (t + 1) * SUB] = 1.0
    ones = np.zeros((nh, LANES, LANES), np.float32)
    dk = LANES // nh
    for h in range(nh):
        ones[h, h * dk:(h + 1) * dk, :] = 1.0
    return jnp.asarray(ones, BF16), jnp.asarray(sel, BF16)


def _lane_block(col, name, width):
    assert (col[name] * LANES) % width == 0
    return col[name] * LANES // width


def _scan_gla(z, b, t, units, nu, col, wg2p, bg, ng, tblk, chunk):
    nh = 2
    nt = t // tblk
    ones, sel = _scan_consts(chunk, nh)
    kw, vw = nu * LANES, nu * nh * LANES
    zspec = lambda name, w: pl.BlockSpec((tblk, w), lambda i, u, j: (i * nt + j, _lane_block(col, name, w) + u))
    in_specs = [
        zspec("gq", kw), zspec("gk", kw), zspec("gv", vw),
        pl.BlockSpec((tblk, LANES), lambda i, u, j: (i * nt + j, col["small"])),
        zspec("gr", vw),
        pl.BlockSpec((LANES, kw), lambda i, u, j: (0, u)),
        pl.BlockSpec((1, kw), lambda i, u, j: (0, u)),
        pl.BlockSpec((1, LANES), lambda i, u, j: (0, 0)),
        pl.BlockSpec(ones.shape, lambda i, u, j: (0, 0, 0)),
        pl.BlockSpec(sel.shape, lambda i, u, j: (0, 0)),
    ]
    dk = LANES // nh
    return pl.pallas_call(
        functools.partial(_scan_kernel, mode="gla", chunk=chunk, nh=nh),
        grid=(b, units // nu, nt),
        in_specs=in_specs,
        out_specs=[pl.BlockSpec((1, tblk, vw), lambda i, u, j: (i, j, u)),
                   pl.BlockSpec((1, nu * nh, dk, LANES), lambda i, u, j: (i, u, 0, 0))],
        out_shape=[jax.ShapeDtypeStruct((b, t, units * nh * LANES), BF16),
                   jax.ShapeDtypeStruct((b, units * nh, dk, LANES), F32)],
        scratch_shapes=[pltpu.VMEM((tblk, kw), F32)] * 3 + [pltpu.VMEM((nu * nh, LANES, LANES), F32)],
        compiler_params=_cparams(("parallel", "parallel", "arbitrary")),
        name="scan_gla",
    )(z, z, z, z, z, wg2p, bg, ng, ones, sel)


def _scan_hgrn(z, b, t, units, nu, col, llb, l1m, oml, ng, tblk, chunk):
    nh = 1
    nt = t // tblk
    ones, sel = _scan_consts(chunk, nh)
    kw = nu * LANES
    zspec = lambda name: pl.BlockSpec((tblk, kw), lambda i, u, j: (i * nt + j, _lane_block(col, name, kw) + u))
    pspec = pl.BlockSpec((1, kw), lambda i, u, j: (0, u))
    in_specs = [zspec("hq"), zspec("hf"), zspec("hi"), zspec("hg"), pspec, pspec, pspec,
                pl.BlockSpec((1, LANES), lambda i, u, j: (0, 0)),
                pl.BlockSpec(ones.shape, lambda i, u, j: (0, 0, 0)),
                pl.BlockSpec(sel.shape, lambda i, u, j: (0, 0))]
    return pl.pallas_call(
        functools.partial(_scan_kernel, mode="hgrn", chunk=chunk, nh=nh),
        grid=(b, units // nu, nt),
        in_specs=in_specs,
        out_specs=[pl.BlockSpec((1, tblk, kw), lambda i, u, j: (i, j, u)),
                   pl.BlockSpec((1, nu, LANES, LANES), lambda i, u, j: (i, u, 0, 0))],
        out_shape=[jax.ShapeDtypeStruct((b, t, units * LANES), BF16),
                   jax.ShapeDtypeStruct((b, units, LANES, LANES), F32)],
        scratch_shapes=[pltpu.VMEM((tblk, kw), F32)] * 3 + [pltpu.VMEM((nu, LANES, LANES), F32)],
        compiler_params=_cparams(("parallel", "parallel", "arbitrary")),
        name="scan_hgrn",
    )(z, z, z, z, llb, l1m, oml, ng, ones, sel)


def _rec_kernel(*refs, mode, nseq, ntok, dk):
    if mode == "gla":
        (qt_ref, kt_ref, st_ref, wg2t_ref, bgt_ref, zv_ref, zr_ref, ng_ref, s0_ref, o_ref, sout_ref, o_s) = refs
        qc = qt_ref[...] * dk ** -0.5
        kc = kt_ref[...]
        zg = jnp.dot(wg2t_ref[...], st_ref[...].astype(BF16), preferred_element_type=F32) + bgt_ref[...]
        ac = jnp.exp(_log_sigmoid(zg) * (1.0 / GLA_GATE_TEMP))
    else:
        (qt_ref, ft_ref, llb_ref, l1m_ref, oml_ref, zv_ref, zr_ref, ng_ref, s0_ref, o_ref, sout_ref, o_s) = refs
        hf = ft_ref[...]
        lsig = _log_sigmoid(hf)
        a = llb_ref[...]
        bb = l1m_ref[...] + lsig
        ac = jnp.exp(jnp.maximum(a, bb) + jnp.log(1.0 + jnp.exp(-jnp.abs(a - bb))))
        kc = oml_ref[...] * _sigmoid(-hf)
        qc = _silu(qt_ref[...])
    for sq in range(nseq):
        s = s0_ref[sq, 0]
        for t in range(ntok):
            j = sq * ntok + t
            vrow = zv_ref[j:j + 1, :]
            s = s * ac[:, j:j + 1] + kc[:, j:j + 1] * vrow
            o_s[j:j + 1, :] = jnp.sum(s * qc[:, j:j + 1], axis=0, keepdims=True)
        sout_ref[sq, 0] = s
    o = _rms(o_s[...], ng_ref[...]) * _silu(zr_ref[...])
    o_ref[...] = o.astype(o_ref.dtype)


def _rec(mode, zt, z, col, params, ng, s0, nb, ntok, heads, dk, nseq):
    m = nb * ntok
    rows = nseq * ntok
    ng_groups = nb // nseq
    if mode == "gla":
        wg2t, bgt = params
        per = LANES // dk
        tspec = lambda name: pl.BlockSpec((None, dk, rows), lambda h, g: (g, col[name] * per + h, 0))
        in_specs = [tspec("gq"), tspec("gk"),
                    pl.BlockSpec((None, LANES, rows), lambda h, g: (g, col["small"], 0)),
                    pl.BlockSpec((dk, LANES), lambda h, g: (h, 0)),
                    pl.BlockSpec((dk, 1), lambda h, g: (h, 0)),
                    pl.BlockSpec((rows, LANES), lambda h, g: (g, col["gv"] + h)),
                    pl.BlockSpec((rows, LANES), lambda h, g: (g, col["gr"] + h))]
        args = (zt, zt, zt, wg2t, bgt, z, z)
    else:
        llb, l1m, oml = params
        tspec = lambda name: pl.BlockSpec((None, dk, rows), lambda h, g: (g, col[name] + h, 0))
        pspec = pl.BlockSpec((dk, 1), lambda h, g: (h, 0))
        in_specs = [tspec("hq"), tspec("hf"), pspec, pspec, pspec,
                    pl.BlockSpec((rows, LANES), lambda h, g: (g, col["hi"] + h)),
                    pl.BlockSpec((rows, LANES), lambda h, g: (g, col["hg"] + h))]
        args = (zt, zt, llb, l1m, oml, z, z)
    in_specs += [pl.BlockSpec((1, LANES), lambda h, g: (0, 0)),
                 pl.BlockSpec((nseq, 1, dk, LANES), lambda h, g: (g, h, 0, 0))]
    return pl.pallas_call(
        functools.partial(_rec_kernel, mode=mode, nseq=nseq, ntok=ntok, dk=dk),
        grid=(heads, ng_groups),
        in_specs=in_specs,
        out_specs=[pl.BlockSpec((rows, LANES), lambda h, g: (g, h)),
                   pl.BlockSpec((nseq, 1, dk, LANES), lambda h, g: (g, h, 0, 0))],
        out_shape=[jax.ShapeDtypeStruct((m, heads * LANES), BF16),
                   jax.ShapeDtypeStruct(s0.shape, F32)],
        scratch_shapes=[pltpu.VMEM((rows, LANES), F32)],
        compiler_params=_cparams(("parallel", "parallel")),
        name="rec_" + mode,
    )(*args, ng, s0)


def _lfpool_kernel(lf_ref, o_ref, *, nh):
    x = lf_ref[...]
    w = x.shape[1]
    lane = lax.broadcasted_iota(jnp.int32, x.shape, 1)
    incl = x
    sh = nh
    while sh < w:
        incl = incl + jnp.where(lane < w - sh, pltpu.roll(incl, w - sh, 1), 0.0)
        sh *= 2
    tot = jnp.where(lane < nh, incl, 0.0)
    sh = nh
    while sh < w:
        tot = tot + pltpu.roll(tot, sh, 1)
        sh *= 2
    o_ref[:, :w] = incl - x
    o_ref[:, w:] = tot


def _lfpool(clf, nh, rows):
    depth, n_pool, w = clf.shape
    return pl.pallas_call(
        functools.partial(_lfpool_kernel, nh=nh),
        grid=(depth, n_pool // rows),
        in_specs=[pl.BlockSpec((None, rows, w), lambda l, i: (l, i, 0))],
        out_specs=pl.BlockSpec((None, rows, 2 * w), lambda l, i: (l, i, 0)),
        out_shape=jax.ShapeDtypeStruct((depth, n_pool, 2 * w), F32),
        compiler_params=_cparams(("parallel", "parallel")),
        name="lfpool",
    )(clf)


def _paged_kernel(pt_ref, q_ref, kn_ref, vn_ref, zs_ref, bf_ref, mask_ref, *rest, npg, nh, dh, ntok, scale,
                  npages):
    k_refs = rest[:npg]
    v_refs = rest[npg:2 * npg]
    rt_refs = rest[2 * npg:3 * npg]
    o_ref, lf_ref = rest[3 * npg:3 * npg + 2]
    m_s, l_s, acc_s, car_s = rest[3 * npg + 2:]
    b = pl.program_id(0)
    g = pl.program_id(1)

    @pl.when(g == 0)
    def _():
        m_s[...] = jnp.full(m_s.shape, NEG, F32)
        l_s[...] = jnp.zeros_like(l_s)
        acc_s[...] = jnp.zeros_like(acc_s)
        car_s[...] = jnp.zeros_like(car_s)

    q = q_ref[0] * scale
    q_all = jnp.concatenate([q[:, h * dh:(h + 1) * dh] for h in range(nh)], axis=0).astype(BF16)
    w = mask_ref.shape[1]
    carry = car_s[...]
    s_pages = []
    for i in range(npg):
        pid = pt_ref[b, npages - 1 - (g * npg + i)]
        rt = rt_refs[i][pl.ds(pid % SUB, 1), :]
        bias = rt[:, :w] + carry
        carry = carry + rt[:, w:]
        s_pages.append(_dot_t(q_all, k_refs[i][...].astype(BF16)) + (mask_ref[...] + bias))
    car_s[...] = carry
    s = jnp.concatenate(s_pages, axis=1)
    m = m_s[...]
    m_new = jnp.maximum(m, jnp.max(s, axis=-1, keepdims=True))
    alpha = jnp.exp(m - m_new)
    p = jnp.exp(s - m_new)
    l_s[...] = alpha * l_s[...] + jnp.sum(p, axis=-1, keepdims=True)
    pv = jnp.zeros(acc_s.shape, F32)
    for i in range(npg):
        pv = pv + jnp.dot(p[:, i * w:(i + 1) * w].astype(BF16), v_refs[i][...].astype(BF16),
                          preferred_element_type=F32)
    acc_s[...] = alpha * acc_s[...] + pv
    m_s[...] = m_new

    @pl.when(g == pl.num_programs(1) - 1)
    def _():
        lfn = _log_sigmoid(zs_ref[0] + bf_ref[...])
        lf_ref[0] = lfn
        row = lax.broadcasted_iota(jnp.int32, lfn.shape, 0)
        cn = _local_cumsum(jnp.where(row < ntok, lfn, 0.0))
        trow = lax.broadcasted_iota(jnp.int32, (SUB, 1), 0)
        for h in range(nh):
            hs = slice(h * dh, (h + 1) * dh)
            rs = slice(h * SUB, (h + 1) * SUB)
            m, l, acc = m_s[rs, :], l_s[rs, :], acc_s[rs, :]
            for sp in range(ntok):
                logit = jnp.sum(q[:, hs] * kn_ref[0, sp:sp + 1, hs], axis=-1, keepdims=True)
                logit = logit - cn[sp:sp + 1, FF_OFF + h:FF_OFF + h + 1]
                logit = jnp.where(trow >= sp, logit, NEG)
                m_new = jnp.maximum(m, logit)
                alpha = jnp.exp(m - m_new)
                p = jnp.exp(logit - m_new)
                l = alpha * l + p
                acc = alpha * acc + p * vn_ref[0, sp:sp + 1, hs]
                m = m_new
            o_ref[0, :, hs] = (acc / l).astype(o_ref.dtype)


def _paged(layer, page_table, q, kn, vn, zs, bf_row, cache_k, cache_v, rtot, nh, dh, ntok, npg):
    nb = q.shape[0]
    w = cache_k.shape[2]
    npages = page_table.shape[1]
    ngrp = npages // npg
    mask = np.full((nh * SUB, w), NEG, np.float32)
    for h in range(nh):
        mask[h * SUB:(h + 1) * SUB, h::nh] = 0.0

    def page_of(b, g, pt, i):
        return pt[b, npages - 1 - (g * npg + i)]

    seq3 = lambda b, g, pt: (b, 0, 0)
    in_specs = [pl.BlockSpec((1, SUB, nh * dh), seq3)] * 3 + [
        pl.BlockSpec((1, SUB, LANES), seq3),
        pl.BlockSpec((1, LANES), lambda b, g, pt: (0, 0)),
        pl.BlockSpec(mask.shape, lambda b, g, pt: (0, 0))]
    kv_specs = [pl.BlockSpec((None, None, w, dh), lambda b, g, pt, i=i: (layer, page_of(b, g, pt, i), 0, 0))
                for i in range(npg)]
    in_specs += kv_specs + kv_specs
    in_specs += [pl.BlockSpec((None, SUB, 2 * w), lambda b, g, pt, i=i: (layer, page_of(b, g, pt, i) // SUB, 0))
                 for i in range(npg)]
    grid_spec = pltpu.PrefetchScalarGridSpec(
        num_scalar_prefetch=1, grid=(nb, ngrp), in_specs=in_specs,
        out_specs=[pl.BlockSpec((1, SUB, nh * dh), seq3), pl.BlockSpec((1, SUB, LANES), seq3)],
        scratch_shapes=[pltpu.VMEM((nh * SUB, 1), F32), pltpu.VMEM((nh * SUB, 1), F32),
                        pltpu.VMEM((nh * SUB, dh), F32), pltpu.VMEM((1, w), F32)])
    return pl.pallas_call(
        functools.partial(_paged_kernel, npg=npg, nh=nh, dh=dh, ntok=ntok, scale=dh ** -0.5, npages=npages),
        grid_spec=grid_spec,
        out_shape=[jax.ShapeDtypeStruct((nb, SUB, nh * dh), F32),
                   jax.ShapeDtypeStruct((nb, SUB, LANES), F32)],
        compiler_params=_cparams(("parallel", "arbitrary")),
        name="paged",
    )(page_table, q, kn, vn, zs, bf_row, jnp.asarray(mask), *([cache_k] * npg), *([cache_v] * npg),
      *([rtot] * npg))


def _merge_kernel(x_ref, oa_ref, ob_ref, oc_ref, g1_ref, wg_ref, wa_ref, wb_ref, wc_ref, wo_ref, y_ref):
    x = x_ref[...]
    d = x.shape[1]
    xn = _rms(x, g1_ref[...]).astype(BF16)
    merged = jnp.zeros(x.shape, F32)
    for i, (o_ref, w_ref) in enumerate(((oa_ref, wa_ref), (ob_ref, wb_ref), (oc_ref, wc_ref))):
        gate = _sigmoid(jnp.dot(xn, wg_ref[:, i * d:(i + 1) * d], preferred_element_type=F32))
        merged = merged + gate * jnp.dot(o_ref[...], w_ref[...], preferred_element_type=F32)
    y_ref[...] = x + jnp.dot(merged.astype(BF16), wo_ref[...], preferred_element_type=F32)


def _merge(x, oa, ob, oc, g1, wg_all, layer, wa, wb, wc, wo, tm):
    m, d = x.shape
    const = lambda a: pl.BlockSpec(a.shape, lambda i: (0,) * a.ndim)
    rows = lambda a: pl.BlockSpec((tm, a.shape[1]), lambda i: (i, 0))
    return pl.pallas_call(
        _merge_kernel,
        grid=(m // tm,),
        in_specs=[rows(x), rows(oa), rows(ob), rows(oc), const(g1),
                  pl.BlockSpec((None,) + wg_all.shape[1:], lambda i: (layer, 0, 0)),
                  const(wa), const(wb), const(wc), const(wo)],
        out_specs=pl.BlockSpec((tm, d), lambda i: (i, 0)),
        out_shape=jax.ShapeDtypeStruct((m, d), F32),
        compiler_params=_cparams(("parallel",)),
        name="merge",
    )(x, oa, ob, oc, g1, wg_all, wa, wb, wc, wo)


def _ffn_kernel(x_ref, g2_ref, wg_ref, wu_ref, wd_ref, gf_ref, y_ref, *, final):
    x = x_ref[...]
    h = _rms(x, g2_ref[...]).astype(BF16)
    acc = x
    hidden = wg_ref.shape[1]
    for c0 in range(0, hidden, FFN_HIDDEN_CHUNK):
        cs = slice(c0, min(c0 + FFN_HIDDEN_CHUNK, hidden))
        a = jnp.dot(h, wg_ref[:, cs], preferred_element_type=F32)
        u = jnp.dot(h, wu_ref[:, cs], preferred_element_type=F32)
        acc = acc + jnp.dot((_silu(a) * u).astype(BF16), wd_ref[cs, :], preferred_element_type=F32)
    y_ref[...] = _rms(acc, gf_ref[...]) if final else acc


def _ffn(x, g2, wg, wu, wd, gf, tm, final):
    m, d = x.shape
    const = lambda a: pl.BlockSpec(a.shape, lambda i: (0,) * a.ndim)
    return pl.pallas_call(
        functools.partial(_ffn_kernel, final=final),
        grid=(m // tm,),
        in_specs=[pl.BlockSpec((tm, d), lambda i: (i, 0)), const(g2), const(wg), const(wu), const(wd), const(gf)],
        out_specs=pl.BlockSpec((tm, d), lambda i: (i, 0)),
        out_shape=jax.ShapeDtypeStruct((m, d), F32),
        compiler_params=_cparams(("parallel",)),
        name="ffn",
    )(x, g2, wg, wu, wd, gf)


def _pick(n, pref):
    for c in pref:
        if n % c == 0:
            return c
    return n


def kernel(x_prompt, x_sample, state_gla, cache_fox_k, cache_fox_v, cache_fox_logf, state_hgrn, page_table,
           norm1_g, w_in, gla_wg2, gla_bg, gla_norm_g, fox_bf, hg_lb_logits, hg_norm_g,
           w_branch_a, w_branch_b, w_branch_c, w_out, norm2_g, w_ffn_gate, w_ffn_up, w_ffn_down,
           final_norm_g):
    depth, d_model, _ = w_in.shape
    bp, tp, _ = x_prompt.shape
    nb, ntok, _ = x_sample.shape
    _, _, gh, gdk, gdv = state_gla.shape
    _, _, hh, hdk, hdv = state_hgrn.shape
    _, n_pool, page, fh, fdh = cache_fox_k.shape
    hidden = w_ffn_gate.shape[2]
    assert gdv == LANES and hdk == LANES and hdv == LANES and fdh == LANES and 2 * gdk == LANES
    assert ntok <= SUB and gh % 2 == 0

    gq_w, gv_w, f_w, h_w = gh * gdk, gh * gdv, fh * fdh, hh * hdk
    names = ["gq", "gk", "gv", "glr", "gr", "fq", "fk", "fv", "ff", "hq", "hf", "hi", "hg", "ga", "gb", "gc"]
    widths = [gq_w, gq_w, gv_w, GLA_GATE_RANK, gv_w, f_w, f_w, f_w, fh, h_w, h_w, h_w, h_w,
              d_model, d_model, d_model]
    starts = dict(zip(names, np.concatenate([[0], np.cumsum(widths)[:-1]]).tolist()))
    wid = dict(zip(names, widths))
    order = ["gq", "gk", "gv", "gr", "fq", "fk", "fv", "hq", "hf", "hi", "hg"]
    col, off = {}, 0
    for nm in order:
        col[nm] = off // LANES
        off += wid[nm]
    col["small"] = off // LANES
    n_used = off + LANES
    tn = min(14, n_used // LANES) * LANES
    n_pad = -(-n_used // tn) * tn

    segs = [(starts[nm], col[nm] * LANES, wid[nm]) for nm in order]
    segs += [(starts["ff"], col["small"] * LANES + FF_OFF, fh),
             (starts["glr"], col["small"] * LANES + GLR_OFF, GLA_GATE_RANK)]
    w_pad_all, w_gates_all = _wprep(w_in, segs, starts["ga"], 3 * d_model, n_pad, _pick(d_model, (128, 64, 32, 16, 8)))

    lb_cum = jnp.cumsum(jax.nn.softmax(hg_lb_logits.astype(F32), axis=0), axis=0)
    hg_lb = lb_cum - lb_cum[:1]
    log_lb, log1m_lb, one_m_lb = jnp.log(hg_lb), jnp.log1p(-hg_lb), 1.0 - hg_lb


    xp = x_prompt.reshape(bp * tp, d_model)
    xs = jnp.pad(x_sample, ((0, 0), (0, SUB - ntok), (0, 0))).reshape(nb * SUB, d_model)
    ms = nb * SUB

    tm_p = _pick(bp * tp, (1024, 512, 256, 128))
    tt = _pick(tp, (512, 256, 128))
    fblk = _pick(tp, (512, 256, 128))
    tblk = _pick(tp, (512, 256, 128))
    chunk = min(128, tblk)
    tm_e = _pick(bp * tp, (512, 256, 128))
    nseq = _pick(nb, (8, 4, 2, 1))
    npg = _pick(page_table.shape[1], (16, 8, 4, 2, 1))

    ck = cache_fox_k.reshape(depth, n_pool, page * fh, fdh)
    cv = cache_fox_v.reshape(depth, n_pool, page * fh, fdh)
    rtot = _lfpool(cache_fox_logf.astype(F32).reshape(depth, n_pool, page * fh), fh, _pick(n_pool, (256, 128, 64, 32, 16, 8)))

    outs = {k: [] for k in ("gla_p", "gla_s", "k_p", "v_p", "lf_p", "k_s", "v_s", "lf_s", "hg_p", "hg_s")}

    for l in range(depth):
        g1 = norm1_g[l].reshape(1, d_model)
        g2 = norm2_g[l].reshape(1, d_model)
        bf_row = jnp.zeros((1, LANES), F32).at[0, FF_OFF:FF_OFF + fh].set(fox_bf[l])
        wg2p = jnp.zeros((LANES, gq_w), F32).at[GLR_OFF:GLR_OFF + GLA_GATE_RANK].set(gla_wg2[l])
        wg2_b = wg2p.astype(BF16)
        bg_row = gla_bg[l].reshape(1, gq_w)
        gng = gla_norm_g[l].reshape(1, LANES)
        hng = hg_norm_g[l].reshape(1, LANES)
        llb_u, l1m_u, oml_u = (a[l].reshape(1, h_w) for a in (log_lb, log1m_lb, one_m_lb))
        wa, wb, wc, wo = (w[l].astype(BF16) for w in (w_branch_a, w_branch_b, w_branch_c, w_out))
        wfg, wfu, wfd = (w[l].astype(BF16) for w in (w_ffn_gate, w_ffn_up, w_ffn_down))
        gf = final_norm_g.reshape(1, d_model)
        final = l == depth - 1

        z, fk, fv = _proj(xp, g1, w_pad_all, l, tm_p, tn, (col["fk"] * LANES, col["fv"] * LANES, fh, fdh))
        outs["k_p"].append(fk.reshape(bp, tp, fh, fdh))
        outs["v_p"].append(fv.reshape(bp, tp, fh, fdh))
        lf, ctok = _fox_prep(z, bf_row, bp, tp, col["small"], fh, tt)
        outs["lf_p"].append(lf)
        ob = _flash(z, ctok, bp, tp, fh, fdh, col["fq"], col["fk"], col["fv"], fblk)
        oa, sg = _scan_gla(z, bp, tp, gh // 2, GLA_UNITS_PER_STEP, col, wg2_b, bg_row, gng, tblk, chunk)
        outs["gla_p"].append(sg)
        oc, sh = _scan_hgrn(z, bp, tp, hh, HGRN_UNITS_PER_STEP, col, llb_u, l1m_u, oml_u, hng, tblk, chunk)
        outs["hg_p"].append(sh)
        x1 = _merge(xp, oa.reshape(bp * tp, -1), ob.reshape(bp * tp, -1), oc.reshape(bp * tp, -1),
                    g1, w_gates_all, l, wa, wb, wc, wo, tm_e)
        xp = _ffn(x1, g2, wfg, wfu, wfd, gf, tm_e, final)

        zs, = _proj(xs, g1, w_pad_all, l, ms, tn)
        zs3 = zs.reshape(nb, SUB, n_pad)
        take = lambda nm, w: zs3[:, :, col[nm] * LANES:col[nm] * LANES + w]
        fks, fvs = take("fk", f_w), take("fv", f_w)
        outs["k_s"].append(fks[:, :ntok].reshape(nb, ntok, fh, fdh))
        outs["v_s"].append(fvs[:, :ntok].reshape(nb, ntok, fh, fdh))
        small_s = take("small", LANES)
        obs, lfs = _paged(l, page_table, take("fq", f_w), fks, fvs, small_s, bf_row, ck, cv, rtot,
                          fh, fdh, ntok, npg)
        outs["lf_s"].append(lfs[:, :ntok, FF_OFF:FF_OFF + fh])
        obs = obs.astype(BF16)
        zc = zs3[:, :ntok].reshape(nb * ntok, n_pad)
        zt = zc.reshape(nb // nseq, nseq * ntok, n_pad).transpose(0, 2, 1)
        wg2t = wg2p.T.astype(BF16)
        oas, sgs = _rec("gla", zt, zc, col, (wg2t, gla_bg[l].reshape(-1, 1)), gng, state_gla[l],
                        nb, ntok, gh, gdk, nseq)
        outs["gla_s"].append(sgs)
        ocs, shs = _rec("hgrn", zt, zc, col, tuple(a[l].reshape(-1, 1) for a in (log_lb, log1m_lb, one_m_lb)),
                        hng, state_hgrn[l], nb, ntok, hh, hdk, nseq)
        outs["hg_s"].append(shs)
        pad_tok = lambda o: jnp.pad(o.reshape(nb, ntok, -1), ((0, 0), (0, SUB - ntok), (0, 0))).reshape(ms, -1)
        x1s = _merge(xs, pad_tok(oas), obs.reshape(ms, -1), pad_tok(ocs), g1, w_gates_all, l, wa, wb, wc, wo, ms)
        xs = _ffn(x1s, g2, wfg, wfu, wfd, gf, ms, final)

    st = lambda k: jnp.stack(outs[k])
    y_p = xp.reshape(bp, tp, d_model)
    y_s = xs.reshape(nb, SUB, d_model)[:, :ntok]
    return (y_p, y_s, st("gla_p"), st("gla_s"), st("k_p"), st("v_p"), st("lf_p"),
            st("k_s"), st("v_s"), st("lf_s"), st("hg_p"), st("hg_s"))
```

```python
import functools

import numpy as np
import jax
import jax.numpy as jnp
from jax import lax
from jax.experimental import pallas as pl
from jax.experimental.pallas import tpu as pltpu

F32 = jnp.float32
BF16 = jnp.bfloat16
EPS = 1e-6
NEG = -1e30
LANES = 128
SUB = 8
HI = lax.Precision.HIGHEST
VMEM_LIMIT = 56 * 1024 * 1024

FAST_BLOCK_DECAY = 60.0
GLA_UNITS_PER_STEP = 2
HGRN_UNITS_PER_STEP = 4
FFN_HIDDEN_CHUNK = 256
GLA_GATE_TEMP = 16.0
GLA_GATE_RANK = 16
FF_OFF = 0
GLR_OFF = 16


def _cparams(sem):
    return pltpu.CompilerParams(dimension_semantics=sem, vmem_limit_bytes=VMEM_LIMIT)


def _rms(x, g):
    return x * lax.rsqrt(jnp.mean(x * x, axis=-1, keepdims=True) + EPS) * g


def _sigmoid(x):
    return 1.0 / (1.0 + jnp.exp(-x))


def _log_sigmoid(x):
    return jnp.minimum(x, 0.0) - jnp.log(1.0 + jnp.exp(-jnp.abs(x)))


def _silu(x):
    return x * _sigmoid(x)


def _dot_t(a, b):
    return lax.dot_general(a, b, (((1,), (1,)), ((), ())), preferred_element_type=F32)


def _wprep_kernel(w_ref, wp_ref, wg_ref, *, segs, gate_src):
    wp_ref[...] = jnp.zeros_like(wp_ref)
    for src, dst, width in segs:
        wp_ref[:, dst:dst + width] = w_ref[:, src:src + width].astype(BF16)
    wg_ref[...] = w_ref[:, gate_src:gate_src + wg_ref.shape[1]].astype(BF16)


def _wprep(w_in, segs, gate_src, gate_w, n_pad, tr):
    depth, d, n_in = w_in.shape
    return pl.pallas_call(
        functools.partial(_wprep_kernel, segs=segs, gate_src=gate_src),
        grid=(depth, d // tr),
        in_specs=[pl.BlockSpec((None, tr, n_in), lambda l, i: (l, i, 0))],
        out_specs=[pl.BlockSpec((None, tr, n_pad), lambda l, i: (l, i, 0)),
                   pl.BlockSpec((None, tr, gate_w), lambda l, i: (l, i, 0))],
        out_shape=[jax.ShapeDtypeStruct((depth, d, n_pad), BF16),
                   jax.ShapeDtypeStruct((depth, d, gate_w), BF16)],
        compiler_params=_cparams(("parallel", "parallel")),
        name="wprep",
    )(w_in)


def _proj_kernel(x_ref, g_ref, w_ref, z_ref, *rest, kv):
    xn_ref = rest[-1]

    @pl.when(pl.program_id(1) == 0)
    def _():
        xn_ref[...] = _rms(x_ref[...], g_ref[...]).astype(BF16)

    zt = jnp.dot(xn_ref[...], w_ref[...], preferred_element_type=F32)
    z_ref[...] = zt
    if kv is not None:
        fk_ref, fv_ref = rest[:2]
        jkv, koff, voff, nh, dh = kv
        tm = zt.shape[0]

        @pl.when(pl.program_id(1) == jkv)
        def _():
            for h in range(nh):
                fk_ref[pl.ds(h, tm, stride=nh), :] = zt[:, koff + h * dh:koff + (h + 1) * dh]
                fv_ref[pl.ds(h, tm, stride=nh), :] = zt[:, voff + h * dh:voff + (h + 1) * dh]


def _proj(x, g, w_all, layer, tm, tn, kv_cols=None):
    m, d = x.shape
    n = w_all.shape[2]
    out_specs = [pl.BlockSpec((tm, tn), lambda i, j: (i, j))]
    out_shape = [jax.ShapeDtypeStruct((m, n), F32)]
    kv = None
    if kv_cols is not None:
        kc, vc, nh, dh = kv_cols
        assert kc // tn == (vc + nh * dh - 1) // tn
        kv = (kc // tn, kc % tn, vc % tn, nh, dh)
        out_specs += [pl.BlockSpec((tm * nh, dh), lambda i, j: (i, 0))] * 2
        out_shape += [jax.ShapeDtypeStruct((m * nh, dh), F32)] * 2
    return pl.pallas_call(
        functools.partial(_proj_kernel, kv=kv),
        grid=(m // tm, n // tn),
        in_specs=[pl.BlockSpec((tm, d), lambda i, j: (i, 0)),
                  pl.BlockSpec((1, d), lambda i, j: (0, 0)),
                  pl.BlockSpec((None, d, tn), lambda i, j: (layer, 0, j))],
        out_specs=out_specs,
        out_shape=out_shape,
        scratch_shapes=[pltpu.VMEM((tm, d), BF16)],
        compiler_params=_cparams(("parallel", "arbitrary")),
        name="proj",
    )(x, g, w_all)


def _fox_prep_kernel(zs_ref, bf_ref, tril_ref, lf_ref, c_ref, carry_ref, *, nh):
    @pl.when(pl.program_id(1) == 0)
    def _():
        carry_ref[...] = jnp.zeros_like(carry_ref)

    lf = _log_sigmoid(zs_ref[...] + bf_ref[...])
    lf_ref[0] = lf[:, FF_OFF:FF_OFF + nh]
    c = jnp.dot(tril_ref[...], lf, precision=HI, preferred_element_type=F32) + carry_ref[...][:1]
    c_ref[0] = c
    carry_ref[...] = jnp.broadcast_to(c[-1:], carry_ref.shape)


def _fox_prep(z, bf_row, b, t, small_blk, nh, tt):
    tril = np.tril(np.ones((tt, tt), np.float32))
    nt = t // tt
    return pl.pallas_call(
        functools.partial(_fox_prep_kernel, nh=nh),
        grid=(b, nt),
        in_specs=[pl.BlockSpec((tt, LANES), lambda i, j: (i * nt + j, small_blk)),
                  pl.BlockSpec((1, LANES), lambda i, j: (0, 0)),
                  pl.BlockSpec((tt, tt), lambda i, j: (0, 0))],
        out_specs=[pl.BlockSpec((1, tt, nh), lambda i, j: (i, j, 0)),
                   pl.BlockSpec((1, tt, LANES), lambda i, j: (i, j, 0))],
        out_shape=[jax.ShapeDtypeStruct((b, t, nh), F32),
                   jax.ShapeDtypeStruct((b, t, LANES), F32)],
        scratch_shapes=[pltpu.VMEM((SUB, LANES), F32)],
        compiler_params=_cparams(("parallel", "arbitrary")),
        name="fox_prep",
    )(z, bf_row, jnp.asarray(tril))


BIAS_PIECES = 3
LOG2E = 1.4426950408889634


def _flash_kernel(q_ref, k_ref, v_ref, c_ref, o_ref, kb_ref, vt_ref, sa_ref, sb_ref, *, blk, scale, nh):
    h = pl.program_id(1)
    qi = pl.program_id(2)
    dh = q_ref.shape[1]

    @pl.when(qi == 0)
    def _():
        def prep(j, _):
            rows = pl.ds(pl.multiple_of(j * blk, blk), blk)
            kb_ref[rows, :dh] = k_ref[rows, :].astype(BF16)
            vt_ref[j] = v_ref[rows, :].T.astype(BF16)
            x = c_ref[0, rows, :] * (-LOG2E)
            lane = lax.broadcasted_iota(jnp.int32, x.shape, 1)
            extra = jnp.zeros(x.shape, F32)
            for hh in range(nh):
                rem = x[:, FF_OFF + hh:FF_OFF + hh + 1]
                for piece in range(BIAS_PIECES):
                    part = rem.astype(BF16).astype(F32)
                    extra = jnp.where(lane == hh * BIAS_PIECES + piece, part, extra)
                    rem = rem - part
            kb_ref[rows, dh:] = extra.astype(BF16)
            return 0

        lax.fori_loop(0, k_ref.shape[0] // blk, prep, 0)

    lane_q = lax.broadcasted_iota(jnp.int32, (blk, dh), 1)
    own = (lane_q >= h * BIAS_PIECES) & (lane_q < (h + 1) * BIAS_PIECES)
    q = jnp.concatenate([(q_ref[...] * (scale * LOG2E)).astype(BF16),
                         jnp.where(own, 1.0, 0.0).astype(BF16)], axis=1)

    def scores(j):
        return _dot_t(kb_ref[pl.ds(pl.multiple_of(j * blk, blk), blk), :], q)

    def absorb(s_ref, j, carry, masked):
        m, l, acc = carry
        s = s_ref[...]
        if masked:
            key = lax.broadcasted_iota(jnp.int32, (blk, blk), 0)
            qry = lax.broadcasted_iota(jnp.int32, (blk, blk), 1)
            s = jnp.where(key <= qry, s, NEG)
        m_new = jnp.maximum(m, jnp.max(s, axis=0, keepdims=True))
        alpha = jnp.exp2(m - m_new)
        p = jnp.exp2(s - m_new)
        l = alpha * l + jnp.sum(p, axis=0, keepdims=True)
        acc = alpha * acc + jnp.dot(vt_ref[j], p.astype(BF16), preferred_element_type=F32)
        return m_new, l, acc

    sa_ref[...] = scores(0)

    def pair(p, carry):
        j = 2 * p
        sb_ref[...] = scores(j + 1)
        carry = absorb(sa_ref, j, carry, False)
        sa_ref[...] = scores(j + 2)
        return absorb(sb_ref, j + 1, carry, False)

    init = (jnp.full((1, blk), NEG, F32), jnp.zeros((1, blk), F32), jnp.zeros((dh, blk), F32))
    carry = lax.fori_loop(0, qi // 2, pair, init)

    def odd_tail(carry):
        sb_ref[...] = scores(qi)
        carry = absorb(sa_ref, qi - 1, carry, False)
        return absorb(sb_ref, qi, carry, True)

    def even_tail(carry):
        return absorb(sa_ref, qi, carry, True)

    m, l, acc = lax.cond(qi % 2 == 1, odd_tail, even_tail, carry)
    o_ref[0] = (acc / l).T.astype(o_ref.dtype)


def _flash(z, ctok, b, t, nh, dh, q_blk0, k_blk0, v_blk0, blk):
    assert FF_OFF + nh <= LANES and nh * BIAS_PIECES <= dh
    nq = t // blk
    return pl.pallas_call(
        functools.partial(_flash_kernel, blk=blk, scale=dh ** -0.5, nh=nh),
        grid=(b, nh, nq),
        in_specs=[pl.BlockSpec((blk, dh), lambda i, h, j: (i * nq + j, q_blk0 + h)),
                  pl.BlockSpec((t, dh), lambda i, h, j: (i, k_blk0 + h)),
                  pl.BlockSpec((t, dh), lambda i, h, j: (i, v_blk0 + h)),
                  pl.BlockSpec((1, t, LANES), lambda i, h, j: (i, 0, 0))],
        out_specs=pl.BlockSpec((1, blk, dh), lambda i, h, j: (i, j, h)),
        out_shape=jax.ShapeDtypeStruct((b, t, nh * dh), BF16),
        scratch_shapes=[pltpu.VMEM((t, 2 * dh), BF16), pltpu.VMEM((nq, dh, blk), BF16),
                        pltpu.VMEM((blk, blk), F32), pltpu.VMEM((blk, blk), F32)],
        compiler_params=_cparams(("parallel", "parallel", "arbitrary")),
        name="flash",
    )(z, z, z, ctok)


def _local_cumsum(x):
    row = lax.broadcasted_iota(jnp.int32, x.shape, 0)
    for sh in (1, 2, 4):
        x = x + jnp.where(row >= sh, pltpu.roll(x, sh, 0), 0.0)
    return x


def _scan_chunk(q, k, la, vs, masks, states, ones_ws, sel, fast):
    c = q.shape[0]
    n = c // SUB
    assert n > 1
    nh = len(vs)
    sub_iota = lax.broadcasted_iota(jnp.int32, (SUB, LANES), 0)
    zero_blk = jnp.zeros((SUB, LANES), F32)

    r = [jnp.zeros((1, LANES), F32)]
    qt, kh, kt, p_rows = [], [], [], []
    for i in range(n):
        sl = slice(i * SUB, (i + 1) * SUB)
        qi, ki = q[sl], k[sl]
        li = _local_cumsum(la[sl])
        tot = li[SUB - 1:SUB]
        r.append(r[i] + tot)
        qt.append(qi * jnp.exp(li))
        if fast:
            kt.append(ki * jnp.exp(-li))
            kh.append(kt[i] * jnp.exp(tot))
        else:
            kh.append(ki * jnp.exp(tot - li))
            for t in range(SUB):
                d = jnp.where(sub_iota <= t, li[t:t + 1] - li, NEG)
                p_rows.append(jnp.exp(d) * (qi[t:t + 1] * ki))

    qbar = jnp.concatenate([qt[i] * jnp.exp(r[i]) for i in range(n)], axis=0)
    r_ends = jnp.concatenate(r[1:], axis=0)

    def rhs(i):
        g = jnp.exp(jnp.minimum(r[i] - r_ends, 0.0))
        blocks = [kh[j] * g[j:j + 1] if j + 1 < i else kh[j] for j in range(min(i, n))]
        if fast and i < n:
            blocks.append(kt[i])
        blocks += [zero_blk] * (n - len(blocks))
        return jnp.concatenate(blocks, axis=0)

    def mask(x, h):
        return x if masks[h] is None else x * masks[h]

    outs = []
    for h in range(nh):
        o = _dot_t(mask(qbar, h).astype(BF16), states[h].astype(BF16))
        if not fast:
            p_all = jnp.concatenate(p_rows, axis=0).astype(BF16)
            rr = jnp.dot(p_all, ones_ws[h], preferred_element_type=F32)
            vrep = jnp.concatenate([vs[h][i * SUB:(i + 1) * SUB] for i in range(n) for _ in range(SUB)], axis=0)
            o = o + jnp.dot(sel, (rr * vrep).astype(BF16), preferred_element_type=F32)
        outs.append(o)

    first = 0 if fast else 1
    a_rows = [[zero_blk[:, :c]] * first for _ in range(nh)]
    for i in range(first, n):
        lhs = jnp.concatenate([mask(qt[i], h) for h in range(nh)], axis=0).astype(BF16)
        a_i = _dot_t(lhs, rhs(i).astype(BF16))
        for h in range(nh):
            a_rows[h].append(a_i[h * SUB:(h + 1) * SUB])
    if fast:
        causal = lax.broadcasted_iota(jnp.int32, (c, c), 1) <= lax.broadcasted_iota(jnp.int32, (c, c), 0)
    for h in range(nh):
        a = jnp.concatenate(a_rows[h], axis=0)
        if fast:
            a = jnp.where(causal, a, 0.0)
        outs[h] = outs[h] + jnp.dot(a.astype(BF16), vs[h].astype(BF16), preferred_element_type=F32)

    k_end = rhs(n)
    new_states = []
    for h in range(nh):
        upd = lax.dot_general(vs[h].astype(BF16), mask(k_end, h).astype(BF16),
                              (((0,), (0,)), ((), ())), preferred_element_type=F32)
        new_states.append(states[h] * jnp.exp(r[n]) + upd)
    return outs, new_states


def _scan_kernel(*refs, mode, chunk, nh):
    if mode == "gla":
        (zq_ref, zk_ref, zv_ref, zs_ref, zr_ref, wg2_ref, bg_ref, ng_ref, ones_ref, sel_ref,
         o_ref, sout_ref, q_s, k_s, la_s, st_s) = refs
    else:
        (zq_ref, zf_ref, zv_ref, zr_ref, llb_ref, l1m_ref, oml_ref, ng_ref, ones_ref, sel_ref,
         o_ref, sout_ref, q_s, k_s, la_s, st_s) = refs
    ti = pl.program_id(2)
    tblk = q_s.shape[0]

    @pl.when(ti == 0)
    def _():
        st_s[...] = jnp.zeros_like(st_s)

    if mode == "gla":
        dk = LANES // nh
        q_s[...] = zq_ref[...] * dk ** -0.5
        k_s[...] = zk_ref[...]
        zg = jnp.dot(zs_ref[...].astype(BF16), wg2_ref[...], preferred_element_type=F32) + bg_ref[...]
        la_s[...] = _log_sigmoid(zg) * (1.0 / GLA_GATE_TEMP)
        lane = lax.broadcasted_iota(jnp.int32, (1, LANES), 1)
        masks = [((lane >= h * dk) & (lane < (h + 1) * dk)).astype(F32) for h in range(nh)]
    else:
        hf = zf_ref[...]
        e = jnp.exp(-jnp.abs(hf))
        l1pe = jnp.log(1.0 + e)
        inv = 1.0 / (1.0 + e)
        sig_neg = jnp.where(hf > 0, e * inv, inv)
        lsig = jnp.minimum(hf, 0.0) - l1pe
        a = llb_ref[...]
        bb = l1m_ref[...] + lsig
        la_s[...] = jnp.maximum(a, bb) + jnp.log(1.0 + jnp.exp(-jnp.abs(a - bb)))
        k_s[...] = oml_ref[...] * sig_neg
        q_s[...] = _silu(zq_ref[...])
        masks = [None]

    sel = sel_ref[...]
    ones_ws = [ones_ref[h] for h in range(nh)]

    nu = q_s.shape[1] // LANES

    def body(ci, _, fast):
        off = pl.multiple_of(ci * chunk, chunk)
        sl = pl.ds(off, chunk)
        for u in range(nu):
            ul = slice(u * LANES, (u + 1) * LANES)
            hl = [slice((u * nh + h) * LANES, (u * nh + h + 1) * LANES) for h in range(nh)]
            vs = [zv_ref[sl, hl[h]] for h in range(nh)]
            states = [st_s[u * nh + h] for h in range(nh)]
            outs, new_states = _scan_chunk(q_s[sl, ul], k_s[sl, ul], la_s[sl, ul], vs, masks, states, ones_ws,
                                           sel, fast)
            for h in range(nh):
                st_s[u * nh + h] = new_states[h]
                o = _rms(outs[h], ng_ref[...]) * _silu(zr_ref[sl, hl[h]])
                o_ref[0, sl, hl[h]] = o.astype(o_ref.dtype)
        return 0

    safe = jnp.min(la_s[...]) * SUB >= -FAST_BLOCK_DECAY

    @pl.when(safe)
    def _():
        lax.fori_loop(0, tblk // chunk, functools.partial(body, fast=True), 0)

    @pl.when(jnp.logical_not(safe))
    def _():
        lax.fori_loop(0, tblk // chunk, functools.partial(body, fast=False), 0)

    @pl.when(ti == pl.num_programs(2) - 1)
    def _():
        dk_out = LANES // nh
        for u in range(nu):
            for h in range(nh):
                sout_ref[0, u * nh + h] = st_s[u * nh + h].T[h * dk_out:(h + 1) * dk_out, :]


def _scan_consts(chunk, nh):
    n = chunk // SUB
    sel = np.zeros((chunk, n * SUB * SUB), np.float32)
    for i in range(n):
        for t in range(SUB):
            sel[i * SUB + t, i * 64 + t * SUB:i * 64 + (t + 1) * SUB] = 1.0
    ones = np.zeros((nh, LANES, LANES), np.float32)
    dk = LANES // nh
    for h in range(nh):
        ones[h, h * dk:(h + 1) * dk, :] = 1.0
    return jnp.asarray(ones, BF16), jnp.asarray(sel, BF16)


def _lane_block(col, name, width):
    assert (col[name] * LANES) % width == 0
    return col[name] * LANES // width


def _scan_gla(z, b, t, units, nu, col, wg2p, bg, ng, tblk, chunk):
    nh = 2
    nt = t // tblk
    ones, sel = _scan_consts(chunk, nh)
    kw, vw = nu * LANES, nu * nh * LANES
    zspec = lambda name, w: pl.BlockSpec((tblk, w), lambda i, u, j: (i * nt + j, _lane_block(col, name, w) + u))
    in_specs = [
        zspec("gq", kw), zspec("gk", kw), zspec("gv", vw),
        pl.BlockSpec((tblk, LANES), lambda i, u, j: (i * nt + j, col["small"])),
        zspec("gr", vw),
        pl.BlockSpec((LANES, kw), lambda i, u, j: (0, u)),
        pl.BlockSpec((1, kw), lambda i, u, j: (0, u)),
        pl.BlockSpec((1, LANES), lambda i, u, j: (0, 0)),
        pl.BlockSpec(ones.shape, lambda i, u, j: (0, 0, 0)),
        pl.BlockSpec(sel.shape, lambda i, u, j: (0, 0)),
    ]
    dk = LANES // nh
    return pl.pallas_call(
        functools.partial(_scan_kernel, mode="gla", chunk=chunk, nh=nh),
        grid=(b, units // nu, nt),
        in_specs=in_specs,
        out_specs=[pl.BlockSpec((1, tblk, vw), lambda i, u, j: (i, j, u)),
                   pl.BlockSpec((1, nu * nh, dk, LANES), lambda i, u, j: (i, u, 0, 0))],
        out_shape=[jax.ShapeDtypeStruct((b, t, units * nh * LANES), BF16),
                   jax.ShapeDtypeStruct((b, units * nh, dk, LANES), F32)],
        scratch_shapes=[pltpu.VMEM((tblk, kw), F32)] * 3 + [pltpu.VMEM((nu * nh, LANES, LANES), F32)],
        compiler_params=_cparams(("parallel", "parallel", "arbitrary")),
        name="scan_gla",
    )(z, z, z, z, z, wg2p, bg, ng, ones, sel)


def _scan_hgrn(z, b, t, units, nu, col, llb, l1m, oml, ng, tblk, chunk):
    nh = 1
    nt = t // tblk
    ones, sel = _scan_consts(chunk, nh)
    kw = nu * LANES
    zspec = lambda name: pl.BlockSpec((tblk, kw), lambda i, u, j: (i * nt + j, _lane_block(col, name, kw) + u))
    pspec = pl.BlockSpec((1, kw), lambda i, u, j: (0, u))
    in_specs = [zspec("hq"), zspec("hf"), zspec("hi"), zspec("hg"), pspec, pspec, pspec,
                pl.BlockSpec((1, LANES), lambda i, u, j: (0, 0)),
                pl.BlockSpec(ones.shape, lambda i, u, j: (0, 0, 0)),
                pl.BlockSpec(sel.shape, lambda i, u, j: (0, 0))]
    return pl.pallas_call(
        functools.partial(_scan_kernel, mode="hgrn", chunk=chunk, nh=nh),
        grid=(b, units // nu, nt),
        in_specs=in_specs,
        out_specs=[pl.BlockSpec((1, tblk, kw), lambda i, u, j: (i, j, u)),
                   pl.BlockSpec((1, nu, LANES, LANES), lambda i, u, j: (i, u, 0, 0))],
        out_shape=[jax.ShapeDtypeStruct((b, t, units * LANES), BF16),
                   jax.ShapeDtypeStruct((b, units, LANES, LANES), F32)],
        scratch_shapes=[pltpu.VMEM((tblk, kw), F32)] * 3 + [pltpu.VMEM((nu, LANES, LANES), F32)],
        compiler_params=_cparams(("parallel", "parallel", "arbitrary")),
        name="scan_hgrn",
    )(z, z, z, z, llb, l1m, oml, ng, ones, sel)


def _rec_kernel(*refs, mode, nseq, ntok, dk):
    if mode == "gla":
        (qt_ref, kt_ref, st_ref, wg2t_ref, bgt_ref, zv_ref, zr_ref, ng_ref, s0_ref, o_ref, sout_ref, o_s) = refs
        qc = qt_ref[...] * dk ** -0.5
        kc = kt_ref[...]
        zg = jnp.dot(wg2t_ref[...], st_ref[...].astype(BF16), preferred_element_type=F32) + bgt_ref[...]
        ac = jnp.exp(_log_sigmoid(zg) * (1.0 / GLA_GATE_TEMP))
    else:
        (qt_ref, ft_ref, llb_ref, l1m_ref, oml_ref, zv_ref, zr_ref, ng_ref, s0_ref, o_ref, sout_ref, o_s) = refs
        hf = ft_ref[...]
        lsig = _log_sigmoid(hf)
        a = llb_ref[...]
        bb = l1m_ref[...] + lsig
        ac = jnp.exp(jnp.maximum(a, bb) + jnp.log(1.0 + jnp.exp(-jnp.abs(a - bb))))
        kc = oml_ref[...] * _sigmoid(-hf)
        qc = _silu(qt_ref[...])
    for sq in range(nseq):
        s = s0_ref[sq, 0]
        for t in range(ntok):
            j = sq * ntok + t
            vrow = zv_ref[j:j + 1, :]
            s = s * ac[:, j:j + 1] + kc[:, j:j + 1] * vrow
            o_s[j:j + 1, :] = jnp.sum(s * qc[:, j:j + 1], axis=0, keepdims=True)
        sout_ref[sq, 0] = s
    o = _rms(o_s[...], ng_ref[...]) * _silu(zr_ref[...])
    o_ref[...] = o.astype(o_ref.dtype)


def _rec(mode, zt, z, col, params, ng, s0, nb, ntok, heads, dk, nseq):
    m = nb * ntok
    rows = nseq * ntok
    ng_groups = nb // nseq
    if mode == "gla":
        wg2t, bgt = params
        per = LANES // dk
        tspec = lambda name: pl.BlockSpec((None, dk, rows), lambda h, g: (g, col[name] * per + h, 0))
        in_specs = [tspec("gq"), tspec("gk"),
                    pl.BlockSpec((None, LANES, rows), lambda h, g: (g, col["small"], 0)),
                    pl.BlockSpec((dk, LANES), lambda h, g: (h, 0)),
                    pl.BlockSpec((dk, 1), lambda h, g: (h, 0)),
                    pl.BlockSpec((rows, LANES), lambda h, g: (g, col["gv"] + h)),
                    pl.BlockSpec((rows, LANES), lambda h, g: (g, col["gr"] + h))]
        args = (zt, zt, zt, wg2t, bgt, z, z)
    else:
        llb, l1m, oml = params
        tspec = lambda name: pl.BlockSpec((None, dk, rows), lambda h, g: (g, col[name] + h, 0))
        pspec = pl.BlockSpec((dk, 1), lambda h, g: (h, 0))
        in_specs = [tspec("hq"), tspec("hf"), pspec, pspec, pspec,
                    pl.BlockSpec((rows, LANES), lambda h, g: (g, col["hi"] + h)),
                    pl.BlockSpec((rows, LANES), lambda h, g: (g, col["hg"] + h))]
        args = (zt, zt, llb, l1m, oml, z, z)
    in_specs += [pl.BlockSpec((1, LANES), lambda h, g: (0, 0)),
                 pl.BlockSpec((nseq, 1, dk, LANES), lambda h, g: (g, h, 0, 0))]
    return pl.pallas_call(
        functools.partial(_rec_kernel, mode=mode, nseq=nseq, ntok=ntok, dk=dk),
        grid=(heads, ng_groups),
        in_specs=in_specs,
        out_specs=[pl.BlockSpec((rows, LANES), lambda h, g: (g, h)),
                   pl.BlockSpec((nseq, 1, dk, LANES), lambda h, g: (g, h, 0, 0))],
        out_shape=[jax.ShapeDtypeStruct((m, heads * LANES), BF16),
                   jax.ShapeDtypeStruct(s0.shape, F32)],
        scratch_shapes=[pltpu.VMEM((rows, LANES), F32)],
        compiler_params=_cparams(("parallel", "parallel")),
        name="rec_" + mode,
    )(*args, ng, s0)


def _lfpool_kernel(lf_ref, o_ref, *, nh):
    x = lf_ref[...]
    w = x.shape[1]
    lane = lax.broadcasted_iota(jnp.int32, x.shape, 1)
    incl = x
    sh = nh
    while sh < w:
        incl = incl + jnp.where(lane < w - sh, pltpu.roll(incl, w - sh, 1), 0.0)
        sh *= 2
    tot = jnp.where(lane < nh, incl, 0.0)
    sh = nh
    while sh < w:
        tot = tot + pltpu.roll(tot, sh, 1)
        sh *= 2
    o_ref[:, :w] = incl - x
    o_ref[:, w:] = tot


def _lfpool(clf, nh, rows):
    depth, n_pool, w = clf.shape
    return pl.pallas_call(
        functools.partial(_lfpool_kernel, nh=nh),
        grid=(depth, n_pool // rows),
        in_specs=[pl.BlockSpec((None, rows, w), lambda l, i: (l, i, 0))],
        out_specs=pl.BlockSpec((None, rows, 2 * w), lambda l, i: (l, i, 0)),
        out_shape=jax.ShapeDtypeStruct((depth, n_pool, 2 * w), F32),
        compiler_params=_cparams(("parallel", "parallel")),
        name="lfpool",
    )(clf)


def _paged_kernel(pt_ref, q_ref, kn_ref, vn_ref, zs_ref, bf_ref, mask_ref, *rest, npg, nh, dh, ntok, scale,
                  npages):
    k_refs = rest[:npg]
    v_refs = rest[npg:2 * npg]
    rt_refs = rest[2 * npg:3 * npg]
    o_ref, lf_ref = rest[3 * npg:3 * npg + 2]
    m_s, l_s, acc_s, car_s = rest[3 * npg + 2:]
    b = pl.program_id(0)
    g = pl.program_id(1)

    @pl.when(g == 0)
    def _():
        m_s[...] = jnp.full(m_s.shape, NEG, F32)
        l_s[...] = jnp.zeros_like(l_s)
        acc_s[...] = jnp.zeros_like(acc_s)
        car_s[...] = jnp.zeros_like(car_s)

    q = q_ref[0] * scale
    q_all = jnp.concatenate([q[:, h * dh:(h + 1) * dh] for h in range(nh)], axis=0).astype(BF16)
    w = mask_ref.shape[1]
    carry = car_s[...]
    s_pages = []
    for i in range(npg):
        pid = pt_ref[b, npages - 1 - (g * npg + i)]
        rt = rt_refs[i][pl.ds(pid % SUB, 1), :]
        bias = rt[:, :w] + carry
        carry = carry + rt[:, w:]
        s_pages.append(_dot_t(q_all, k_refs[i][...].astype(BF16)) + (mask_ref[...] + bias))
    car_s[...] = carry
    s = jnp.concatenate(s_pages, axis=1)
    m = m_s[...]
    m_new = jnp.maximum(m, jnp.max(s, axis=-1, keepdims=True))
    alpha = jnp.exp(m - m_new)
    p = jnp.exp(s - m_new)
    l_s[...] = alpha * l_s[...] + jnp.sum(p, axis=-1, keepdims=True)
    pv = jnp.zeros(acc_s.shape, F32)
    for i in range(npg):
        pv = pv + jnp.dot(p[:, i * w:(i + 1) * w].astype(BF16), v_refs[i][...].astype(BF16),
                          preferred_element_type=F32)
    acc_s[...] = alpha * acc_s[...] + pv
    m_s[...] = m_new

    @pl.when(g == pl.num_programs(1) - 1)
    def _():
        lfn = _log_sigmoid(zs_ref[0] + bf_ref[...])
        lf_ref[0] = lfn
        row = lax.broadcasted_iota(jnp.int32, lfn.shape, 0)
        cn = _local_cumsum(jnp.where(row < ntok, lfn, 0.0))
        trow = lax.broadcasted_iota(jnp.int32, (SUB, 1), 0)
        for h in range(nh):
            hs = slice(h * dh, (h + 1) * dh)
            rs = slice(h * SUB, (h + 1) * SUB)
            m, l, acc = m_s[rs, :], l_s[rs, :], acc_s[rs, :]
            for sp in range(ntok):
                logit = jnp.sum(q[:, hs] * kn_ref[0, sp:sp + 1, hs], axis=-1, keepdims=True)
                logit = logit - cn[sp:sp + 1, FF_OFF + h:FF_OFF + h + 1]
                logit = jnp.where(trow >= sp, logit, NEG)
                m_new = jnp.maximum(m, logit)
                alpha = jnp.exp(m - m_new)
                p = jnp.exp(logit - m_new)
                l = alpha * l + p
                acc = alpha * acc + p * vn_ref[0, sp:sp + 1, hs]
                m = m_new
            o_ref[0, :, hs] = (acc / l).astype(o_ref.dtype)


def _paged(layer, page_table, q, kn, vn, zs, bf_row, cache_k, cache_v, rtot, nh, dh, ntok, npg):
    nb = q.shape[0]
    w = cache_k.shape[2]
    npages = page_table.shape[1]
    ngrp = npages // npg
    mask = np.full((nh * SUB, w), NEG, np.float32)
    for h in range(nh):
        mask[h * SUB:(h + 1) * SUB, h::nh] = 0.0

    def page_of(b, g, pt, i):
        return pt[b, npages - 1 - (g * npg + i)]

    seq3 = lambda b, g, pt: (b, 0, 0)
    in_specs = [pl.BlockSpec((1, SUB, nh * dh), seq3)] * 3 + [
        pl.BlockSpec((1, SUB, LANES), seq3),
        pl.BlockSpec((1, LANES), lambda b, g, pt: (0, 0)),
        pl.BlockSpec(mask.shape, lambda b, g, pt: (0, 0))]
    kv_specs = [pl.BlockSpec((None, None, w, dh), lambda b, g, pt, i=i: (layer, page_of(b, g, pt, i), 0, 0))
                for i in range(npg)]
    in_specs += kv_specs + kv_specs
    in_specs += [pl.BlockSpec((None, SUB, 2 * w), lambda b, g, pt, i=i: (layer, page_of(b, g, pt, i) // SUB, 0))
                 for i in range(npg)]
    grid_spec = pltpu.PrefetchScalarGridSpec(
        num_scalar_prefetch=1, grid=(nb, ngrp), in_specs=in_specs,
        out_specs=[pl.BlockSpec((1, SUB, nh * dh), seq3), pl.BlockSpec((1, SUB, LANES), seq3)],
        scratch_shapes=[pltpu.VMEM((nh * SUB, 1), F32), pltpu.VMEM((nh * SUB, 1), F32),
                        pltpu.VMEM((nh * SUB, dh), F32), pltpu.VMEM((1, w), F32)])
    return pl.pallas_call(
        functools.partial(_paged_kernel, npg=npg, nh=nh, dh=dh, ntok=ntok, scale=dh ** -0.5, npages=npages),
        grid_spec=grid_spec,
        out_shape=[jax.ShapeDtypeStruct((nb, SUB, nh * dh), F32),
                   jax.ShapeDtypeStruct((nb, SUB, LANES), F32)],
        compiler_params=_cparams(("parallel", "arbitrary")),
        name="paged",
    )(page_table, q, kn, vn, zs, bf_row, jnp.asarray(mask), *([cache_k] * npg), *([cache_v] * npg),
      *([rtot] * npg))


def _merge_kernel(x_ref, oa_ref, ob_ref, oc_ref, g1_ref, wg_ref, wa_ref, wb_ref, wc_ref, wo_ref, y_ref):
    x = x_ref[...]
    d = x.shape[1]
    xn = _rms(x, g1_ref[...]).astype(BF16)
    merged = jnp.zeros(x.shape, F32)
    for i, (o_ref, w_ref) in enumerate(((oa_ref, wa_ref), (ob_ref, wb_ref), (oc_ref, wc_ref))):
        gate = _sigmoid(jnp.dot(xn, wg_ref[:, i * d:(i + 1) * d], preferred_element_type=F32))
        merged = merged + gate * jnp.dot(o_ref[...], w_ref[...], preferred_element_type=F32)
    y_ref[...] = x + jnp.dot(merged.astype(BF16), wo_ref[...], preferred_element_type=F32)


def _merge(x, oa, ob, oc, g1, wg_all, layer, wa, wb, wc, wo, tm):
    m, d = x.shape
    const = lambda a: pl.BlockSpec(a.shape, lambda i: (0,) * a.ndim)
    rows = lambda a: pl.BlockSpec((tm, a.shape[1]), lambda i: (i, 0))
    return pl.pallas_call(
        _merge_kernel,
        grid=(m // tm,),
        in_specs=[rows(x), rows(oa), rows(ob), rows(oc), const(g1),
                  pl.BlockSpec((None,) + wg_all.shape[1:], lambda i: (layer, 0, 0)),
                  const(wa), const(wb), const(wc), const(wo)],
        out_specs=pl.BlockSpec((tm, d), lambda i: (i, 0)),
        out_shape=jax.ShapeDtypeStruct((m, d), F32),
        compiler_params=_cparams(("parallel",)),
        name="merge",
    )(x, oa, ob, oc, g1, wg_all, wa, wb, wc, wo)


def _ffn_kernel(x_ref, g2_ref, wg_ref, wu_ref, wd_ref, gf_ref, y_ref, *, final):
    x = x_ref[...]
    h = _rms(x, g2_ref[...]).astype(BF16)
    acc = x
    hidden = wg_ref.shape[1]
    for c0 in range(0, hidden, FFN_HIDDEN_CHUNK):
        cs = slice(c0, min(c0 + FFN_HIDDEN_CHUNK, hidden))
        a = jnp.dot(h, wg_ref[:, cs], preferred_element_type=F32)
        u = jnp.dot(h, wu_ref[:, cs], preferred_element_type=F32)
        acc = acc + jnp.dot((_silu(a) * u).astype(BF16), wd_ref[cs, :], preferred_element_type=F32)
    y_ref[...] = _rms(acc, gf_ref[...]) if final else acc


def _ffn(x, g2, wg, wu, wd, gf, tm, final):
    m, d = x.shape
    const = lambda a: pl.BlockSpec(a.shape, lambda i: (0,) * a.ndim)
    return pl.pallas_call(
        functools.partial(_ffn_kernel, final=final),
        grid=(m // tm,),
        in_specs=[pl.BlockSpec((tm, d), lambda i: (i, 0)), const(g2), const(wg), const(wu), const(wd), const(gf)],
        out_specs=pl.BlockSpec((tm, d), lambda i: (i, 0)),
        out_shape=jax.ShapeDtypeStruct((m, d), F32),
        compiler_params=_cparams(("parallel",)),
        name="ffn",
    )(x, g2, wg, wu, wd, gf)


def _pick(n, pref):
    for c in pref:
        if n % c == 0:
            return c
    return n


def kernel(x_prompt, x_sample, state_gla, cache_fox_k, cache_fox_v, cache_fox_logf, state_hgrn, page_table,
           norm1_g, w_in, gla_wg2, gla_bg, gla_norm_g, fox_bf, hg_lb_logits, hg_norm_g,
           w_branch_a, w_branch_b, w_branch_c, w_out, norm2_g, w_ffn_gate, w_ffn_up, w_ffn_down,
           final_norm_g):
    depth, d_model, _ = w_in.shape
    bp, tp, _ = x_prompt.shape
    nb, ntok, _ = x_sample.shape
    _, _, gh, gdk, gdv = state_gla.shape
    _, _, hh, hdk, hdv = state_hgrn.shape
    _, n_pool, page, fh, fdh = cache_fox_k.shape
    hidden = w_ffn_gate.shape[2]
    assert gdv == LANES and hdk == LANES and hdv == LANES and fdh == LANES and 2 * gdk == LANES
    assert ntok <= SUB and gh % 2 == 0

    gq_w, gv_w, f_w, h_w = gh * gdk, gh * gdv, fh * fdh, hh * hdk
    names = ["gq", "gk", "gv", "glr", "gr", "fq", "fk", "fv", "ff", "hq", "hf", "hi", "hg", "ga", "gb", "gc"]
    widths = [gq_w, gq_w, gv_w, GLA_GATE_RANK, gv_w, f_w, f_w, f_w, fh, h_w, h_w, h_w, h_w,
              d_model, d_model, d_model]
    starts = dict(zip(names, np.concatenate([[0], np.cumsum(widths)[:-1]]).tolist()))
    wid = dict(zip(names, widths))
    order = ["gq", "gk", "gv", "gr", "fq", "fk", "fv", "hq", "hf", "hi", "hg"]
    col, off = {}, 0
    for nm in order:
        col[nm] = off // LANES
        off += wid[nm]
    col["small"] = off // LANES
    n_used = off + LANES
    tn = min(14, n_used // LANES) * LANES
    n_pad = -(-n_used // tn) * tn

    segs = [(starts[nm], col[nm] * LANES, wid[nm]) for nm in order]
    segs += [(starts["ff"], col["small"] * LANES + FF_OFF, fh),
             (starts["glr"], col["small"] * LANES + GLR_OFF, GLA_GATE_RANK)]
    w_pad_all, w_gates_all = _wprep(w_in, segs, starts["ga"], 3 * d_model, n_pad, _pick(d_model, (128, 64, 32, 16, 8)))

    lb_cum = jnp.cumsum(jax.nn.softmax(hg_lb_logits.astype(F32), axis=0), axis=0)
    hg_lb = lb_cum - lb_cum[:1]
    log_lb, log1m_lb, one_m_lb = jnp.log(hg_lb), jnp.log1p(-hg_lb), 1.0 - hg_lb


    xp = x_prompt.reshape(bp * tp, d_model)
    xs = jnp.pad(x_sample, ((0, 0), (0, SUB - ntok), (0, 0))).reshape(nb * SUB, d_model)
    ms = nb * SUB

    tm_p = _pick(bp * tp, (1024, 512, 256, 128))
    tt = _pick(tp, (512, 256, 128))
    fblk = _pick(tp, (512, 256, 128))
    tblk = _pick(tp, (512, 256, 128))
    chunk = min(128, tblk)
    tm_e = _pick(bp * tp, (512, 256, 128))
    nseq = _pick(nb, (8, 4, 2, 1))
    npg = _pick(page_table.shape[1], (16, 8, 4, 2, 1))

    ck = cache_fox_k.reshape(depth, n_pool, page * fh, fdh)
    cv = cache_fox_v.reshape(depth, n_pool, page * fh, fdh)
    rtot = _lfpool(cache_fox_logf.astype(F32).reshape(depth, n_pool, page * fh), fh, _pick(n_pool, (256, 128, 64, 32, 16, 8)))

    outs = {k: [] for k in ("gla_p", "gla_s", "k_p", "v_p", "lf_p", "k_s", "v_s", "lf_s", "hg_p", "hg_s")}

    for l in range(depth):
        g1 = norm1_g[l].reshape(1, d_model)
        g2 = norm2_g[l].reshape(1, d_model)
        bf_row = jnp.zeros((1, LANES), F32).at[0, FF_OFF:FF_OFF + fh].set(fox_bf[l])
        wg2p = jnp.zeros((LANES, gq_w), F32).at[GLR_OFF:GLR_OFF + GLA_GATE_RANK].set(gla_wg2[l])
        wg2_b = wg2p.astype(BF16)
        bg_row = gla_bg[l].reshape(1, gq_w)
        gng = gla_norm_g[l].reshape(1, LANES)
        hng = hg_norm_g[l].reshape(1, LANES)
        llb_u, l1m_u, oml_u = (a[l].reshape(1, h_w) for a in (log_lb, log1m_lb, one_m_lb))
        wa, wb, wc, wo = (w[l].astype(BF16) for w in (w_branch_a, w_branch_b, w_branch_c, w_out))
        wfg, wfu, wfd = (w[l].astype(BF16) for w in (w_ffn_gate, w_ffn_up, w_ffn_down))
        gf = final_norm_g.reshape(1, d_model)
        final = l == depth - 1

        z, fk, fv = _proj(xp, g1, w_pad_all, l, tm_p, tn, (col["fk"] * LANES, col["fv"] * LANES, fh, fdh))
        outs["k_p"].append(fk.reshape(bp, tp, fh, fdh))
        outs["v_p"].append(fv.reshape(bp, tp, fh, fdh))
        lf, ctok = _fox_prep(z, bf_row, bp, tp, col["small"], fh, tt)
        outs["lf_p"].append(lf)
        ob = _flash(z, ctok, bp, tp, fh, fdh, col["fq"], col["fk"], col["fv"], fblk)
        oa, sg = _scan_gla(z, bp, tp, gh // 2, GLA_UNITS_PER_STEP, col, wg2_b, bg_row, gng, tblk, chunk)
        outs["gla_p"].append(sg)
        oc, sh = _scan_hgrn(z, bp, tp, hh, HGRN_UNITS_PER_STEP, col, llb_u, l1m_u, oml_u, hng, tblk, chunk)
        outs["hg_p"].append(sh)
        x1 = _merge(xp, oa.reshape(bp * tp, -1), ob.reshape(bp * tp, -1), oc.reshape(bp * tp, -1),
                    g1, w_gates_all, l, wa, wb, wc, wo, tm_e)
        xp = _ffn(x1, g2, wfg, wfu, wfd, gf, tm_e, final)

        zs, = _proj(xs, g1, w_pad_all, l, ms, tn)
        zs3 = zs.reshape(nb, SUB, n_pad)
        take = lambda nm, w: zs3[:, :, col[nm] * LANES:col[nm] * LANES + w]
        fks, fvs = take("fk", f_w), take("fv", f_w)
        outs["k_s"].append(fks[:, :ntok].reshape(nb, ntok, fh, fdh))
        outs["v_s"].append(fvs[:, :ntok].reshape(nb, ntok, fh, fdh))
        small_s = take("small", LANES)
        obs, lfs = _paged(l, page_table, take("fq", f_w), fks, fvs, small_s, bf_row, ck, cv, rtot,
                          fh, fdh, ntok, npg)
        outs["lf_s"].append(lfs[:, :ntok, FF_OFF:FF_OFF + fh])
        obs = obs.astype(BF16)
        zc = zs3[:, :ntok].reshape(nb * ntok, n_pad)
        zt = zc.reshape(nb // nseq, nseq * ntok, n_pad).transpose(0, 2, 1)
        wg2t = wg2p.T.astype(BF16)
        oas, sgs = _rec("gla", zt, zc, col, (wg2t, gla_bg[l].reshape(-1, 1)), gng, state_gla[l],
                        nb, ntok, gh, gdk, nseq)
        outs["gla_s"].append(sgs)
        ocs, shs = _rec("hgrn", zt, zc, col, tuple(a[l].reshape(-1, 1) for a in (log_lb, log1m_lb, one_m_lb)),
                        hng, state_hgrn[l], nb, ntok, hh, hdk, nseq)
        outs["hg_s"].append(shs)
        pad_tok = lambda o: jnp.pad(o.reshape(nb, ntok, -1), ((0, 0), (0, SUB - ntok), (0, 0))).reshape(ms, -1)
        x1s = _merge(xs, pad_tok(oas), obs.reshape(ms, -1), pad_tok(ocs), g1, w_gates_all, l, wa, wb, wc, wo, ms)
        xs = _ffn(x1s, g2, wfg, wfu, wfd, gf, ms, final)

    st = lambda k: jnp.stack(outs[k])
    y_p = xp.reshape(bp, tp, d_model)
    y_s = xs.reshape(nb, SUB, d_model)[:, :ntok]
    return (y_p, y_s, st("gla_p"), st("gla_s"), st("k_p"), st("v_p"), st("lf_p"),
            st("k_s"), st("v_s"), st("lf_s"), st("hg_p"), st("hg_s"))
```

```python
import functools

import numpy as np
import jax
import jax.numpy as jnp
from jax import lax
from jax.experimental import pallas as pl
from jax.experimental.pallas import tpu as pltpu

F32 = jnp.float32
BF16 = jnp.bfloat16
EPS = 1e-6
NEG = -1e30
LANES = 128
SUB = 8
HI = lax.Precision.HIGHEST
VMEM_LIMIT = 56 * 1024 * 1024

FAST_BLOCK_DECAY = 60.0
GLA_UNITS_PER_STEP = 2
HGRN_UNITS_PER_STEP = 4
FFN_HIDDEN_CHUNK = 256
GLA_GATE_TEMP = 16.0
GLA_GATE_RANK = 16
FF_OFF = 0
GLR_OFF = 16


def _cparams(sem):
    return pltpu.CompilerParams(dimension_semantics=sem, vmem_limit_bytes=VMEM_LIMIT)


def _rms(x, g):
    return x * lax.rsqrt(jnp.mean(x * x, axis=-1, keepdims=True) + EPS) * g


def _sigmoid(x):
    return 1.0 / (1.0 + jnp.exp(-x))


def _log_sigmoid(x):
    return jnp.minimum(x, 0.0) - jnp.log(1.0 + jnp.exp(-jnp.abs(x)))


def _silu(x):
    return x * _sigmoid(x)


def _dot_t(a, b):
    return lax.dot_general(a, b, (((1,), (1,)), ((), ())), preferred_element_type=F32)


def _wprep_kernel(w_ref, wp_ref, wg_ref, *, segs, gate_src):
    wp_ref[...] = jnp.zeros_like(wp_ref)
    for src, dst, width in segs:
        wp_ref[:, dst:dst + width] = w_ref[:, src:src + width].astype(BF16)
    wg_ref[...] = w_ref[:, gate_src:gate_src + wg_ref.shape[1]].astype(BF16)


def _wprep(w_in, segs, gate_src, gate_w, n_pad, tr):
    depth, d, n_in = w_in.shape
    return pl.pallas_call(
        functools.partial(_wprep_kernel, segs=segs, gate_src=gate_src),
        grid=(depth, d // tr),
        in_specs=[pl.BlockSpec((None, tr, n_in), lambda l, i: (l, i, 0))],
        out_specs=[pl.BlockSpec((None, tr, n_pad), lambda l, i: (l, i, 0)),
                   pl.BlockSpec((None, tr, gate_w), lambda l, i: (l, i, 0))],
        out_shape=[jax.ShapeDtypeStruct((depth, d, n_pad), BF16),
                   jax.ShapeDtypeStruct((depth, d, gate_w), BF16)],
        compiler_params=_cparams(("parallel", "parallel")),
        name="wprep",
    )(w_in)


def _proj_kernel(x_ref, g_ref, w_ref, z_ref, *rest, kv):
    xn_ref = rest[-1]

    @pl.when(pl.program_id(1) == 0)
    def _():
        xn_ref[...] = _rms(x_ref[...], g_ref[...]).astype(BF16)

    zt = jnp.dot(xn_ref[...], w_ref[...], preferred_element_type=F32)
    z_ref[...] = zt
    if kv is not None:
        fk_ref, fv_ref = rest[:2]
        jkv, koff, voff, nh, dh = kv
        tm = zt.shape[0]

        @pl.when(pl.program_id(1) == jkv)
        def _():
            for h in range(nh):
                fk_ref[pl.ds(h, tm, stride=nh), :] = zt[:, koff + h * dh:koff + (h + 1) * dh]
                fv_ref[pl.ds(h, tm, stride=nh), :] = zt[:, voff + h * dh:voff + (h + 1) * dh]


def _proj(x, g, w_all, layer, tm, tn, kv_cols=None):
    m, d = x.shape
    n = w_all.shape[2]
    out_specs = [pl.BlockSpec((tm, tn), lambda i, j: (i, j))]
    out_shape = [jax.ShapeDtypeStruct((m, n), F32)]
    kv = None
    if kv_cols is not None:
        kc, vc, nh, dh = kv_cols
        assert kc // tn == (vc + nh * dh - 1) // tn
        kv = (kc // tn, kc % tn, vc % tn, nh, dh)
        out_specs += [pl.BlockSpec((tm * nh, dh), lambda i, j: (i, 0))] * 2
        out_shape += [jax.ShapeDtypeStruct((m * nh, dh), F32)] * 2
    return pl.pallas_call(
        functools.partial(_proj_kernel, kv=kv),
        grid=(m // tm, n // tn),
        in_specs=[pl.BlockSpec((tm, d), lambda i, j: (i, 0)),
                  pl.BlockSpec((1, d), lambda i, j: (0, 0)),
                  pl.BlockSpec((None, d, tn), lambda i, j: (layer, 0, j))],
        out_specs=out_specs,
        out_shape=out_shape,
        scratch_shapes=[pltpu.VMEM((tm, d), BF16)],
        compiler_params=_cparams(("parallel", "arbitrary")),
        name="proj",
    )(x, g, w_all)


def _fox_prep_kernel(zs_ref, bf_ref, tril_ref, lf_ref, c_ref, carry_ref, *, nh):
    @pl.when(pl.program_id(1) == 0)
    def _():
        carry_ref[...] = jnp.zeros_like(carry_ref)

    lf = _log_sigmoid(zs_ref[...] + bf_ref[...])
    lf_ref[0] = lf[:, FF_OFF:FF_OFF + nh]
    c = jnp.dot(tril_ref[...], lf, precision=HI, preferred_element_type=F32) + carry_ref[...][:1]
    c_ref[0] = c
    carry_ref[...] = jnp.broadcast_to(c[-1:], carry_ref.shape)


def _fox_prep(z, bf_row, b, t, small_blk, nh, tt):
    tril = np.tril(np.ones((tt, tt), np.float32))
    nt = t // tt
    return pl.pallas_call(
        functools.partial(_fox_prep_kernel, nh=nh),
        grid=(b, nt),
        in_specs=[pl.BlockSpec((tt, LANES), lambda i, j: (i * nt + j, small_blk)),
                  pl.BlockSpec((1, LANES), lambda i, j: (0, 0)),
                  pl.BlockSpec((tt, tt), lambda i, j: (0, 0))],
        out_specs=[pl.BlockSpec((1, tt, nh), lambda i, j: (i, j, 0)),
                   pl.BlockSpec((1, tt, LANES), lambda i, j: (i, j, 0))],
        out_shape=[jax.ShapeDtypeStruct((b, t, nh), F32),
                   jax.ShapeDtypeStruct((b, t, LANES), F32)],
        scratch_shapes=[pltpu.VMEM((SUB, LANES), F32)],
        compiler_params=_cparams(("parallel", "arbitrary")),
        name="fox_prep",
    )(z, bf_row, jnp.asarray(tril))


BIAS_PIECES = 3
LOG2E = 1.4426950408889634


def _flash_kernel(q_ref, k_ref, v_ref, c_ref, eye_ref, place_ref, o_ref, kb_ref, vt_ref, sa_ref, sb_ref, *,
                  blk, scale, nh):
    h = pl.program_id(1)
    qi = pl.program_id(2)
    dh = q_ref.shape[1]

    @pl.when(qi == 0)
    def _():
        def prep(j, _):
            rows = pl.ds(pl.multiple_of(j * blk, blk), blk)
            kb_ref[rows, :dh] = k_ref[rows, :].astype(BF16)
            vt_ref[j] = _dot_t(eye_ref[...], v_ref[rows, :].astype(BF16)).astype(BF16)
            rem = c_ref[0, rows, :] * (-LOG2E)
            extra = jnp.zeros(rem.shape, F32)
            for piece in range(BIAS_PIECES):
                part = rem.astype(BF16)
                extra = extra + jnp.dot(part, place_ref[piece], preferred_element_type=F32)
                rem = rem - part.astype(F32)
            kb_ref[rows, dh:] = extra.astype(BF16)
            return 0

        lax.fori_loop(0, k_ref.shape[0] // blk, prep, 0)

    qblk = q_ref.shape[0]
    lane_q = lax.broadcasted_iota(jnp.int32, (qblk, dh), 1)
    own = (lane_q >= h * BIAS_PIECES) & (lane_q < (h + 1) * BIAS_PIECES)
    q = jnp.concatenate([(q_ref[...] * (scale * LOG2E)).astype(BF16),
                         jnp.where(own, 1.0, 0.0).astype(BF16)], axis=1)

    def scores(j):
        return _dot_t(kb_ref[pl.ds(pl.multiple_of(j * blk, blk), blk), :], q)

    def absorb(s_ref, j, carry, key_off):
        m, l, acc = carry
        s = s_ref[...]
        if key_off is not None:
            key = lax.broadcasted_iota(jnp.int32, (blk, qblk), 0) + key_off
            qry = lax.broadcasted_iota(jnp.int32, (blk, qblk), 1)
            s = jnp.where(key <= qry, s, NEG)
        m_new = jnp.maximum(m, jnp.max(s, axis=0, keepdims=True))
        alpha = jnp.exp2(m - m_new)
        p = jnp.exp2(s - m_new)
        l = alpha * l + jnp.sum(p, axis=0, keepdims=True)
        acc = alpha * acc + jnp.dot(vt_ref[j], p.astype(BF16), preferred_element_type=F32)
        return m_new, l, acc

    sa_ref[...] = scores(0)

    def pair(p, carry):
        j = 2 * p
        sb_ref[...] = scores(j + 1)
        carry = absorb(sa_ref, j, carry, None)
        sa_ref[...] = scores(j + 2)
        return absorb(sb_ref, j + 1, carry, None)

    init = (jnp.full((1, qblk), NEG, F32), jnp.zeros((1, qblk), F32), jnp.zeros((dh, qblk), F32))
    carry = lax.fori_loop(0, qi, pair, init)
    sb_ref[...] = scores(2 * qi + 1)
    carry = absorb(sa_ref, 2 * qi, carry, 0)
    m, l, acc = absorb(sb_ref, 2 * qi + 1, carry, blk)
    o_ref[0] = (acc / l).T.astype(o_ref.dtype)


def _flash(z, ctok, b, t, nh, dh, q_blk0, k_blk0, v_blk0, qblk):
    assert FF_OFF + nh <= LANES and nh * BIAS_PIECES <= dh and dh == LANES and qblk % 2 == 0
    blk = qblk // 2
    nq = t // qblk
    eye = np.eye(dh, dtype=np.float32)
    place = np.zeros((BIAS_PIECES, LANES, dh), np.float32)
    for piece in range(BIAS_PIECES):
        for hh in range(nh):
            place[piece, FF_OFF + hh, hh * BIAS_PIECES + piece] = 1.0
    return pl.pallas_call(
        functools.partial(_flash_kernel, blk=blk, scale=dh ** -0.5, nh=nh),
        grid=(b, nh, nq),
        in_specs=[pl.BlockSpec((qblk, dh), lambda i, h, j: (i * nq + j, q_blk0 + h)),
                  pl.BlockSpec((t, dh), lambda i, h, j: (i, k_blk0 + h)),
                  pl.BlockSpec((t, dh), lambda i, h, j: (i, v_blk0 + h)),
                  pl.BlockSpec((1, t, LANES), lambda i, h, j: (i, 0, 0)),
                  pl.BlockSpec(eye.shape, lambda i, h, j: (0, 0)),
                  pl.BlockSpec(place.shape, lambda i, h, j: (0, 0, 0))],
        out_specs=pl.BlockSpec((1, qblk, dh), lambda i, h, j: (i, j, h)),
        out_shape=jax.ShapeDtypeStruct((b, t, nh * dh), BF16),
        scratch_shapes=[pltpu.VMEM((t, 2 * dh), BF16), pltpu.VMEM((t // blk, dh, blk), BF16),
                        pltpu.VMEM((blk, qblk), F32), pltpu.VMEM((blk, qblk), F32)],
        compiler_params=_cparams(("parallel", "parallel", "arbitrary")),
        name="flash",
    )(z, z, z, ctok, jnp.asarray(eye, BF16), jnp.asarray(place, BF16))


def _local_cumsum(x):
    row = lax.broadcasted_iota(jnp.int32, x.shape, 0)
    for sh in (1, 2, 4):
        x = x + jnp.where(row >= sh, pltpu.roll(x, sh, 0), 0.0)
    return x


def _scan_chunk(q, k, la, vs, masks, states, ones_ws, sel, fast):
    c = q.shape[0]
    n = c // SUB
    assert n > 1
    nh = len(vs)
    sub_iota = lax.broadcasted_iota(jnp.int32, (SUB, LANES), 0)
    zero_blk = jnp.zeros((SUB, LANES), F32)

    r = [jnp.zeros((1, LANES), F32)]
    qt, kh, kt, p_rows = [], [], [], []
    for i in range(n):
        sl = slice(i * SUB, (i + 1) * SUB)
        qi, ki = q[sl], k[sl]
        li = _local_cumsum(la[sl])
        tot = li[SUB - 1:SUB]
        r.append(r[i] + tot)
        qt.append(qi * jnp.exp(li))
        if fast:
            kt.append(ki * jnp.exp(-li))
            kh.append(kt[i] * jnp.exp(tot))
        else:
            kh.append(ki * jnp.exp(tot - li))
            for t in range(SUB):
                d = jnp.where(sub_iota <= t, li[t:t + 1] - li, NEG)
                p_rows.append(jnp.exp(d) * (qi[t:t + 1] * ki))

    qbar = jnp.concatenate([qt[i] * jnp.exp(r[i]) for i in range(n)], axis=0)
    r_ends = jnp.concatenate(r[1:], axis=0)

    def rhs(i):
        g = jnp.exp(jnp.minimum(r[i] - r_ends, 0.0))
        blocks = [kh[j] * g[j:j + 1] if j + 1 < i else kh[j] for j in range(min(i, n))]
        if fast and i < n:
            blocks.append(kt[i])
        blocks += [zero_blk] * (n - len(blocks))
        return jnp.concatenate(blocks, axis=0)

    def mask(x, h):
        return x if masks[h] is None else x * masks[h]

    outs = []
    for h in range(nh):
        o = _dot_t(mask(qbar, h).astype(BF16), states[h].astype(BF16))
        if not fast:
            p_all = jnp.concatenate(p_rows, axis=0).astype(BF16)
            rr = jnp.dot(p_all, ones_ws[h], preferred_element_type=F32)
            vrep = jnp.concatenate([vs[h][i * SUB:(i + 1) * SUB] for i in range(n) for _ in range(SUB)], axis=0)
            o = o + jnp.dot(sel, (rr * vrep).astype(BF16), preferred_element_type=F32)
        outs.append(o)

    first = 0 if fast else 1
    a_rows = [[zero_blk[:, :c]] * first for _ in range(nh)]
    for i in range(first, n):
        lhs = jnp.concatenate([mask(qt[i], h) for h in range(nh)], axis=0).astype(BF16)
        a_i = _dot_t(lhs, rhs(i).astype(BF16))
        for h in range(nh):
            a_rows[h].append(a_i[h * SUB:(h + 1) * SUB])
    if fast:
        causal = lax.broadcasted_iota(jnp.int32, (c, c), 1) <= lax.broadcasted_iota(jnp.int32, (c, c), 0)
    for h in range(nh):
        a = jnp.concatenate(a_rows[h], axis=0)
        if fast:
            a = jnp.where(causal, a, 0.0)
        outs[h] = outs[h] + jnp.dot(a.astype(BF16), vs[h].astype(BF16), preferred_element_type=F32)

    k_end = rhs(n)
    new_states = []
    for h in range(nh):
        upd = lax.dot_general(vs[h].astype(BF16), mask(k_end, h).astype(BF16),
                              (((0,), (0,)), ((), ())), preferred_element_type=F32)
        new_states.append(states[h] * jnp.exp(r[n]) + upd)
    return outs, new_states


def _scan_kernel(*refs, mode, chunk, nh):
    if mode == "gla":
        (zq_ref, zk_ref, zv_ref, zs_ref, zr_ref, wg2_ref, bg_ref, ng_ref, ones_ref, sel_ref,
         o_ref, sout_ref, q_s, k_s, la_s, st_s) = refs
    else:
        (zq_ref, zf_ref, zv_ref, zr_ref, llb_ref, l1m_ref, oml_ref, ng_ref, ones_ref, sel_ref,
         o_ref, sout_ref, q_s, k_s, la_s, st_s) = refs
    ti = pl.program_id(2)
    tblk = q_s.shape[0]

    @pl.when(ti == 0)
    def _():
        st_s[...] = jnp.zeros_like(st_s)

    if mode == "gla":
        dk = LANES // nh
        q_s[...] = zq_ref[...] * dk ** -0.5
        k_s[...] = zk_ref[...]
        zg = jnp.dot(zs_ref[...].astype(BF16), wg2_ref[...], preferred_element_type=F32) + bg_ref[...]
        la_s[...] = _log_sigmoid(zg) * (1.0 / GLA_GATE_TEMP)
        lane = lax.broadcasted_iota(jnp.int32, (1, LANES), 1)
        masks = [((lane >= h * dk) & (lane < (h + 1) * dk)).astype(F32) for h in range(nh)]
    else:
        hf = zf_ref[...]
        e = jnp.exp(-jnp.abs(hf))
        l1pe = jnp.log(1.0 + e)
        inv = 1.0 / (1.0 + e)
        sig_neg = jnp.where(hf > 0, e * inv, inv)
        lsig = jnp.minimum(hf, 0.0) - l1pe
        a = llb_ref[...]
        bb = l1m_ref[...] + lsig
        la_s[...] = jnp.maximum(a, bb) + jnp.log(1.0 + jnp.exp(-jnp.abs(a - bb)))
        k_s[...] = oml_ref[...] * sig_neg
        q_s[...] = _silu(zq_ref[...])
        masks = [None]

    sel = sel_ref[...]
    ones_ws = [ones_ref[h] for h in range(nh)]

    nu = q_s.shape[1] // LANES

    def body(ci, _, fast):
        off = pl.multiple_of(ci * chunk, chunk)
        sl = pl.ds(off, chunk)
        for u in range(nu):
            ul = slice(u * LANES, (u + 1) * LANES)
            hl = [slice((u * nh + h) * LANES, (u * nh + h + 1) * LANES) for h in range(nh)]
            vs = [zv_ref[sl, hl[h]] for h in range(nh)]
            states = [st_s[u * nh + h] for h in range(nh)]
            outs, new_states = _scan_chunk(q_s[sl, ul], k_s[sl, ul], la_s[sl, ul], vs, masks, states, ones_ws,
                                           sel, fast)
            for h in range(nh):
                st_s[u * nh + h] = new_states[h]
                o = _rms(outs[h], ng_ref[...]) * _silu(zr_ref[sl, hl[h]])
                o_ref[0, sl, hl[h]] = o.astype(o_ref.dtype)
        return 0

    safe = jnp.min(la_s[...]) * SUB >= -FAST_BLOCK_DECAY

    @pl.when(safe)
    def _():
        lax.fori_loop(0, tblk // chunk, functools.partial(body, fast=True), 0)

    @pl.when(jnp.logical_not(safe))
    def _():
        lax.fori_loop(0, tblk // chunk, functools.partial(body, fast=False), 0)

    @pl.when(ti == pl.num_programs(2) - 1)
    def _():
        dk_out = LANES // nh
        for u in range(nu):
            for h in range(nh):
                sout_ref[0, u * nh + h] = st_s[u * nh + h].T[h * dk_out:(h + 1) * dk_out, :]


def _scan_consts(chunk, nh):
    n = chunk // SUB
    sel = np.zeros((chunk, n * SUB * SUB), np.float32)
    for i in range(n):
        for t in range(SUB):
            sel[i * SUB + t, i * 64 + t * SUB:i * 64 + (t + 1) * SUB] = 1.0
    ones = np.zeros((nh, LANES, LANES), np.float32)
    dk = LANES // nh
    for h in range(nh):
        ones[h, h * dk:(h + 1) * dk, :] = 1.0
    return jnp.asarray(ones, BF16), jnp.asarray(sel, BF16)


def _lane_block(col, name, width):
    assert (col[name] * LANES) % width == 0
    return col[name] * LANES // width


def _scan_gla(z, b, t, units, nu, col, wg2p, bg, ng, tblk, chunk):
    nh = 2
    nt = t // tblk
    ones, sel = _scan_consts(chunk, nh)
    kw, vw = nu * LANES, nu * nh * LANES
    zspec = lambda name, w: pl.BlockSpec((tblk, w), lambda i, u, j: (i * nt + j, _lane_block(col, name, w) + u))
    in_specs = [
        zspec("gq", kw), zspec("gk", kw), zspec("gv", vw),
        pl.BlockSpec((tblk, LANES), lambda i, u, j: (i * nt + j, col["small"])),
        zspec("gr", vw),
        pl.BlockSpec((LANES, kw), lambda i, u, j: (0, u)),
        pl.BlockSpec((1, kw), lambda i, u, j: (0, u)),
        pl.BlockSpec((1, LANES), lambda i, u, j: (0, 0)),
        pl.BlockSpec(ones.shape, lambda i, u, j: (0, 0, 0)),
        pl.BlockSpec(sel.shape, lambda i, u, j: (0, 0)),
    ]
    dk = LANES // nh
    return pl.pallas_call(
        functools.partial(_scan_kernel, mode="gla", chunk=chunk, nh=nh),
        grid=(b, units // nu, nt),
        in_specs=in_specs,
        out_specs=[pl.BlockSpec((1, tblk, vw), lambda i, u, j: (i, j, u)),
                   pl.BlockSpec((1, nu * nh, dk, LANES), lambda i, u, j: (i, u, 0, 0))],
        out_shape=[jax.ShapeDtypeStruct((b, t, units * nh * LANES), BF16),
                   jax.ShapeDtypeStruct((b, units * nh, dk, LANES), F32)],
        scratch_shapes=[pltpu.VMEM((tblk, kw), F32)] * 3 + [pltpu.VMEM((nu * nh, LANES, LANES), F32)],
        compiler_params=_cparams(("parallel", "parallel", "arbitrary")),
        name="scan_gla",
    )(z, z, z, z, z, wg2p, bg, ng, ones, sel)


def _scan_hgrn(z, b, t, units, nu, col, llb, l1m, oml, ng, tblk, chunk):
    nh = 1
    nt = t // tblk
    ones, sel = _scan_consts(chunk, nh)
    kw = nu * LANES
    zspec = lambda name: pl.BlockSpec((tblk, kw), lambda i, u, j: (i * nt + j, _lane_block(col, name, kw) + u))
    pspec = pl.BlockSpec((1, kw), lambda i, u, j: (0, u))
    in_specs = [zspec("hq"), zspec("hf"), zspec("hi"), zspec("hg"), pspec, pspec, pspec,
                pl.BlockSpec((1, LANES), lambda i, u, j: (0, 0)),
                pl.BlockSpec(ones.shape, lambda i, u, j: (0, 0, 0)),
                pl.BlockSpec(sel.shape, lambda i, u, j: (0, 0))]
    return pl.pallas_call(
        functools.partial(_scan_kernel, mode="hgrn", chunk=chunk, nh=nh),
        grid=(b, units // nu, nt),
        in_specs=in_specs,
        out_specs=[pl.BlockSpec((1, tblk, kw), lambda i, u, j: (i, j, u)),
                   pl.BlockSpec((1, nu, LANES, LANES), lambda i, u, j: (i, u, 0, 0))],
        out_shape=[jax.ShapeDtypeStruct((b, t, units * LANES), BF16),
                   jax.ShapeDtypeStruct((b, units, LANES, LANES), F32)],
        scratch_shapes=[pltpu.VMEM((tblk, kw), F32)] * 3 + [pltpu.VMEM((nu, LANES, LANES), F32)],
        compiler_params=_cparams(("parallel", "parallel", "arbitrary")),
        name="scan_hgrn",
    )(z, z, z, z, llb, l1m, oml, ng, ones, sel)


def _rec_kernel(*refs, mode, nseq, ntok, dk):
    if mode == "gla":
        (qt_ref, kt_ref, st_ref, wg2t_ref, bgt_ref, zv_ref, zr_ref, ng_ref, s0_ref, o_ref, sout_ref, o_s) = refs
        qc = qt_ref[...] * dk ** -0.5
        kc = kt_ref[...]
        zg = jnp.dot(wg2t_ref[...], st_ref[...].astype(BF16), preferred_element_type=F32) + bgt_ref[...]
        ac = jnp.exp(_log_sigmoid(zg) * (1.0 / GLA_GATE_TEMP))
    else:
        (qt_ref, ft_ref, llb_ref, l1m_ref, oml_ref, zv_ref, zr_ref, ng_ref, s0_ref, o_ref, sout_ref, o_s) = refs
        hf = ft_ref[...]
        lsig = _log_sigmoid(hf)
        a = llb_ref[...]
        bb = l1m_ref[...] + lsig
        ac = jnp.exp(jnp.maximum(a, bb) + jnp.log(1.0 + jnp.exp(-jnp.abs(a - bb))))
        kc = oml_ref[...] * _sigmoid(-hf)
        qc = _silu(qt_ref[...])
    for sq in range(nseq):
        s = s0_ref[sq, 0]
        for t in range(ntok):
            j = sq * ntok + t
            vrow = zv_ref[j:j + 1, :]
            s = s * ac[:, j:j + 1] + kc[:, j:j + 1] * vrow
            o_s[j:j + 1, :] = jnp.sum(s * qc[:, j:j + 1], axis=0, keepdims=True)
        sout_ref[sq, 0] = s
    o = _rms(o_s[...], ng_ref[...]) * _silu(zr_ref[...])
    o_ref[...] = o.astype(o_ref.dtype)


def _rec(mode, zt, z, col, params, ng, s0, nb, ntok, heads, dk, nseq):
    m = nb * ntok
    rows = nseq * ntok
    ng_groups = nb // nseq
    if mode == "gla":
        wg2t, bgt = params
        per = LANES // dk
        tspec = lambda name: pl.BlockSpec((None, dk, rows), lambda h, g: (g, col[name] * per + h, 0))
        in_specs = [tspec("gq"), tspec("gk"),
                    pl.BlockSpec((None, LANES, rows), lambda h, g: (g, col["small"], 0)),
                    pl.BlockSpec((dk, LANES), lambda h, g: (h, 0)),
                    pl.BlockSpec((dk, 1), lambda h, g: (h, 0)),
                    pl.BlockSpec((rows, LANES), lambda h, g: (g, col["gv"] + h)),
                    pl.BlockSpec((rows, LANES), lambda h, g: (g, col["gr"] + h))]
        args = (zt, zt, zt, wg2t, bgt, z, z)
    else:
        llb, l1m, oml = params
        tspec = lambda name: pl.BlockSpec((None, dk, rows), lambda h, g: (g, col[name] + h, 0))
        pspec = pl.BlockSpec((dk, 1), lambda h, g: (h, 0))
        in_specs = [tspec("hq"), tspec("hf"), pspec, pspec, pspec,
                    pl.BlockSpec((rows, LANES), lambda h, g: (g, col["hi"] + h)),
                    pl.BlockSpec((rows, LANES), lambda h, g: (g, col["hg"] + h))]
        args = (zt, zt, llb, l1m, oml, z, z)
    in_specs += [pl.BlockSpec((1, LANES), lambda h, g: (0, 0)),
                 pl.BlockSpec((nseq, 1, dk, LANES), lambda h, g: (g, h, 0, 0))]
    return pl.pallas_call(
        functools.partial(_rec_kernel, mode=mode, nseq=nseq, ntok=ntok, dk=dk),
        grid=(heads, ng_groups),
        in_specs=in_specs,
        out_specs=[pl.BlockSpec((rows, LANES), lambda h, g: (g, h)),
                   pl.BlockSpec((nseq, 1, dk, LANES), lambda h, g: (g, h, 0, 0))],
        out_shape=[jax.ShapeDtypeStruct((m, heads * LANES), BF16),
                   jax.ShapeDtypeStruct(s0.shape, F32)],
        scratch_shapes=[pltpu.VMEM((rows, LANES), F32)],
        compiler_params=_cparams(("parallel", "parallel")),
        name="rec_" + mode,
    )(*args, ng, s0)


def _lfpool_kernel(lf_ref, o_ref, *, nh):
    x = lf_ref[...]
    w = x.shape[1]
    lane = lax.broadcasted_iota(jnp.int32, x.shape, 1)
    incl = x
    sh = nh
    while sh < w:
        incl = incl + jnp.where(lane < w - sh, pltpu.roll(incl, w - sh, 1), 0.0)
        sh *= 2
    tot = jnp.where(lane < nh, incl, 0.0)
    sh = nh
    while sh < w:
        tot = tot + pltpu.roll(tot, sh, 1)
        sh *= 2
    o_ref[:, :w] = incl - x
    o_ref[:, w:] = tot


def _lfpool(clf, nh, rows):
    depth, n_pool, w = clf.shape
    return pl.pallas_call(
        functools.partial(_lfpool_kernel, nh=nh),
        grid=(depth, n_pool // rows),
        in_specs=[pl.BlockSpec((None, rows, w), lambda l, i: (l, i, 0))],
        out_specs=pl.BlockSpec((None, rows, 2 * w), lambda l, i: (l, i, 0)),
        out_shape=jax.ShapeDtypeStruct((depth, n_pool, 2 * w), F32),
        compiler_params=_cparams(("parallel", "parallel")),
        name="lfpool",
    )(clf)


def _paged_kernel(pt_ref, q_ref, kn_ref, vn_ref, zs_ref, bf_ref, mask_ref, *rest, npg, nh, dh, ntok, scale):
    k_refs = rest[:npg]
    v_refs = rest[npg:2 * npg]
    rt_ref, o_ref, lf_ref = rest[2 * npg:2 * npg + 3]
    m_s, l_s, acc_s, car_s = rest[2 * npg + 3:]
    g = pl.program_id(1)

    @pl.when(g == 0)
    def _():
        m_s[...] = jnp.full(m_s.shape, NEG, F32)
        l_s[...] = jnp.zeros_like(l_s)
        acc_s[...] = jnp.zeros_like(acc_s)
        car_s[...] = jnp.zeros_like(car_s)

    q = q_ref[0] * scale
    q_all = jnp.concatenate([q[:, h * dh:(h + 1) * dh] for h in range(nh)], axis=0).astype(BF16)
    w = mask_ref.shape[1]
    carry = car_s[...]
    s_pages = []
    for i in range(npg):
        rt = rt_ref[0, i:i + 1, :]
        bias = rt[:, :w] + carry
        carry = carry + rt[:, w:]
        s_pages.append(_dot_t(q_all, k_refs[i][...].astype(BF16)) + (mask_ref[...] + bias))
    car_s[...] = carry
    s = jnp.concatenate(s_pages, axis=1)
    m = m_s[...]
    m_new = jnp.maximum(m, jnp.max(s, axis=-1, keepdims=True))
    alpha = jnp.exp(m - m_new)
    p = jnp.exp(s - m_new)
    l_s[...] = alpha * l_s[...] + jnp.sum(p, axis=-1, keepdims=True)
    pv = jnp.zeros(acc_s.shape, F32)
    for i in range(npg):
        pv = pv + jnp.dot(p[:, i * w:(i + 1) * w].astype(BF16), v_refs[i][...].astype(BF16),
                          preferred_element_type=F32)
    acc_s[...] = alpha * acc_s[...] + pv
    m_s[...] = m_new

    @pl.when(g == pl.num_programs(1) - 1)
    def _():
        lfn = _log_sigmoid(zs_ref[0] + bf_ref[...])
        lf_ref[0] = lfn
        row = lax.broadcasted_iota(jnp.int32, lfn.shape, 0)
        cn = _local_cumsum(jnp.where(row < ntok, lfn, 0.0))
        trow = lax.broadcasted_iota(jnp.int32, (SUB, 1), 0)
        for h in range(nh):
            hs = slice(h * dh, (h + 1) * dh)
            rs = slice(h * SUB, (h + 1) * SUB)
            m, l, acc = m_s[rs, :], l_s[rs, :], acc_s[rs, :]
            for sp in range(ntok):
                logit = jnp.sum(q[:, hs] * kn_ref[0, sp:sp + 1, hs], axis=-1, keepdims=True)
                logit = logit - cn[sp:sp + 1, FF_OFF + h:FF_OFF + h + 1]
                logit = jnp.where(trow >= sp, logit, NEG)
                m_new = jnp.maximum(m, logit)
                alpha = jnp.exp(m - m_new)
                p = jnp.exp(logit - m_new)
                l = alpha * l + p
                acc = alpha * acc + p * vn_ref[0, sp:sp + 1, hs]
                m = m_new
            o_ref[0, :, hs] = (acc / l).astype(o_ref.dtype)


def _paged(layer, page_table, q, kn, vn, zs, bf_row, cache_k, cache_v, rt_seq, nh, dh, ntok, npg):
    nb = q.shape[0]
    w = cache_k.shape[2]
    npages = page_table.shape[1]
    ngrp = npages // npg
    mask = np.full((nh * SUB, w), NEG, np.float32)
    for h in range(nh):
        mask[h * SUB:(h + 1) * SUB, h::nh] = 0.0

    def page_of(b, g, pt, i):
        return pt[b, npages - 1 - (g * npg + i)]

    seq3 = lambda b, g, pt: (b, 0, 0)
    in_specs = [pl.BlockSpec((1, SUB, nh * dh), seq3)] * 3 + [
        pl.BlockSpec((1, SUB, LANES), seq3),
        pl.BlockSpec((1, LANES), lambda b, g, pt: (0, 0)),
        pl.BlockSpec(mask.shape, lambda b, g, pt: (0, 0))]
    kv_specs = [pl.BlockSpec((None, None, w, dh), lambda b, g, pt, i=i: (layer, page_of(b, g, pt, i), 0, 0))
                for i in range(npg)]
    in_specs += kv_specs + kv_specs
    in_specs += [pl.BlockSpec((1, npg, 2 * w), lambda b, g, pt: (b, g, 0))]
    grid_spec = pltpu.PrefetchScalarGridSpec(
        num_scalar_prefetch=1, grid=(nb, ngrp), in_specs=in_specs,
        out_specs=[pl.BlockSpec((1, SUB, nh * dh), seq3), pl.BlockSpec((1, SUB, LANES), seq3)],
        scratch_shapes=[pltpu.VMEM((nh * SUB, 1), F32), pltpu.VMEM((nh * SUB, 1), F32),
                        pltpu.VMEM((nh * SUB, dh), F32), pltpu.VMEM((1, w), F32)])
    return pl.pallas_call(
        functools.partial(_paged_kernel, npg=npg, nh=nh, dh=dh, ntok=ntok, scale=dh ** -0.5),
        grid_spec=grid_spec,
        out_shape=[jax.ShapeDtypeStruct((nb, SUB, nh * dh), F32),
                   jax.ShapeDtypeStruct((nb, SUB, LANES), F32)],
        compiler_params=_cparams(("parallel", "arbitrary")),
        name="paged",
    )(page_table, q, kn, vn, zs, bf_row, jnp.asarray(mask), *([cache_k] * npg), *([cache_v] * npg), rt_seq)


def _merge_kernel(x_ref, oa_ref, ob_ref, oc_ref, g1_ref, wg_ref, wa_ref, wb_ref, wc_ref, wo_ref, y_ref):
    x = x_ref[...]
    d = x.shape[1]
    xn = _rms(x, g1_ref[...]).astype(BF16)
    merged = jnp.zeros(x.shape, F32)
    for i, (o_ref, w_ref) in enumerate(((oa_ref, wa_ref), (ob_ref, wb_ref), (oc_ref, wc_ref))):
        gate = _sigmoid(jnp.dot(xn, wg_ref[:, i * d:(i + 1) * d], preferred_element_type=F32))
        merged = merged + gate * jnp.dot(o_ref[...], w_ref[...], preferred_element_type=F32)
    y_ref[...] = x + jnp.dot(merged.astype(BF16), wo_ref[...], preferred_element_type=F32)


def _merge(x, oa, ob, oc, g1, wg_all, layer, wa, wb, wc, wo, tm):
    m, d = x.shape
    const = lambda a: pl.BlockSpec(a.shape, lambda i: (0,) * a.ndim)
    rows = lambda a: pl.BlockSpec((tm, a.shape[1]), lambda i: (i, 0))
    return pl.pallas_call(
        _merge_kernel,
        grid=(m // tm,),
        in_specs=[rows(x), rows(oa), rows(ob), rows(oc), const(g1),
                  pl.BlockSpec((None,) + wg_all.shape[1:], lambda i: (layer, 0, 0)),
                  const(wa), const(wb), const(wc), const(wo)],
        out_specs=pl.BlockSpec((tm, d), lambda i: (i, 0)),
        out_shape=jax.ShapeDtypeStruct((m, d), F32),
        compiler_params=_cparams(("parallel",)),
        name="merge",
    )(x, oa, ob, oc, g1, wg_all, wa, wb, wc, wo)


def _ffn_kernel(x_ref, g2_ref, wg_ref, wu_ref, wd_ref, gf_ref, y_ref, *, final):
    x = x_ref[...]
    h = _rms(x, g2_ref[...]).astype(BF16)
    acc = x
    hidden = wg_ref.shape[1]
    for c0 in range(0, hidden, FFN_HIDDEN_CHUNK):
        cs = slice(c0, min(c0 + FFN_HIDDEN_CHUNK, hidden))
        a = jnp.dot(h, wg_ref[:, cs], preferred_element_type=F32)
        u = jnp.dot(h, wu_ref[:, cs], preferred_element_type=F32)
        acc = acc + jnp.dot((_silu(a) * u).astype(BF16), wd_ref[cs, :], preferred_element_type=F32)
    y_ref[...] = _rms(acc, gf_ref[...]) if final else acc


def _ffn(x, g2, wg, wu, wd, gf, tm, final):
    m, d = x.shape
    const = lambda a: pl.BlockSpec(a.shape, lambda i: (0,) * a.ndim)
    return pl.pallas_call(
        functools.partial(_ffn_kernel, final=final),
        grid=(m // tm,),
        in_specs=[pl.BlockSpec((tm, d), lambda i: (i, 0)), const(g2), const(wg), const(wu), const(wd), const(gf)],
        out_specs=pl.BlockSpec((tm, d), lambda i: (i, 0)),
        out_shape=jax.ShapeDtypeStruct((m, d), F32),
        compiler_params=_cparams(("parallel",)),
        name="ffn",
    )(x, g2, wg, wu, wd, gf)


def _pick(n, pref):
    for c in pref:
        if n % c == 0:
            return c
    return n


def kernel(x_prompt, x_sample, state_gla, cache_fox_k, cache_fox_v, cache_fox_logf, state_hgrn, page_table,
           norm1_g, w_in, gla_wg2, gla_bg, gla_norm_g, fox_bf, hg_lb_logits, hg_norm_g,
           w_branch_a, w_branch_b, w_branch_c, w_out, norm2_g, w_ffn_gate, w_ffn_up, w_ffn_down,
           final_norm_g):
    depth, d_model, _ = w_in.shape
    bp, tp, _ = x_prompt.shape
    nb, ntok, _ = x_sample.shape
    _, _, gh, gdk, gdv = state_gla.shape
    _, _, hh, hdk, hdv = state_hgrn.shape
    _, n_pool, page, fh, fdh = cache_fox_k.shape
    hidden = w_ffn_gate.shape[2]
    assert gdv == LANES and hdk == LANES and hdv == LANES and fdh == LANES and 2 * gdk == LANES
    assert ntok <= SUB and gh % 2 == 0

    gq_w, gv_w, f_w, h_w = gh * gdk, gh * gdv, fh * fdh, hh * hdk
    names = ["gq", "gk", "gv", "glr", "gr", "fq", "fk", "fv", "ff", "hq", "hf", "hi", "hg", "ga", "gb", "gc"]
    widths = [gq_w, gq_w, gv_w, GLA_GATE_RANK, gv_w, f_w, f_w, f_w, fh, h_w, h_w, h_w, h_w,
              d_model, d_model, d_model]
    starts = dict(zip(names, np.concatenate([[0], np.cumsum(widths)[:-1]]).tolist()))
    wid = dict(zip(names, widths))
    order = ["gq", "gk", "gv", "gr", "fq", "fk", "fv", "hq", "hf", "hi", "hg"]
    col, off = {}, 0
    for nm in order:
        col[nm] = off // LANES
        off += wid[nm]
    col["small"] = off // LANES
    n_used = off + LANES
    tn = min(14, n_used // LANES) * LANES
    n_pad = -(-n_used // tn) * tn

    segs = [(starts[nm], col[nm] * LANES, wid[nm]) for nm in order]
    segs += [(starts["ff"], col["small"] * LANES + FF_OFF, fh),
             (starts["glr"], col["small"] * LANES + GLR_OFF, GLA_GATE_RANK)]
    w_pad_all, w_gates_all = _wprep(w_in, segs, starts["ga"], 3 * d_model, n_pad, _pick(d_model, (128, 64, 32, 16, 8)))

    lb_cum = jnp.cumsum(jax.nn.softmax(hg_lb_logits.astype(F32), axis=0), axis=0)
    hg_lb = lb_cum - lb_cum[:1]
    log_lb, log1m_lb, one_m_lb = jnp.log(hg_lb), jnp.log1p(-hg_lb), 1.0 - hg_lb


    xp = x_prompt.reshape(bp * tp, d_model)
    xs = jnp.pad(x_sample, ((0, 0), (0, SUB - ntok), (0, 0))).reshape(nb * SUB, d_model)
    ms = nb * SUB

    tm_p = _pick(bp * tp, (1024, 512, 256, 128))
    tt = _pick(tp, (256, 128))
    fblk = _pick(tp, (1024, 512, 256))
    tblk = _pick(tp, (512, 256, 128))
    chunk = min(128, tblk)
    tm_e = _pick(bp * tp, (512, 256, 128))
    nseq = _pick(nb, (8, 4, 2, 1))
    npg = _pick(page_table.shape[1], (16, 8, 4, 2, 1))

    ck = cache_fox_k.reshape(depth, n_pool, page * fh, fdh)
    cv = cache_fox_v.reshape(depth, n_pool, page * fh, fdh)
    rtot = _lfpool(cache_fox_logf.astype(F32).reshape(depth, n_pool, page * fh), fh, _pick(n_pool, (256, 128, 64, 32, 16, 8)))

    outs = {k: [] for k in ("gla_p", "gla_s", "k_p", "v_p", "lf_p", "k_s", "v_s", "lf_s", "hg_p", "hg_s")}

    for l in range(depth):
        g1 = norm1_g[l].reshape(1, d_model)
        g2 = norm2_g[l].reshape(1, d_model)
        bf_row = jnp.zeros((1, LANES), F32).at[0, FF_OFF:FF_OFF + fh].set(fox_bf[l])
        wg2p = jnp.zeros((LANES, gq_w), F32).at[GLR_OFF:GLR_OFF + GLA_GATE_RANK].set(gla_wg2[l])
        wg2_b = wg2p.astype(BF16)
        bg_row = gla_bg[l].reshape(1, gq_w)
        gng = gla_norm_g[l].reshape(1, LANES)
        hng = hg_norm_g[l].reshape(1, LANES)
        llb_u, l1m_u, oml_u = (a[l].reshape(1, h_w) for a in (log_lb, log1m_lb, one_m_lb))
        wa, wb, wc, wo = (w[l].astype(BF16) for w in (w_branch_a, w_branch_b, w_branch_c, w_out))
        wfg, wfu, wfd = (w[l].astype(BF16) for w in (w_ffn_gate, w_ffn_up, w_ffn_down))
        gf = final_norm_g.reshape(1, d_model)
        final = l == depth - 1

        z, fk, fv = _proj(xp, g1, w_pad_all, l, tm_p, tn, (col["fk"] * LANES, col["fv"] * LANES, fh, fdh))
        outs["k_p"].append(fk.reshape(bp, tp, fh, fdh))
        outs["v_p"].append(fv.reshape(bp, tp, fh, fdh))
        lf, ctok = _fox_prep(z, bf_row, bp, tp, col["small"], fh, tt)
        outs["lf_p"].append(lf)
        ob = _flash(z, ctok, bp, tp, fh, fdh, col["fq"], col["fk"], col["fv"], fblk)
        oa, sg = _scan_gla(z, bp, tp, gh // 2, GLA_UNITS_PER_STEP, col, wg2_b, bg_row, gng, tblk, chunk)
        outs["gla_p"].append(sg)
        oc, sh = _scan_hgrn(z, bp, tp, hh, HGRN_UNITS_PER_STEP, col, llb_u, l1m_u, oml_u, hng, tblk, chunk)
        outs["hg_p"].append(sh)
        x1 = _merge(xp, oa.reshape(bp * tp, -1), ob.reshape(bp * tp, -1), oc.reshape(bp * tp, -1),
                    g1, w_gates_all, l, wa, wb, wc, wo, tm_e)
        xp = _ffn(x1, g2, wfg, wfu, wfd, gf, tm_e, final)

        zs, = _proj(xs, g1, w_pad_all, l, ms, tn)
        zs3 = zs.reshape(nb, SUB, n_pad)
        take = lambda nm, w: zs3[:, :, col[nm] * LANES:col[nm] * LANES + w]
        fks, fvs = take("fk", f_w), take("fv", f_w)
        outs["k_s"].append(fks[:, :ntok].reshape(nb, ntok, fh, fdh))
        outs["v_s"].append(fvs[:, :ntok].reshape(nb, ntok, fh, fdh))
        small_s = take("small", LANES)
        rt_seq = jnp.take(rtot[l], page_table[:, ::-1], axis=0)
        obs, lfs = _paged(l, page_table, take("fq", f_w), fks, fvs, small_s, bf_row, ck, cv, rt_seq,
                          fh, fdh, ntok, npg)
        outs["lf_s"].append(lfs[:, :ntok, FF_OFF:FF_OFF + fh])
        obs = obs.astype(BF16)
        zc = zs3[:, :ntok].reshape(nb * ntok, n_pad)
        zt = zc.reshape(nb // nseq, nseq * ntok, n_pad).transpose(0, 2, 1)
        wg2t = wg2p.T.astype(BF16)
        oas, sgs = _rec("gla", zt, zc, col, (wg2t, gla_bg[l].reshape(-1, 1)), gng, state_gla[l],
                        nb, ntok, gh, gdk, nseq)
        outs["gla_s"].append(sgs)
        ocs, shs = _rec("hgrn", zt, zc, col, tuple(a[l].reshape(-1, 1) for a in (log_lb, log1m_lb, one_m_lb)),
                        hng, state_hgrn[l], nb, ntok, hh, hdk, nseq)
        outs["hg_s"].append(shs)
        pad_tok = lambda o: jnp.pad(o.reshape(nb, ntok, -1), ((0, 0), (0, SUB - ntok), (0, 0))).reshape(ms, -1)
        x1s = _merge(xs, pad_tok(oas), obs.reshape(ms, -1), pad_tok(ocs), g1, w_gates_all, l, wa, wb, wc, wo, ms)
        xs = _ffn(x1s, g2, wfg, wfu, wfd, gf, ms, final)

    st = lambda k: jnp.stack(outs[k])
    y_p = xp.reshape(bp, tp, d_model)
    y_s = xs.reshape(nb, SUB, d_model)[:, :ntok]
    return (y_p, y_s, st("gla_p"), st("gla_s"), st("k_p"), st("v_p"), st("lf_p"),
            st("k_s"), st("v_s"), st("lf_s"), st("hg_p"), st("hg_s"))
```

```python
import functools

import numpy as np
import jax
import jax.numpy as jnp
from jax import lax
from jax.experimental import pallas as pl
from jax.experimental.pallas import tpu as pltpu

F32 = jnp.float32
BF16 = jnp.bfloat16
EPS = 1e-6
NEG = -1e30
LANES = 128
SUB = 8
HI = lax.Precision.HIGHEST
VMEM_LIMIT = 56 * 1024 * 1024

FAST_BLOCK_DECAY = 60.0
GLA_UNITS_PER_STEP = 2
HGRN_UNITS_PER_STEP = 4
FFN_HIDDEN_CHUNK = 256
GLA_GATE_TEMP = 16.0
GLA_GATE_RANK = 16
FF_OFF = 0
GLR_OFF = 16


def _cparams(sem):
    return pltpu.CompilerParams(dimension_semantics=sem, vmem_limit_bytes=VMEM_LIMIT)


def _rms(x, g):
    return x * lax.rsqrt(jnp.mean(x * x, axis=-1, keepdims=True) + EPS) * g


def _sigmoid(x):
    return 1.0 / (1.0 + jnp.exp(-x))


def _log_sigmoid(x):
    return jnp.minimum(x, 0.0) - jnp.log(1.0 + jnp.exp(-jnp.abs(x)))


def _silu(x):
    return x * _sigmoid(x)


def _dot_t(a, b):
    return lax.dot_general(a, b, (((1,), (1,)), ((), ())), preferred_element_type=F32)


def _wprep_kernel(w_ref, wp_ref, wg_ref, *, segs, gate_src):
    wp_ref[...] = jnp.zeros_like(wp_ref)
    for src, dst, width in segs:
        wp_ref[:, dst:dst + width] = w_ref[:, src:src + width].astype(BF16)
    wg_ref[...] = w_ref[:, gate_src:gate_src + wg_ref.shape[1]].astype(BF16)


def _wprep(w_in, segs, gate_src, gate_w, n_pad, tr):
    depth, d, n_in = w_in.shape
    return pl.pallas_call(
        functools.partial(_wprep_kernel, segs=segs, gate_src=gate_src),
        grid=(depth, d // tr),
        in_specs=[pl.BlockSpec((None, tr, n_in), lambda l, i: (l, i, 0))],
        out_specs=[pl.BlockSpec((None, tr, n_pad), lambda l, i: (l, i, 0)),
                   pl.BlockSpec((None, tr, gate_w), lambda l, i: (l, i, 0))],
        out_shape=[jax.ShapeDtypeStruct((depth, d, n_pad), BF16),
                   jax.ShapeDtypeStruct((depth, d, gate_w), BF16)],
        compiler_params=_cparams(("parallel", "parallel")),
        name="wprep",
    )(w_in)


def _proj_kernel(x_ref, g_ref, w_ref, z_ref, *rest, kv):
    xn_ref = rest[-1]

    @pl.when(pl.program_id(1) == 0)
    def _():
        xn_ref[...] = _rms(x_ref[...], g_ref[...]).astype(BF16)

    zt = jnp.dot(xn_ref[...], w_ref[...], preferred_element_type=F32)
    z_ref[...] = zt
    if kv is not None:
        fk_ref, fv_ref = rest[:2]
        jkv, koff, voff, nh, dh = kv
        tm = zt.shape[0]

        @pl.when(pl.program_id(1) == jkv)
        def _():
            for h in range(nh):
                fk_ref[pl.ds(h, tm, stride=nh), :] = zt[:, koff + h * dh:koff + (h + 1) * dh]
                fv_ref[pl.ds(h, tm, stride=nh), :] = zt[:, voff + h * dh:voff + (h + 1) * dh]


def _proj(x, g, w_all, layer, tm, tn, kv_cols=None):
    m, d = x.shape
    n = w_all.shape[2]
    out_specs = [pl.BlockSpec((tm, tn), lambda i, j: (i, j))]
    out_shape = [jax.ShapeDtypeStruct((m, n), F32)]
    kv = None
    if kv_cols is not None:
        kc, vc, nh, dh = kv_cols
        assert kc // tn == (vc + nh * dh - 1) // tn
        kv = (kc // tn, kc % tn, vc % tn, nh, dh)
        out_specs += [pl.BlockSpec((tm * nh, dh), lambda i, j: (i, 0))] * 2
        out_shape += [jax.ShapeDtypeStruct((m * nh, dh), F32)] * 2
    return pl.pallas_call(
        functools.partial(_proj_kernel, kv=kv),
        grid=(m // tm, n // tn),
        in_specs=[pl.BlockSpec((tm, d), lambda i, j: (i, 0)),
                  pl.BlockSpec((1, d), lambda i, j: (0, 0)),
                  pl.BlockSpec((None, d, tn), lambda i, j: (layer, 0, j))],
        out_specs=out_specs,
        out_shape=out_shape,
        scratch_shapes=[pltpu.VMEM((tm, d), BF16)],
        compiler_params=_cparams(("parallel", "arbitrary")),
        name="proj",
    )(x, g, w_all)


def _fox_prep_kernel(zs_ref, bf_ref, tril_ref, lf_ref, c_ref, carry_ref, *, nh):
    @pl.when(pl.program_id(1) == 0)
    def _():
        carry_ref[...] = jnp.zeros_like(carry_ref)

    lf = _log_sigmoid(zs_ref[...] + bf_ref[...])
    lf_ref[0] = lf[:, FF_OFF:FF_OFF + nh]
    grp = tril_ref.shape[0]
    carry = carry_ref[...][:1]
    for g0 in range(0, lf.shape[0], grp):
        c = jnp.dot(tril_ref[...], lf[g0:g0 + grp], precision=HI, preferred_element_type=F32) + carry
        c_ref[0, g0:g0 + grp, :] = c
        carry = c[-1:]
    carry_ref[...] = jnp.broadcast_to(carry, carry_ref.shape)


def _fox_prep(z, bf_row, b, t, small_blk, nh, tt):
    grp = min(tt, LANES)
    tril = np.tril(np.ones((grp, grp), np.float32))
    nt = t // tt
    return pl.pallas_call(
        functools.partial(_fox_prep_kernel, nh=nh),
        grid=(b, nt),
        in_specs=[pl.BlockSpec((tt, LANES), lambda i, j: (i * nt + j, small_blk)),
                  pl.BlockSpec((1, LANES), lambda i, j: (0, 0)),
                  pl.BlockSpec((grp, grp), lambda i, j: (0, 0))],
        out_specs=[pl.BlockSpec((1, tt, nh), lambda i, j: (i, j, 0)),
                   pl.BlockSpec((1, tt, LANES), lambda i, j: (i, j, 0))],
        out_shape=[jax.ShapeDtypeStruct((b, t, nh), F32),
                   jax.ShapeDtypeStruct((b, t, LANES), F32)],
        scratch_shapes=[pltpu.VMEM((SUB, LANES), F32)],
        compiler_params=_cparams(("parallel", "arbitrary")),
        name="fox_prep",
    )(z, bf_row, jnp.asarray(tril))


BIAS_PIECES = 3
LOG2E = 1.4426950408889634


def _flash_kernel(q_ref, k_ref, v_ref, c_ref, eye_ref, place_ref, o_ref, kb_ref, vt_ref, sa_ref, sb_ref, *,
                  blk, scale, nh):
    h = pl.program_id(1)
    qi = pl.program_id(2)
    dh = q_ref.shape[1]

    @pl.when(qi == 0)
    def _():
        def prep(j, _):
            rows = pl.ds(pl.multiple_of(j * blk, blk), blk)
            kb_ref[rows, :dh] = k_ref[rows, :].astype(BF16)
            vt_ref[j] = _dot_t(eye_ref[...], v_ref[rows, :].astype(BF16)).astype(BF16)
            rem = c_ref[0, rows, :] * (-LOG2E)
            extra = jnp.zeros(rem.shape, F32)
            for piece in range(BIAS_PIECES):
                part = rem.astype(BF16)
                extra = extra + jnp.dot(part, place_ref[piece], preferred_element_type=F32)
                rem = rem - part.astype(F32)
            kb_ref[rows, dh:] = extra.astype(BF16)
            return 0

        lax.fori_loop(0, k_ref.shape[0] // blk, prep, 0)

    qblk = q_ref.shape[0]
    lane_q = lax.broadcasted_iota(jnp.int32, (qblk, dh), 1)
    own = (lane_q >= h * BIAS_PIECES) & (lane_q < (h + 1) * BIAS_PIECES)
    q = jnp.concatenate([(q_ref[...] * (scale * LOG2E)).astype(BF16),
                         jnp.where(own, 1.0, 0.0).astype(BF16)], axis=1)

    def scores(j):
        return _dot_t(kb_ref[pl.ds(pl.multiple_of(j * blk, blk), blk), :], q)

    def absorb(s_ref, j, carry, key_off):
        m, l, acc = carry
        s = s_ref[...]
        if key_off is not None:
            key = lax.broadcasted_iota(jnp.int32, (blk, qblk), 0) + key_off
            qry = lax.broadcasted_iota(jnp.int32, (blk, qblk), 1)
            s = jnp.where(key <= qry, s, NEG)
        m_new = jnp.maximum(m, jnp.max(s, axis=0, keepdims=True))
        alpha = jnp.exp2(m - m_new)
        p = jnp.exp2(s - m_new)
        l = alpha * l + jnp.sum(p, axis=0, keepdims=True)
        acc = alpha * acc + jnp.dot(vt_ref[j], p.astype(BF16), preferred_element_type=F32)
        return m_new, l, acc

    sa_ref[...] = scores(0)

    def pair(p, carry):
        j = 2 * p
        sb_ref[...] = scores(j + 1)
        carry = absorb(sa_ref, j, carry, None)
        sa_ref[...] = scores(j + 2)
        return absorb(sb_ref, j + 1, carry, None)

    init = (jnp.full((1, qblk), NEG, F32), jnp.zeros((1, qblk), F32), jnp.zeros((dh, qblk), F32))
    carry = lax.fori_loop(0, qi, pair, init)
    sb_ref[...] = scores(2 * qi + 1)
    carry = absorb(sa_ref, 2 * qi, carry, 0)
    m, l, acc = absorb(sb_ref, 2 * qi + 1, carry, blk)
    o_ref[0] = (acc / l).T.astype(o_ref.dtype)


def _flash(z, ctok, b, t, nh, dh, q_blk0, k_blk0, v_blk0, qblk):
    assert FF_OFF + nh <= LANES and nh * BIAS_PIECES <= dh and dh == LANES and qblk % 2 == 0
    blk = qblk // 2
    nq = t // qblk
    eye = np.eye(dh, dtype=np.float32)
    place = np.zeros((BIAS_PIECES, LANES, dh), np.float32)
    for piece in range(BIAS_PIECES):
        for hh in range(nh):
            place[piece, FF_OFF + hh, hh * BIAS_PIECES + piece] = 1.0
    return pl.pallas_call(
        functools.partial(_flash_kernel, blk=blk, scale=dh ** -0.5, nh=nh),
        grid=(b, nh, nq),
        in_specs=[pl.BlockSpec((qblk, dh), lambda i, h, j: (i * nq + j, q_blk0 + h)),
                  pl.BlockSpec((t, dh), lambda i, h, j: (i, k_blk0 + h)),
                  pl.BlockSpec((t, dh), lambda i, h, j: (i, v_blk0 + h)),
                  pl.BlockSpec((1, t, LANES), lambda i, h, j: (i, 0, 0)),
                  pl.BlockSpec(eye.shape, lambda i, h, j: (0, 0)),
                  pl.BlockSpec(place.shape, lambda i, h, j: (0, 0, 0))],
        out_specs=pl.BlockSpec((1, qblk, dh), lambda i, h, j: (i, j, h)),
        out_shape=jax.ShapeDtypeStruct((b, t, nh * dh), BF16),
        scratch_shapes=[pltpu.VMEM((t, 2 * dh), BF16), pltpu.VMEM((t // blk, dh, blk), BF16),
                        pltpu.VMEM((blk, qblk), F32), pltpu.VMEM((blk, qblk), F32)],
        compiler_params=_cparams(("parallel", "parallel", "arbitrary")),
        name="flash",
    )(z, z, z, ctok, jnp.asarray(eye, BF16), jnp.asarray(place, BF16))


def _local_cumsum(x):
    row = lax.broadcasted_iota(jnp.int32, x.shape, 0)
    for sh in (1, 2, 4):
        x = x + jnp.where(row >= sh, pltpu.roll(x, sh, 0), 0.0)
    return x


def _scan_chunk(q, k, la, vs, masks, states, ones_ws, sel, fast):
    c = q.shape[0]
    n = c // SUB
    assert n > 1
    nh = len(vs)
    sub_iota = lax.broadcasted_iota(jnp.int32, (SUB, LANES), 0)
    zero_blk = jnp.zeros((SUB, LANES), F32)

    r = [jnp.zeros((1, LANES), F32)]
    qt, kh, kt, p_rows = [], [], [], []
    for i in range(n):
        sl = slice(i * SUB, (i + 1) * SUB)
        qi, ki = q[sl], k[sl]
        li = _local_cumsum(la[sl])
        tot = li[SUB - 1:SUB]
        r.append(r[i] + tot)
        qt.append(qi * jnp.exp(li))
        if fast:
            kt.append(ki * jnp.exp(-li))
            kh.append(kt[i] * jnp.exp(tot))
        else:
            kh.append(ki * jnp.exp(tot - li))
            for t in range(SUB):
                d = jnp.where(sub_iota <= t, li[t:t + 1] - li, NEG)
                p_rows.append(jnp.exp(d) * (qi[t:t + 1] * ki))

    qbar = jnp.concatenate([qt[i] * jnp.exp(r[i]) for i in range(n)], axis=0)
    r_ends = jnp.concatenate(r[1:], axis=0)

    def rhs(i):
        g = jnp.exp(jnp.minimum(r[i] - r_ends, 0.0))
        blocks = [kh[j] * g[j:j + 1] if j + 1 < i else kh[j] for j in range(min(i, n))]
        if fast and i < n:
            blocks.append(kt[i])
        blocks += [zero_blk] * (n - len(blocks))
        return jnp.concatenate(blocks, axis=0)

    def mask(x, h):
        return x if masks[h] is None else x * masks[h]

    outs = []
    for h in range(nh):
        o = _dot_t(mask(qbar, h).astype(BF16), states[h].astype(BF16))
        if not fast:
            p_all = jnp.concatenate(p_rows, axis=0).astype(BF16)
            rr = jnp.dot(p_all, ones_ws[h], preferred_element_type=F32)
            vrep = jnp.concatenate([vs[h][i * SUB:(i + 1) * SUB] for i in range(n) for _ in range(SUB)], axis=0)
            o = o + jnp.dot(sel, (rr * vrep).astype(BF16), preferred_element_type=F32)
        outs.append(o)

    first = 0 if fast else 1
    a_rows = [[zero_blk[:, :c]] * first for _ in range(nh)]
    for i in range(first, n):
        lhs = jnp.concatenate([mask(qt[i], h) for h in range(nh)], axis=0).astype(BF16)
        a_i = _dot_t(lhs, rhs(i).astype(BF16))
        for h in range(nh):
            a_rows[h].append(a_i[h * SUB:(h + 1) * SUB])
    if fast:
        causal = lax.broadcasted_iota(jnp.int32, (c, c), 1) <= lax.broadcasted_iota(jnp.int32, (c, c), 0)
    for h in range(nh):
        a = jnp.concatenate(a_rows[h], axis=0)
        if fast:
            a = jnp.where(causal, a, 0.0)
        outs[h] = outs[h] + jnp.dot(a.astype(BF16), vs[h].astype(BF16), preferred_element_type=F32)

    k_end = rhs(n)
    new_states = []
    for h in range(nh):
        upd = lax.dot_general(vs[h].astype(BF16), mask(k_end, h).astype(BF16),
                              (((0,), (0,)), ((), ())), preferred_element_type=F32)
        new_states.append(states[h] * jnp.exp(r[n]) + upd)
    return outs, new_states


def _scan_kernel(*refs, mode, chunk, nh):
    if mode == "gla":
        (zq_ref, zk_ref, zv_ref, zs_ref, zr_ref, wg2_ref, bg_ref, ng_ref, ones_ref, sel_ref,
         o_ref, sout_ref, zg_s, st_s) = refs
    else:
        (zq_ref, zf_ref, zv_ref, zr_ref, llb_ref, l1m_ref, oml_ref, ng_ref, ones_ref, sel_ref,
         o_ref, sout_ref, st_s) = refs
    ti = pl.program_id(2)
    tblk = zq_ref.shape[0]
    nu = zq_ref.shape[1] // LANES

    @pl.when(ti == 0)
    def _():
        st_s[...] = jnp.zeros_like(st_s)

    if mode == "gla":
        dk = LANES // nh
        zg_s[...] = jnp.dot(zs_ref[...].astype(BF16), wg2_ref[...], preferred_element_type=F32) + bg_ref[...]
        la_low = _log_sigmoid(jnp.min(zg_s[...], axis=0, keepdims=True)) * (1.0 / GLA_GATE_TEMP)
        lane = lax.broadcasted_iota(jnp.int32, (1, LANES), 1)
        masks = [((lane >= h * dk) & (lane < (h + 1) * dk)).astype(F32) for h in range(nh)]
    else:
        hf_min = jnp.min(zf_ref[...], axis=0, keepdims=True)
        la_low = jnp.maximum(llb_ref[...], l1m_ref[...] + _log_sigmoid(hf_min))
        masks = [None]

    def features(sl, u):
        ul = slice(u * LANES, (u + 1) * LANES)
        if mode == "gla":
            return (zq_ref[sl, ul] * dk ** -0.5, zk_ref[sl, ul],
                    _log_sigmoid(zg_s[sl, ul]) * (1.0 / GLA_GATE_TEMP))
        hf = zf_ref[sl, ul]
        e = jnp.exp(-jnp.abs(hf))
        inv = 1.0 / (1.0 + e)
        sig_neg = jnp.where(hf > 0, e * inv, inv)
        lsig = jnp.minimum(hf, 0.0) - jnp.log(1.0 + e)
        a = llb_ref[:, ul]
        bb = l1m_ref[:, ul] + lsig
        la = jnp.maximum(a, bb) + jnp.log(1.0 + jnp.exp(-jnp.abs(a - bb)))
        return _silu(zq_ref[sl, ul]), oml_ref[:, ul] * sig_neg, la

    sel = sel_ref[...]
    ones_ws = [ones_ref[h] for h in range(nh)]

    def body(ci, _, fast):
        off = pl.multiple_of(ci * chunk, chunk)
        sl = pl.ds(off, chunk)
        for u in range(nu):
            hl = [slice((u * nh + h) * LANES, (u * nh + h + 1) * LANES) for h in range(nh)]
            vs = [zv_ref[sl, hl[h]] for h in range(nh)]
            states = [st_s[u * nh + h] for h in range(nh)]
            q, k, la = features(sl, u)
            outs, new_states = _scan_chunk(q, k, la, vs, masks, states, ones_ws, sel, fast)
            for h in range(nh):
                st_s[u * nh + h] = new_states[h]
                o = _rms(outs[h], ng_ref[...]) * _silu(zr_ref[sl, hl[h]])
                o_ref[0, sl, hl[h]] = o.astype(o_ref.dtype)
        return 0

    safe = jnp.min(la_low) * SUB >= -FAST_BLOCK_DECAY

    @pl.when(safe)
    def _():
        lax.fori_loop(0, tblk // chunk, functools.partial(body, fast=True), 0)

    @pl.when(jnp.logical_not(safe))
    def _():
        lax.fori_loop(0, tblk // chunk, functools.partial(body, fast=False), 0)

    @pl.when(ti == pl.num_programs(2) - 1)
    def _():
        dk_out = LANES // nh
        for u in range(nu):
            for h in range(nh):
                sout_ref[0, u * nh + h] = st_s[u * nh + h].T[h * dk_out:(h + 1) * dk_out, :]


def _scan_consts(chunk, nh):
    n = chunk // SUB
    sel = np.zeros((chunk, n * SUB * SUB), np.float32)
    for i in range(n):
        for t in range(SUB):
            sel[i * SUB + t, i * 64 + t * SUB:i * 64 + (t + 1) * SUB] = 1.0
    ones = np.zeros((nh, LANES, LANES), np.float32)
    dk = LANES // nh
    for h in range(nh):
        ones[h, h * dk:(h + 1) * dk, :] = 1.0
    return jnp.asarray(ones, BF16), jnp.asarray(sel, BF16)


def _lane_block(col, name, width):
    assert (col[name] * LANES) % width == 0
    return col[name] * LANES // width


def _scan_gla(z, b, t, units, nu, col, wg2p, bg, ng, tblk, chunk):
    nh = 2
    nt = t // tblk
    ones, sel = _scan_consts(chunk, nh)
    kw, vw = nu * LANES, nu * nh * LANES
    zspec = lambda name, w: pl.BlockSpec((tblk, w), lambda i, u, j: (i * nt + j, _lane_block(col, name, w) + u))
    in_specs = [
        zspec("gq", kw), zspec("gk", kw), zspec("gv", vw),
        pl.BlockSpec((tblk, LANES), lambda i, u, j: (i * nt + j, col["small"])),
        zspec("gr", vw),
        pl.BlockSpec((LANES, kw), lambda i, u, j: (0, u)),
        pl.BlockSpec((1, kw), lambda i, u, j: (0, u)),
        pl.BlockSpec((1, LANES), lambda i, u, j: (0, 0)),
        pl.BlockSpec(ones.shape, lambda i, u, j: (0, 0, 0)),
        pl.BlockSpec(sel.shape, lambda i, u, j: (0, 0)),
    ]
    dk = LANES // nh
    return pl.pallas_call(
        functools.partial(_scan_kernel, mode="gla", chunk=chunk, nh=nh),
        grid=(b, units // nu, nt),
        in_specs=in_specs,
        out_specs=[pl.BlockSpec((1, tblk, vw), lambda i, u, j: (i, j, u)),
                   pl.BlockSpec((1, nu * nh, dk, LANES), lambda i, u, j: (i, u, 0, 0))],
        out_shape=[jax.ShapeDtypeStruct((b, t, units * nh * LANES), BF16),
                   jax.ShapeDtypeStruct((b, units * nh, dk, LANES), F32)],
        scratch_shapes=[pltpu.VMEM((tblk, kw), F32), pltpu.VMEM((nu * nh, LANES, LANES), F32)],
        compiler_params=_cparams(("parallel", "parallel", "arbitrary")),
        name="scan_gla",
    )(z, z, z, z, z, wg2p, bg, ng, ones, sel)


def _scan_hgrn(z, b, t, units, nu, col, llb, l1m, oml, ng, tblk, chunk):
    nh = 1
    nt = t // tblk
    ones, sel = _scan_consts(chunk, nh)
    kw = nu * LANES
    zspec = lambda name: pl.BlockSpec((tblk, kw), lambda i, u, j: (i * nt + j, _lane_block(col, name, kw) + u))
    pspec = pl.BlockSpec((1, kw), lambda i, u, j: (0, u))
    in_specs = [zspec("hq"), zspec("hf"), zspec("hi"), zspec("hg"), pspec, pspec, pspec,
                pl.BlockSpec((1, LANES), lambda i, u, j: (0, 0)),
                pl.BlockSpec(ones.shape, lambda i, u, j: (0, 0, 0)),
                pl.BlockSpec(sel.shape, lambda i, u, j: (0, 0))]
    return pl.pallas_call(
        functools.partial(_scan_kernel, mode="hgrn", chunk=chunk, nh=nh),
        grid=(b, units // nu, nt),
        in_specs=in_specs,
        out_specs=[pl.BlockSpec((1, tblk, kw), lambda i, u, j: (i, j, u)),
                   pl.BlockSpec((1, nu, LANES, LANES), lambda i, u, j: (i, u, 0, 0))],
        out_shape=[jax.ShapeDtypeStruct((b, t, units * LANES), BF16),
                   jax.ShapeDtypeStruct((b, units, LANES, LANES), F32)],
        scratch_shapes=[pltpu.VMEM((nu, LANES, LANES), F32)],
        compiler_params=_cparams(("parallel", "parallel", "arbitrary")),
        name="scan_hgrn",
    )(z, z, z, z, llb, l1m, oml, ng, ones, sel)


def _rec_kernel(*refs, mode, nseq, ntok, dk):
    if mode == "gla":
        (qt_ref, kt_ref, st_ref, wg2t_ref, bgt_ref, zv_ref, zr_ref, ng_ref, s0_ref, o_ref, sout_ref, o_s) = refs
        qc = qt_ref[...] * dk ** -0.5
        kc = kt_ref[...]
        zg = jnp.dot(wg2t_ref[...], st_ref[...].astype(BF16), preferred_element_type=F32) + bgt_ref[...]
        ac = jnp.exp(_log_sigmoid(zg) * (1.0 / GLA_GATE_TEMP))
    else:
        (qt_ref, ft_ref, llb_ref, l1m_ref, oml_ref, zv_ref, zr_ref, ng_ref, s0_ref, o_ref, sout_ref, o_s) = refs
        hf = ft_ref[...]
        lsig = _log_sigmoid(hf)
        a = llb_ref[...]
        bb = l1m_ref[...] + lsig
        ac = jnp.exp(jnp.maximum(a, bb) + jnp.log(1.0 + jnp.exp(-jnp.abs(a - bb))))
        kc = oml_ref[...] * _sigmoid(-hf)
        qc = _silu(qt_ref[...])
    for sq in range(nseq):
        s = s0_ref[sq, 0]
        for t in range(ntok):
            j = sq * ntok + t
            vrow = zv_ref[j:j + 1, :]
            s = s * ac[:, j:j + 1] + kc[:, j:j + 1] * vrow
            o_s[j:j + 1, :] = jnp.sum(s * qc[:, j:j + 1], axis=0, keepdims=True)
        sout_ref[sq, 0] = s
    o = _rms(o_s[...], ng_ref[...]) * _silu(zr_ref[...])
    o_ref[...] = o.astype(o_ref.dtype)


def _rec(mode, zt, z, col, params, ng, s0, nb, ntok, heads, dk, nseq):
    m = nb * ntok
    rows = nseq * ntok
    ng_groups = nb // nseq
    if mode == "gla":
        wg2t, bgt = params
        per = LANES // dk
        tspec = lambda name: pl.BlockSpec((None, dk, rows), lambda h, g: (g, col[name] * per + h, 0))
        in_specs = [tspec("gq"), tspec("gk"),
                    pl.BlockSpec((None, LANES, rows), lambda h, g: (g, col["small"], 0)),
                    pl.BlockSpec((dk, LANES), lambda h, g: (h, 0)),
                    pl.BlockSpec((dk, 1), lambda h, g: (h, 0)),
                    pl.BlockSpec((rows, LANES), lambda h, g: (g, col["gv"] + h)),
                    pl.BlockSpec((rows, LANES), lambda h, g: (g, col["gr"] + h))]
        args = (zt, zt, zt, wg2t, bgt, z, z)
    else:
        llb, l1m, oml = params
        tspec = lambda name: pl.BlockSpec((None, dk, rows), lambda h, g: (g, col[name] + h, 0))
        pspec = pl.BlockSpec((dk, 1), lambda h, g: (h, 0))
        in_specs = [tspec("hq"), tspec("hf"), pspec, pspec, pspec,
                    pl.BlockSpec((rows, LANES), lambda h, g: (g, col["hi"] + h)),
                    pl.BlockSpec((rows, LANES), lambda h, g: (g, col["hg"] + h))]
        args = (zt, zt, llb, l1m, oml, z, z)
    in_specs += [pl.BlockSpec((1, LANES), lambda h, g: (0, 0)),
                 pl.BlockSpec((nseq, 1, dk, LANES), lambda h, g: (g, h, 0, 0))]
    return pl.pallas_call(
        functools.partial(_rec_kernel, mode=mode, nseq=nseq, ntok=ntok, dk=dk),
        grid=(heads, ng_groups),
        in_specs=in_specs,
        out_specs=[pl.BlockSpec((rows, LANES), lambda h, g: (g, h)),
                   pl.BlockSpec((nseq, 1, dk, LANES), lambda h, g: (g, h, 0, 0))],
        out_shape=[jax.ShapeDtypeStruct((m, heads * LANES), BF16),
                   jax.ShapeDtypeStruct(s0.shape, F32)],
        scratch_shapes=[pltpu.VMEM((rows, LANES), F32)],
        compiler_params=_cparams(("parallel", "parallel")),
        name="rec_" + mode,
    )(*args, ng, s0)


def _lfpool_kernel(lf_ref, o_ref, *, nh):
    x = lf_ref[...]
    w = x.shape[1]
    lane = lax.broadcasted_iota(jnp.int32, x.shape, 1)
    incl = x
    sh = nh
    while sh < w:
        incl = incl + jnp.where(lane < w - sh, pltpu.roll(incl, w - sh, 1), 0.0)
        sh *= 2
    tot = jnp.where(lane < nh, incl, 0.0)
    sh = nh
    while sh < w:
        tot = tot + pltpu.roll(tot, sh, 1)
        sh *= 2
    o_ref[:, :w] = incl - x
    o_ref[:, w:] = tot


def _lfpool(clf, nh, rows):
    depth, n_pool, w = clf.shape
    return pl.pallas_call(
        functools.partial(_lfpool_kernel, nh=nh),
        grid=(depth, n_pool // rows),
        in_specs=[pl.BlockSpec((None, rows, w), lambda l, i: (l, i, 0))],
        out_specs=pl.BlockSpec((None, rows, 2 * w), lambda l, i: (l, i, 0)),
        out_shape=jax.ShapeDtypeStruct((depth, n_pool, 2 * w), F32),
        compiler_params=_cparams(("parallel", "parallel")),
        name="lfpool",
    )(clf)


def _paged_kernel(pt_ref, q_ref, kn_ref, vn_ref, zs_ref, bf_ref, mask_ref, *rest, npg, nh, dh, ntok, scale):
    k_refs = rest[:npg]
    v_refs = rest[npg:2 * npg]
    rt_ref, o_ref, lf_ref = rest[2 * npg:2 * npg + 3]
    m_s, l_s, acc_s, car_s = rest[2 * npg + 3:]
    g = pl.program_id(1)

    @pl.when(g == 0)
    def _():
        m_s[...] = jnp.full(m_s.shape, NEG, F32)
        l_s[...] = jnp.zeros_like(l_s)
        acc_s[...] = jnp.zeros_like(acc_s)
        car_s[...] = jnp.zeros_like(car_s)

    q = q_ref[0] * scale
    q_all = jnp.concatenate([q[:, h * dh:(h + 1) * dh] for h in range(nh)], axis=0).astype(BF16)
    w = mask_ref.shape[1]
    carry = car_s[...]
    s_pages = []
    for i in range(npg):
        rt = rt_ref[0, i:i + 1, :]
        bias = rt[:, :w] + carry
        carry = carry + rt[:, w:]
        s_pages.append(_dot_t(q_all, k_refs[i][...].astype(BF16)) + (mask_ref[...] + bias))
    car_s[...] = carry
    s = jnp.concatenate(s_pages, axis=1)
    m = m_s[...]
    m_new = jnp.maximum(m, jnp.max(s, axis=-1, keepdims=True))
    alpha = jnp.exp(m - m_new)
    p = jnp.exp(s - m_new)
    l_s[...] = alpha * l_s[...] + jnp.sum(p, axis=-1, keepdims=True)
    pv = jnp.zeros(acc_s.shape, F32)
    for i in range(npg):
        pv = pv + jnp.dot(p[:, i * w:(i + 1) * w].astype(BF16), v_refs[i][...].astype(BF16),
                          preferred_element_type=F32)
    acc_s[...] = alpha * acc_s[...] + pv
    m_s[...] = m_new

    @pl.when(g == pl.num_programs(1) - 1)
    def _():
        lfn = _log_sigmoid(zs_ref[0] + bf_ref[...])
        lf_ref[0] = lfn
        row = lax.broadcasted_iota(jnp.int32, lfn.shape, 0)
        cn = _local_cumsum(jnp.where(row < ntok, lfn, 0.0))
        trow = lax.broadcasted_iota(jnp.int32, (SUB, 1), 0)
        for h in range(nh):
            hs = slice(h * dh, (h + 1) * dh)
            rs = slice(h * SUB, (h + 1) * SUB)
            m, l, acc = m_s[rs, :], l_s[rs, :], acc_s[rs, :]
            for sp in range(ntok):
                logit = jnp.sum(q[:, hs] * kn_ref[0, sp:sp + 1, hs], axis=-1, keepdims=True)
                logit = logit - cn[sp:sp + 1, FF_OFF + h:FF_OFF + h + 1]
                logit = jnp.where(trow >= sp, logit, NEG)
                m_new = jnp.maximum(m, logit)
                alpha = jnp.exp(m - m_new)
                p = jnp.exp(logit - m_new)
                l = alpha * l + p
                acc = alpha * acc + p * vn_ref[0, sp:sp + 1, hs]
                m = m_new
            o_ref[0, :, hs] = (acc / l).astype(o_ref.dtype)


def _paged(layer, page_table, q, kn, vn, zs, bf_row, cache_k, cache_v, rt_seq, nh, dh, ntok, npg):
    nb = q.shape[0]
    w = cache_k.shape[2]
    npages = page_table.shape[1]
    ngrp = npages // npg
    mask = np.full((nh * SUB, w), NEG, np.float32)
    for h in range(nh):
        mask[h * SUB:(h + 1) * SUB, h::nh] = 0.0

    def page_of(b, g, pt, i):
        return pt[b, npages - 1 - (g * npg + i)]

    seq3 = lambda b, g, pt: (b, 0, 0)
    in_specs = [pl.BlockSpec((1, SUB, nh * dh), seq3)] * 3 + [
        pl.BlockSpec((1, SUB, LANES), seq3),
        pl.BlockSpec((1, LANES), lambda b, g, pt: (0, 0)),
        pl.BlockSpec(mask.shape, lambda b, g, pt: (0, 0))]
    kv_specs = [pl.BlockSpec((None, None, w, dh), lambda b, g, pt, i=i: (layer, page_of(b, g, pt, i), 0, 0))
                for i in range(npg)]
    in_specs += kv_specs + kv_specs
    in_specs += [pl.BlockSpec((1, npg, 2 * w), lambda b, g, pt: (b, g, 0))]
    grid_spec = pltpu.PrefetchScalarGridSpec(
        num_scalar_prefetch=1, grid=(nb, ngrp), in_specs=in_specs,
        out_specs=[pl.BlockSpec((1, SUB, nh * dh), seq3), pl.BlockSpec((1, SUB, LANES), seq3)],
        scratch_shapes=[pltpu.VMEM((nh * SUB, 1), F32), pltpu.VMEM((nh * SUB, 1), F32),
                        pltpu.VMEM((nh * SUB, dh), F32), pltpu.VMEM((1, w), F32)])
    return pl.pallas_call(
        functools.partial(_paged_kernel, npg=npg, nh=nh, dh=dh, ntok=ntok, scale=dh ** -0.5),
        grid_spec=grid_spec,
        out_shape=[jax.ShapeDtypeStruct((nb, SUB, nh * dh), F32),
                   jax.ShapeDtypeStruct((nb, SUB, LANES), F32)],
        compiler_params=_cparams(("parallel", "arbitrary")),
        name="paged",
    )(page_table, q, kn, vn, zs, bf_row, jnp.asarray(mask), *([cache_k] * npg), *([cache_v] * npg), rt_seq)


def _merge_kernel(x_ref, oa_ref, ob_ref, oc_ref, g1_ref, wg_ref, wa_ref, wb_ref, wc_ref, wo_ref, y_ref):
    x = x_ref[...]
    d = x.shape[1]
    xn = _rms(x, g1_ref[...]).astype(BF16)
    merged = jnp.zeros(x.shape, F32)
    for i, (o_ref, w_ref) in enumerate(((oa_ref, wa_ref), (ob_ref, wb_ref), (oc_ref, wc_ref))):
        gate = _sigmoid(jnp.dot(xn, wg_ref[:, i * d:(i + 1) * d], preferred_element_type=F32))
        merged = merged + gate * jnp.dot(o_ref[...], w_ref[...], preferred_element_type=F32)
    y_ref[...] = x + jnp.dot(merged.astype(BF16), wo_ref[...], preferred_element_type=F32)


def _merge(x, oa, ob, oc, g1, wg_all, layer, wa, wb, wc, wo, tm):
    m, d = x.shape
    const = lambda a: pl.BlockSpec(a.shape, lambda i: (0,) * a.ndim)
    rows = lambda a: pl.BlockSpec((tm, a.shape[1]), lambda i: (i, 0))
    return pl.pallas_call(
        _merge_kernel,
        grid=(m // tm,),
        in_specs=[rows(x), rows(oa), rows(ob), rows(oc), const(g1),
                  pl.BlockSpec((None,) + wg_all.shape[1:], lambda i: (layer, 0, 0)),
                  const(wa), const(wb), const(wc), const(wo)],
        out_specs=pl.BlockSpec((tm, d), lambda i: (i, 0)),
        out_shape=jax.ShapeDtypeStruct((m, d), F32),
        compiler_params=_cparams(("parallel",)),
        name="merge",
    )(x, oa, ob, oc, g1, wg_all, wa, wb, wc, wo)


def _ffn_kernel(x_ref, g2_ref, wg_ref, wu_ref, wd_ref, gf_ref, y_ref, *, final):
    x = x_ref[...]
    h = _rms(x, g2_ref[...]).astype(BF16)
    acc = x
    hidden = wg_ref.shape[1]
    for c0 in range(0, hidden, FFN_HIDDEN_CHUNK):
        cs = slice(c0, min(c0 + FFN_HIDDEN_CHUNK, hidden))
        a = jnp.dot(h, wg_ref[:, cs], preferred_element_type=F32)
        u = jnp.dot(h, wu_ref[:, cs], preferred_element_type=F32)
        acc = acc + jnp.dot((_silu(a) * u).astype(BF16), wd_ref[cs, :], preferred_element_type=F32)
    y_ref[...] = _rms(acc, gf_ref[...]) if final else acc


def _ffn(x, g2, wg, wu, wd, gf, tm, final):
    m, d = x.shape
    const = lambda a: pl.BlockSpec(a.shape, lambda i: (0,) * a.ndim)
    return pl.pallas_call(
        functools.partial(_ffn_kernel, final=final),
        grid=(m // tm,),
        in_specs=[pl.BlockSpec((tm, d), lambda i: (i, 0)), const(g2), const(wg), const(wu), const(wd), const(gf)],
        out_specs=pl.BlockSpec((tm, d), lambda i: (i, 0)),
        out_shape=jax.ShapeDtypeStruct((m, d), F32),
        compiler_params=_cparams(("parallel",)),
        name="ffn",
    )(x, g2, wg, wu, wd, gf)


def _pick(n, pref):
    for c in pref:
        if n % c == 0:
            return c
    return n


def kernel(x_prompt, x_sample, state_gla, cache_fox_k, cache_fox_v, cache_fox_logf, state_hgrn, page_table,
           norm1_g, w_in, gla_wg2, gla_bg, gla_norm_g, fox_bf, hg_lb_logits, hg_norm_g,
           w_branch_a, w_branch_b, w_branch_c, w_out, norm2_g, w_ffn_gate, w_ffn_up, w_ffn_down,
           final_norm_g):
    depth, d_model, _ = w_in.shape
    bp, tp, _ = x_prompt.shape
    nb, ntok, _ = x_sample.shape
    _, _, gh, gdk, gdv = state_gla.shape
    _, _, hh, hdk, hdv = state_hgrn.shape
    _, n_pool, page, fh, fdh = cache_fox_k.shape
    hidden = w_ffn_gate.shape[2]
    assert gdv == LANES and hdk == LANES and hdv == LANES and fdh == LANES and 2 * gdk == LANES
    assert ntok <= SUB and gh % 2 == 0

    gq_w, gv_w, f_w, h_w = gh * gdk, gh * gdv, fh * fdh, hh * hdk
    names = ["gq", "gk", "gv", "glr", "gr", "fq", "fk", "fv", "ff", "hq", "hf", "hi", "hg", "ga", "gb", "gc"]
    widths = [gq_w, gq_w, gv_w, GLA_GATE_RANK, gv_w, f_w, f_w, f_w, fh, h_w, h_w, h_w, h_w,
              d_model, d_model, d_model]
    starts = dict(zip(names, np.concatenate([[0], np.cumsum(widths)[:-1]]).tolist()))
    wid = dict(zip(names, widths))
    order = ["gq", "gk", "gv", "gr", "fq", "fk", "fv", "hq", "hf", "hi", "hg"]
    col, off = {}, 0
    for nm in order:
        col[nm] = off // LANES
        off += wid[nm]
    col["small"] = off // LANES
    n_used = off + LANES
    tn = min(14, n_used // LANES) * LANES
    n_pad = -(-n_used // tn) * tn

    segs = [(starts[nm], col[nm] * LANES, wid[nm]) for nm in order]
    segs += [(starts["ff"], col["small"] * LANES + FF_OFF, fh),
             (starts["glr"], col["small"] * LANES + GLR_OFF, GLA_GATE_RANK)]
    w_pad_all, w_gates_all = _wprep(w_in, segs, starts["ga"], 3 * d_model, n_pad, _pick(d_model, (128, 64, 32, 16, 8)))

    lb_cum = jnp.cumsum(jax.nn.softmax(hg_lb_logits.astype(F32), axis=0), axis=0)
    hg_lb = lb_cum - lb_cum[:1]
    log_lb, log1m_lb, one_m_lb = jnp.log(hg_lb), jnp.log1p(-hg_lb), 1.0 - hg_lb


    xp = x_prompt.reshape(bp * tp, d_model)
    xs = jnp.pad(x_sample, ((0, 0), (0, SUB - ntok), (0, 0))).reshape(nb * SUB, d_model)
    ms = nb * SUB

    tm_p = _pick(bp * tp, (1024, 512, 256, 128))
    tt = _pick(tp, (1024, 512, 256, 128))
    fblk = _pick(tp, (1024, 512, 256))
    tblk = _pick(tp, (512, 256, 128))
    chunk = min(128, tblk)
    tm_e = _pick(bp * tp, (512, 256, 128))
    nseq = _pick(nb, (8, 4, 2, 1))
    npg = _pick(page_table.shape[1], (16, 8, 4, 2, 1))

    ck = cache_fox_k.reshape(depth, n_pool, page * fh, fdh)
    cv = cache_fox_v.reshape(depth, n_pool, page * fh, fdh)
    rtot = _lfpool(cache_fox_logf.astype(F32).reshape(depth, n_pool, page * fh), fh, _pick(n_pool, (256, 128, 64, 32, 16, 8)))

    outs = {k: [] for k in ("gla_p", "gla_s", "k_p", "v_p", "lf_p", "k_s", "v_s", "lf_s", "hg_p", "hg_s")}

    for l in range(depth):
        g1 = norm1_g[l].reshape(1, d_model)
        g2 = norm2_g[l].reshape(1, d_model)
        bf_row = jnp.zeros((1, LANES), F32).at[0, FF_OFF:FF_OFF + fh].set(fox_bf[l])
        wg2p = jnp.zeros((LANES, gq_w), F32).at[GLR_OFF:GLR_OFF + GLA_GATE_RANK].set(gla_wg2[l])
        wg2_b = wg2p.astype(BF16)
        bg_row = gla_bg[l].reshape(1, gq_w)
        gng = gla_norm_g[l].reshape(1, LANES)
        hng = hg_norm_g[l].reshape(1, LANES)
        llb_u, l1m_u, oml_u = (a[l].reshape(1, h_w) for a in (log_lb, log1m_lb, one_m_lb))
        wa, wb, wc, wo = (w[l].astype(BF16) for w in (w_branch_a, w_branch_b, w_branch_c, w_out))
        wfg, wfu, wfd = (w[l].astype(BF16) for w in (w_ffn_gate, w_ffn_up, w_ffn_down))
        gf = final_norm_g.reshape(1, d_model)
        final = l == depth - 1

        z, fk, fv = _proj(xp, g1, w_pad_all, l, tm_p, tn, (col["fk"] * LANES, col["fv"] * LANES, fh, fdh))
        outs["k_p"].append(fk.reshape(bp, tp, fh, fdh))
        outs["v_p"].append(fv.reshape(bp, tp, fh, fdh))
        lf, ctok = _fox_prep(z, bf_row, bp, tp, col["small"], fh, tt)
        outs["lf_p"].append(lf)
        ob = _flash(z, ctok, bp, tp, fh, fdh, col["fq"], col["fk"], col["fv"], fblk)
        oa, sg = _scan_gla(z, bp, tp, gh // 2, GLA_UNITS_PER_STEP, col, wg2_b, bg_row, gng, tblk, chunk)
        outs["gla_p"].append(sg)
        oc, sh = _scan_hgrn(z, bp, tp, hh, HGRN_UNITS_PER_STEP, col, llb_u, l1m_u, oml_u, hng, tblk, chunk)
        outs["hg_p"].append(sh)
        x1 = _merge(xp, oa.reshape(bp * tp, -1), ob.reshape(bp * tp, -1), oc.reshape(bp * tp, -1),
                    g1, w_gates_all, l, wa, wb, wc, wo, tm_e)
        xp = _ffn(x1, g2, wfg, wfu, wfd, gf, tm_e, final)

        zs, = _proj(xs, g1, w_pad_all, l, ms, tn)
        zs3 = zs.reshape(nb, SUB, n_pad)
        take = lambda nm, w: zs3[:, :, col[nm] * LANES:col[nm] * LANES + w]
        fks, fvs = take("fk", f_w), take("fv", f_w)
        outs["k_s"].append(fks[:, :ntok].reshape(nb, ntok, fh, fdh))
        outs["v_s"].append(fvs[:, :ntok].reshape(nb, ntok, fh, fdh))
        small_s = take("small", LANES)
        rt_seq = jnp.take(rtot[l], page_table[:, ::-1], axis=0)
        obs, lfs = _paged(l, page_table, take("fq", f_w), fks, fvs, small_s, bf_row, ck, cv, rt_seq,
                          fh, fdh, ntok, npg)
        outs["lf_s"].append(lfs[:, :ntok, FF_OFF:FF_OFF + fh])
        obs = obs.astype(BF16)
        zc = zs3[:, :ntok].reshape(nb * ntok, n_pad)
        zt = zc.reshape(nb // nseq, nseq * ntok, n_pad).transpose(0, 2, 1)
        wg2t = wg2p.T.astype(BF16)
        oas, sgs = _rec("gla", zt, zc, col, (wg2t, gla_bg[l].reshape(-1, 1)), gng, state_gla[l],
                        nb, ntok, gh, gdk, nseq)
        outs["gla_s"].append(sgs)
        ocs, shs = _rec("hgrn", zt, zc, col, tuple(a[l].reshape(-1, 1) for a in (log_lb, log1m_lb, one_m_lb)),
                        hng, state_hgrn[l], nb, ntok, hh, hdk, nseq)
        outs["hg_s"].append(shs)
        pad_tok = lambda o: jnp.pad(o.reshape(nb, ntok, -1), ((0, 0), (0, SUB - ntok), (0, 0))).reshape(ms, -1)
        x1s = _merge(xs, pad_tok(oas), obs.reshape(ms, -1), pad_tok(ocs), g1, w_gates_all, l, wa, wb, wc, wo, ms)
        xs = _ffn(x1s, g2, wfg, wfu, wfd, gf, ms, final)

    st = lambda k: jnp.stack(outs[k])
    y_p = xp.reshape(bp, tp, d_model)
    y_s = xs.reshape(nb, SUB, d_model)[:, :ntok]
    return (y_p, y_s, st("gla_p"), st("gla_s"), st("k_p"), st("v_p"), st("lf_p"),
            st("k_s"), st("v_s"), st("lf_s"), st("hg_p"), st("hg_s"))
```

```python
import functools

import numpy as np
import jax
import jax.numpy as jnp
from jax import lax
from jax.experimental import pallas as pl
from jax.experimental.pallas import tpu as pltpu

F32 = jnp.float32
BF16 = jnp.bfloat16
EPS = 1e-6
NEG = -1e30
LANES = 128
SUB = 8
HI = lax.Precision.HIGHEST
VMEM_LIMIT = 56 * 1024 * 1024

FAST_BLOCK_DECAY = 60.0
GLA_UNITS_PER_STEP = 2
HGRN_UNITS_PER_STEP = 4
FFN_HIDDEN_CHUNK = 256
GLA_GATE_TEMP = 16.0
GLA_GATE_RANK = 16
FF_OFF = 0
GLR_OFF = 16


def _cparams(sem):
    return pltpu.CompilerParams(dimension_semantics=sem, vmem_limit_bytes=VMEM_LIMIT)


def _rms(x, g):
    return x * lax.rsqrt(jnp.mean(x * x, axis=-1, keepdims=True) + EPS) * g


def _sigmoid(x):
    return 1.0 / (1.0 + jnp.exp(-x))


def _log_sigmoid(x):
    return jnp.minimum(x, 0.0) - jnp.log(1.0 + jnp.exp(-jnp.abs(x)))


def _silu(x):
    return x * _sigmoid(x)


def _dot_t(a, b):
    return lax.dot_general(a, b, (((1,), (1,)), ((), ())), preferred_element_type=F32)


def _wprep_kernel(tbl_ref, w_ref, o_ref):
    j = pl.program_id(0)
    col = lax.broadcasted_iota(jnp.int32, (w_ref.shape[0], w_ref.shape[2]), 0)
    keep = (col >= tbl_ref[1, j]) & (col < tbl_ref[2, j])
    for l in range(w_ref.shape[1]):
        o_ref[l] = jnp.where(keep, w_ref[:, l, :], 0.0).T.astype(BF16)


def _wprep(w_cols, blocks):
    n_in, depth, d = w_cols.shape
    tbl = jnp.asarray(np.asarray(blocks, np.int32).T)
    assert all(0 <= b[0] and b[0] + LANES <= n_in for b in blocks)
    grid_spec = pltpu.PrefetchScalarGridSpec(
        num_scalar_prefetch=1, grid=(len(blocks),),
        in_specs=[pl.BlockSpec((pl.Element(LANES), pl.Element(depth), pl.Element(d)),
                               lambda j, tbl: (tbl[0, j], 0, 0))],
        out_specs=pl.BlockSpec((depth, d, LANES), lambda j, tbl: (0, 0, j)))
    return pl.pallas_call(
        _wprep_kernel,
        grid_spec=grid_spec,
        out_shape=jax.ShapeDtypeStruct((depth, d, LANES * len(blocks)), BF16),
        compiler_params=_cparams(("parallel",)),
        name="wprep",
    )(tbl, w_cols)


def _proj_kernel(x_ref, g_ref, w_ref, z_ref, *rest, kv):
    xn_ref = rest[-1]

    @pl.when(pl.program_id(1) == 0)
    def _():
        xn_ref[...] = _rms(x_ref[...], g_ref[...]).astype(BF16)

    zt = jnp.dot(xn_ref[...], w_ref[...], preferred_element_type=F32)
    z_ref[...] = zt
    if kv is not None:
        fk_ref, fv_ref = rest[:2]
        jkv, koff, voff, nh, dh = kv
        tm = zt.shape[0]

        @pl.when(pl.program_id(1) == jkv)
        def _():
            for h in range(nh):
                fk_ref[pl.ds(h, tm, stride=nh), :] = zt[:, koff + h * dh:koff + (h + 1) * dh]
                fv_ref[pl.ds(h, tm, stride=nh), :] = zt[:, voff + h * dh:voff + (h + 1) * dh]


def _proj(x, g, w_all, layer, tm, tn, kv_cols=None):
    m, d = x.shape
    n = w_all.shape[2]
    out_specs = [pl.BlockSpec((tm, tn), lambda i, j: (i, j))]
    out_shape = [jax.ShapeDtypeStruct((m, n), F32)]
    kv = None
    if kv_cols is not None:
        kc, vc, nh, dh = kv_cols
        assert kc // tn == (vc + nh * dh - 1) // tn
        kv = (kc // tn, kc % tn, vc % tn, nh, dh)
        out_specs += [pl.BlockSpec((tm * nh, dh), lambda i, j: (i, 0))] * 2
        out_shape += [jax.ShapeDtypeStruct((m * nh, dh), F32)] * 2
    return pl.pallas_call(
        functools.partial(_proj_kernel, kv=kv),
        grid=(m // tm, n // tn),
        in_specs=[pl.BlockSpec((tm, d), lambda i, j: (i, 0)),
                  pl.BlockSpec((1, d), lambda i, j: (0, 0)),
                  pl.BlockSpec((None, d, tn), lambda i, j: (layer, 0, j))],
        out_specs=out_specs,
        out_shape=out_shape,
        scratch_shapes=[pltpu.VMEM((tm, d), BF16)],
        compiler_params=_cparams(("parallel", "arbitrary")),
        name="proj",
    )(x, g, w_all)


def _fox_prep_kernel(zs_ref, bf_ref, tril_ref, lf_ref, c_ref, carry_ref, *, nh):
    @pl.when(pl.program_id(1) == 0)
    def _():
        carry_ref[...] = jnp.zeros_like(carry_ref)

    lf = _log_sigmoid(zs_ref[...] + bf_ref[...])
    lf_ref[0] = lf[:, FF_OFF:FF_OFF + nh]
    grp = tril_ref.shape[0]
    carry = carry_ref[...][:1]
    for g0 in range(0, lf.shape[0], grp):
        c = jnp.dot(tril_ref[...], lf[g0:g0 + grp], precision=HI, preferred_element_type=F32) + carry
        c_ref[0, g0:g0 + grp, :] = c
        carry = c[-1:]
    carry_ref[...] = jnp.broadcast_to(carry, carry_ref.shape)


def _fox_prep(z, bf_row, b, t, small_blk, nh, tt):
    grp = min(tt, LANES)
    tril = np.tril(np.ones((grp, grp), np.float32))
    nt = t // tt
    return pl.pallas_call(
        functools.partial(_fox_prep_kernel, nh=nh),
        grid=(b, nt),
        in_specs=[pl.BlockSpec((tt, LANES), lambda i, j: (i * nt + j, small_blk)),
                  pl.BlockSpec((1, LANES), lambda i, j: (0, 0)),
                  pl.BlockSpec((grp, grp), lambda i, j: (0, 0))],
        out_specs=[pl.BlockSpec((1, tt, nh), lambda i, j: (i, j, 0)),
                   pl.BlockSpec((1, tt, LANES), lambda i, j: (i, j, 0))],
        out_shape=[jax.ShapeDtypeStruct((b, t, nh), F32),
                   jax.ShapeDtypeStruct((b, t, LANES), F32)],
        scratch_shapes=[pltpu.VMEM((SUB, LANES), F32)],
        compiler_params=_cparams(("parallel", "arbitrary")),
        name="fox_prep",
    )(z, bf_row, jnp.asarray(tril))


BIAS_PIECES = 3
LOG2E = 1.4426950408889634


def _flash_kernel(q_ref, k_ref, v_ref, c_ref, eye_ref, place_ref, o_ref, kb_ref, vt_ref, sa_ref, sb_ref, *,
                  blk, scale, nh):
    h = pl.program_id(1)
    qi = pl.program_id(2)
    dh = q_ref.shape[1]

    @pl.when(qi == 0)
    def _():
        def prep(j, _):
            rows = pl.ds(pl.multiple_of(j * blk, blk), blk)
            kb_ref[rows, :dh] = k_ref[rows, :].astype(BF16)
            vt_ref[j] = _dot_t(eye_ref[...], v_ref[rows, :].astype(BF16)).astype(BF16)
            rem = c_ref[0, rows, :] * (-LOG2E)
            extra = jnp.zeros(rem.shape, F32)
            for piece in range(BIAS_PIECES):
                part = rem.astype(BF16)
                extra = extra + jnp.dot(part, place_ref[piece], preferred_element_type=F32)
                rem = rem - part.astype(F32)
            kb_ref[rows, dh:] = extra.astype(BF16)
            return 0

        lax.fori_loop(0, k_ref.shape[0] // blk, prep, 0)

    qblk = q_ref.shape[0]
    lane_q = lax.broadcasted_iota(jnp.int32, (qblk, dh), 1)
    own = (lane_q >= h * BIAS_PIECES) & (lane_q < (h + 1) * BIAS_PIECES)
    q = jnp.concatenate([(q_ref[...] * (scale * LOG2E)).astype(BF16),
                         jnp.where(own, 1.0, 0.0).astype(BF16)], axis=1)

    def scores(j):
        return _dot_t(kb_ref[pl.ds(pl.multiple_of(j * blk, blk), blk), :], q)

    def absorb(s_ref, j, carry, key_off):
        m, l, acc = carry
        s = s_ref[...]
        if key_off is not None:
            key = lax.broadcasted_iota(jnp.int32, (blk, qblk), 0) + key_off
            qry = lax.broadcasted_iota(jnp.int32, (blk, qblk), 1)
            s = jnp.where(key <= qry, s, NEG)
        m_new = jnp.maximum(m, jnp.max(s, axis=0, keepdims=True))
        alpha = jnp.exp2(m - m_new)
        p = jnp.exp2(s - m_new)
        l = alpha * l + jnp.sum(p, axis=0, keepdims=True)
        acc = alpha * acc + jnp.dot(vt_ref[j], p.astype(BF16), preferred_element_type=F32)
        return m_new, l, acc

    sa_ref[...] = scores(0)

    def pair(p, carry):
        j = 2 * p
        sb_ref[...] = scores(j + 1)
        carry = absorb(sa_ref, j, carry, None)
        sa_ref[...] = scores(j + 2)
        return absorb(sb_ref, j + 1, carry, None)

    init = (jnp.full((1, qblk), NEG, F32), jnp.zeros((1, qblk), F32), jnp.zeros((dh, qblk), F32))
    carry = lax.fori_loop(0, qi, pair, init)
    sb_ref[...] = scores(2 * qi + 1)
    carry = absorb(sa_ref, 2 * qi, carry, 0)
    m, l, acc = absorb(sb_ref, 2 * qi + 1, carry, blk)
    o_ref[0] = (acc / l).T.astype(o_ref.dtype)


def _flash(z, ctok, b, t, nh, dh, q_blk0, k_blk0, v_blk0, qblk):
    assert FF_OFF + nh <= LANES and nh * BIAS_PIECES <= dh and dh == LANES and qblk % 2 == 0
    blk = qblk // 2
    nq = t // qblk
    eye = np.eye(dh, dtype=np.float32)
    place = np.zeros((BIAS_PIECES, LANES, dh), np.float32)
    for piece in range(BIAS_PIECES):
        for hh in range(nh):
            place[piece, FF_OFF + hh, hh * BIAS_PIECES + piece] = 1.0
    return pl.pallas_call(
        functools.partial(_flash_kernel, blk=blk, scale=dh ** -0.5, nh=nh),
        grid=(b, nh, nq),
        in_specs=[pl.BlockSpec((qblk, dh), lambda i, h, j: (i * nq + j, q_blk0 + h)),
                  pl.BlockSpec((t, dh), lambda i, h, j: (i, k_blk0 + h)),
                  pl.BlockSpec((t, dh), lambda i, h, j: (i, v_blk0 + h)),
                  pl.BlockSpec((1, t, LANES), lambda i, h, j: (i, 0, 0)),
                  pl.BlockSpec(eye.shape, lambda i, h, j: (0, 0)),
                  pl.BlockSpec(place.shape, lambda i, h, j: (0, 0, 0))],
        out_specs=pl.BlockSpec((1, qblk, dh), lambda i, h, j: (i, j, h)),
        out_shape=jax.ShapeDtypeStruct((b, t, nh * dh), BF16),
        scratch_shapes=[pltpu.VMEM((t, 2 * dh), BF16), pltpu.VMEM((t // blk, dh, blk), BF16),
                        pltpu.VMEM((blk, qblk), F32), pltpu.VMEM((blk, qblk), F32)],
        compiler_params=_cparams(("parallel", "parallel", "arbitrary")),
        name="flash",
    )(z, z, z, ctok, jnp.asarray(eye, BF16), jnp.asarray(place, BF16))


def _local_cumsum(x):
    row = lax.broadcasted_iota(jnp.int32, x.shape, 0)
    for sh in (1, 2, 4):
        x = x + jnp.where(row >= sh, pltpu.roll(x, sh, 0), 0.0)
    return x


def _scan_chunk(q, k, la, vs, masks, states, ones_ws, sel, fast):
    c = q.shape[0]
    n = c // SUB
    assert n > 1
    nh = len(vs)
    sub_iota = lax.broadcasted_iota(jnp.int32, (SUB, LANES), 0)
    zero_blk = jnp.zeros((SUB, LANES), F32)

    r = [jnp.zeros((1, LANES), F32)]
    qt, kh, kt, p_rows = [], [], [], []
    for i in range(n):
        sl = slice(i * SUB, (i + 1) * SUB)
        qi, ki = q[sl], k[sl]
        li = _local_cumsum(la[sl])
        tot = li[SUB - 1:SUB]
        r.append(r[i] + tot)
        qt.append(qi * jnp.exp(li))
        if fast:
            kt.append(ki * jnp.exp(-li))
            kh.append(kt[i] * jnp.exp(tot))
        else:
            kh.append(ki * jnp.exp(tot - li))
            for t in range(SUB):
                d = jnp.where(sub_iota <= t, li[t:t + 1] - li, NEG)
                p_rows.append(jnp.exp(d) * (qi[t:t + 1] * ki))

    qbar = jnp.concatenate([qt[i] * jnp.exp(r[i]) for i in range(n)], axis=0)
    r_ends = jnp.concatenate(r[1:], axis=0)

    def rhs(i):
        g = jnp.exp(jnp.minimum(r[i] - r_ends, 0.0))
        blocks = [kh[j] * g[j:j + 1] if j + 1 < i else kh[j] for j in range(min(i, n))]
        if fast and i < n:
            blocks.append(kt[i])
        blocks += [zero_blk] * (n - len(blocks))
        return jnp.concatenate(blocks, axis=0)

    def mask(x, h):
        return x if masks[h] is None else x * masks[h]

    outs = []
    for h in range(nh):
        o = _dot_t(mask(qbar, h).astype(BF16), states[h].astype(BF16))
        if not fast:
            p_all = jnp.concatenate(p_rows, axis=0).astype(BF16)
            rr = jnp.dot(p_all, ones_ws[h], preferred_element_type=F32)
            vrep = jnp.concatenate([vs[h][i * SUB:(i + 1) * SUB] for i in range(n) for _ in range(SUB)], axis=0)
            o = o + jnp.dot(sel, (rr * vrep).astype(BF16), preferred_element_type=F32)
        outs.append(o)

    first = 0 if fast else 1
    a_rows = [[zero_blk[:, :c]] * first for _ in range(nh)]
    for i in range(first, n):
        lhs = jnp.concatenate([mask(qt[i], h) for h in range(nh)], axis=0).astype(BF16)
        a_i = _dot_t(lhs, rhs(i).astype(BF16))
        for h in range(nh):
            a_rows[h].append(a_i[h * SUB:(h + 1) * SUB])
    if fast:
        causal = lax.broadcasted_iota(jnp.int32, (c, c), 1) <= lax.broadcasted_iota(jnp.int32, (c, c), 0)
    for h in range(nh):
        a = jnp.concatenate(a_rows[h], axis=0)
        if fast:
            a = jnp.where(causal, a, 0.0)
        outs[h] = outs[h] + jnp.dot(a.astype(BF16), vs[h].astype(BF16), preferred_element_type=F32)

    k_end = rhs(n)
    new_states = []
    for h in range(nh):
        upd = lax.dot_general(vs[h].astype(BF16), mask(k_end, h).astype(BF16),
                              (((0,), (0,)), ((), ())), preferred_element_type=F32)
        new_states.append(states[h] * jnp.exp(r[n]) + upd)
    return outs, new_states


def _scan_kernel(*refs, mode, chunk, nh):
    if mode == "gla":
        (zq_ref, zk_ref, zv_ref, zs_ref, zr_ref, wg2_ref, bg_ref, ng_ref, ones_ref, sel_ref,
         o_ref, sout_ref, zg_s, st_s) = refs
    else:
        (zq_ref, zf_ref, zv_ref, zr_ref, llb_ref, l1m_ref, oml_ref, ng_ref, ones_ref, sel_ref,
         o_ref, sout_ref, st_s) = refs
    ti = pl.program_id(2)
    tblk = zq_ref.shape[0]
    nu = zq_ref.shape[1] // LANES

    @pl.when(ti == 0)
    def _():
        st_s[...] = jnp.zeros_like(st_s)

    if mode == "gla":
        dk = LANES // nh
        zg_s[...] = jnp.dot(zs_ref[...].astype(BF16), wg2_ref[...], preferred_element_type=F32) + bg_ref[...]
        la_low = _log_sigmoid(jnp.min(zg_s[...], axis=0, keepdims=True)) * (1.0 / GLA_GATE_TEMP)
        lane = lax.broadcasted_iota(jnp.int32, (1, LANES), 1)
        masks = [((lane >= h * dk) & (lane < (h + 1) * dk)).astype(F32) for h in range(nh)]
    else:
        hf_min = jnp.min(zf_ref[...], axis=0, keepdims=True)
        la_low = jnp.maximum(llb_ref[...], l1m_ref[...] + _log_sigmoid(hf_min))
        masks = [None]

    def features(sl, u):
        ul = slice(u * LANES, (u + 1) * LANES)
        if mode == "gla":
            return (zq_ref[sl, ul] * dk ** -0.5, zk_ref[sl, ul],
                    _log_sigmoid(zg_s[sl, ul]) * (1.0 / GLA_GATE_TEMP))
        hf = zf_ref[sl, ul]
        e = jnp.exp(-jnp.abs(hf))
        inv = 1.0 / (1.0 + e)
        sig_neg = jnp.where(hf > 0, e * inv, inv)
        lsig = jnp.minimum(hf, 0.0) - jnp.log(1.0 + e)
        a = llb_ref[:, ul]
        bb = l1m_ref[:, ul] + lsig
        la = jnp.maximum(a, bb) + jnp.log(1.0 + jnp.exp(-jnp.abs(a - bb)))
        return _silu(zq_ref[sl, ul]), oml_ref[:, ul] * sig_neg, la

    sel = sel_ref[...]
    ones_ws = [ones_ref[h] for h in range(nh)]

    def body(ci, _, fast):
        off = pl.multiple_of(ci * chunk, chunk)
        sl = pl.ds(off, chunk)
        for u in range(nu):
            hl = [slice((u * nh + h) * LANES, (u * nh + h + 1) * LANES) for h in range(nh)]
            vs = [zv_ref[sl, hl[h]] for h in range(nh)]
            states = [st_s[u * nh + h] for h in range(nh)]
            q, k, la = features(sl, u)
            outs, new_states = _scan_chunk(q, k, la, vs, masks, states, ones_ws, sel, fast)
            for h in range(nh):
                st_s[u * nh + h] = new_states[h]
                o = _rms(outs[h], ng_ref[...]) * _silu(zr_ref[sl, hl[h]])
                o_ref[0, sl, hl[h]] = o.astype(o_ref.dtype)
        return 0

    safe = jnp.min(la_low) * SUB >= -FAST_BLOCK_DECAY

    @pl.when(safe)
    def _():
        lax.fori_loop(0, tblk // chunk, functools.partial(body, fast=True), 0)

    @pl.when(jnp.logical_not(safe))
    def _():
        lax.fori_loop(0, tblk // chunk, functools.partial(body, fast=False), 0)

    @pl.when(ti == pl.num_programs(2) - 1)
    def _():
        dk_out = LANES // nh
        for u in range(nu):
            for h in range(nh):
                sout_ref[0, u * nh + h] = st_s[u * nh + h].T[h * dk_out:(h + 1) * dk_out, :]


def _scan_consts(chunk, nh):
    n = chunk // SUB
    sel = np.zeros((chunk, n * SUB * SUB), np.float32)
    for i in range(n):
        for t in range(SUB):
            sel[i * SUB + t, i * 64 + t * SUB:i * 64 + (t + 1) * SUB] = 1.0
    ones = np.zeros((nh, LANES, LANES), np.float32)
    dk = LANES // nh
    for h in range(nh):
        ones[h, h * dk:(h + 1) * dk, :] = 1.0
    return jnp.asarray(ones, BF16), jnp.asarray(sel, BF16)


def _lane_block(col, name, width):
    assert (col[name] * LANES) % width == 0
    return col[name] * LANES // width


def _scan_gla(z, b, t, units, nu, col, wg2p, bg, ng, tblk, chunk):
    nh = 2
    nt = t // tblk
    ones, sel = _scan_consts(chunk, nh)
    kw, vw = nu * LANES, nu * nh * LANES
    zspec = lambda name, w: pl.BlockSpec((tblk, w), lambda i, u, j: (i * nt + j, _lane_block(col, name, w) + u))
    in_specs = [
        zspec("gq", kw), zspec("gk", kw), zspec("gv", vw),
        pl.BlockSpec((tblk, LANES), lambda i, u, j: (i * nt + j, col["small_g"])),
        zspec("gr", vw),
        pl.BlockSpec((LANES, kw), lambda i, u, j: (0, u)),
        pl.BlockSpec((1, kw), lambda i, u, j: (0, u)),
        pl.BlockSpec((1, LANES), lambda i, u, j: (0, 0)),
        pl.BlockSpec(ones.shape, lambda i, u, j: (0, 0, 0)),
        pl.BlockSpec(sel.shape, lambda i, u, j: (0, 0)),
    ]
    dk = LANES // nh
    return pl.pallas_call(
        functools.partial(_scan_kernel, mode="gla", chunk=chunk, nh=nh),
        grid=(b, units // nu, nt),
        in_specs=in_specs,
        out_specs=[pl.BlockSpec((1, tblk, vw), lambda i, u, j: (i, j, u)),
                   pl.BlockSpec((1, nu * nh, dk, LANES), lambda i, u, j: (i, u, 0, 0))],
        out_shape=[jax.ShapeDtypeStruct((b, t, units * nh * LANES), BF16),
                   jax.ShapeDtypeStruct((b, units * nh, dk, LANES), F32)],
        scratch_shapes=[pltpu.VMEM((tblk, kw), F32), pltpu.VMEM((nu * nh, LANES, LANES), F32)],
        compiler_params=_cparams(("parallel", "parallel", "arbitrary")),
        name="scan_gla",
    )(z, z, z, z, z, wg2p, bg, ng, ones, sel)


def _scan_hgrn(z, b, t, units, nu, col, llb, l1m, oml, ng, tblk, chunk):
    nh = 1
    nt = t // tblk
    ones, sel = _scan_consts(chunk, nh)
    kw = nu * LANES
    zspec = lambda name: pl.BlockSpec((tblk, kw), lambda i, u, j: (i * nt + j, _lane_block(col, name, kw) + u))
    pspec = pl.BlockSpec((1, kw), lambda i, u, j: (0, u))
    in_specs = [zspec("hq"), zspec("hf"), zspec("hi"), zspec("hg"), pspec, pspec, pspec,
                pl.BlockSpec((1, LANES), lambda i, u, j: (0, 0)),
                pl.BlockSpec(ones.shape, lambda i, u, j: (0, 0, 0)),
                pl.BlockSpec(sel.shape, lambda i, u, j: (0, 0))]
    return pl.pallas_call(
        functools.partial(_scan_kernel, mode="hgrn", chunk=chunk, nh=nh),
        grid=(b, units // nu, nt),
        in_specs=in_specs,
        out_specs=[pl.BlockSpec((1, tblk, kw), lambda i, u, j: (i, j, u)),
                   pl.BlockSpec((1, nu, LANES, LANES), lambda i, u, j: (i, u, 0, 0))],
        out_shape=[jax.ShapeDtypeStruct((b, t, units * LANES), BF16),
                   jax.ShapeDtypeStruct((b, units, LANES, LANES), F32)],
        scratch_shapes=[pltpu.VMEM((nu, LANES, LANES), F32)],
        compiler_params=_cparams(("parallel", "parallel", "arbitrary")),
        name="scan_hgrn",
    )(z, z, z, z, llb, l1m, oml, ng, ones, sel)


def _rec_kernel(*refs, mode, nseq, ntok, dk):
    if mode == "gla":
        (qt_ref, kt_ref, st_ref, wg2t_ref, bgt_ref, zv_ref, zr_ref, ng_ref, s0_ref, o_ref, sout_ref, o_s) = refs
        qc = qt_ref[...] * dk ** -0.5
        kc = kt_ref[...]
        zg = jnp.dot(wg2t_ref[...], st_ref[...].astype(BF16), preferred_element_type=F32) + bgt_ref[...]
        ac = jnp.exp(_log_sigmoid(zg) * (1.0 / GLA_GATE_TEMP))
    else:
        (qt_ref, ft_ref, llb_ref, l1m_ref, oml_ref, zv_ref, zr_ref, ng_ref, s0_ref, o_ref, sout_ref, o_s) = refs
        hf = ft_ref[...]
        lsig = _log_sigmoid(hf)
        a = llb_ref[...]
        bb = l1m_ref[...] + lsig
        ac = jnp.exp(jnp.maximum(a, bb) + jnp.log(1.0 + jnp.exp(-jnp.abs(a - bb))))
        kc = oml_ref[...] * _sigmoid(-hf)
        qc = _silu(qt_ref[...])
    for sq in range(nseq):
        s = s0_ref[sq, 0]
        for t in range(ntok):
            j = sq * ntok + t
            vrow = zv_ref[j:j + 1, :]
            s = s * ac[:, j:j + 1] + kc[:, j:j + 1] * vrow
            o_s[j:j + 1, :] = jnp.sum(s * qc[:, j:j + 1], axis=0, keepdims=True)
        sout_ref[sq, 0] = s
    o = _rms(o_s[...], ng_ref[...]) * _silu(zr_ref[...])
    o_ref[...] = o.astype(o_ref.dtype)


def _rec(mode, zt, z, col, params, ng, s0, nb, ntok, heads, dk, nseq):
    m = nb * ntok
    rows = nseq * ntok
    ng_groups = nb // nseq
    if mode == "gla":
        wg2t, bgt = params
        per = LANES // dk
        tspec = lambda name: pl.BlockSpec((None, dk, rows), lambda h, g: (g, col[name] * per + h, 0))
        in_specs = [tspec("gq"), tspec("gk"),
                    pl.BlockSpec((None, LANES, rows), lambda h, g: (g, col["small_g"], 0)),
                    pl.BlockSpec((dk, LANES), lambda h, g: (h, 0)),
                    pl.BlockSpec((dk, 1), lambda h, g: (h, 0)),
                    pl.BlockSpec((rows, LANES), lambda h, g: (g, col["gv"] + h)),
                    pl.BlockSpec((rows, LANES), lambda h, g: (g, col["gr"] + h))]
        args = (zt, zt, zt, wg2t, bgt, z, z)
    else:
        llb, l1m, oml = params
        tspec = lambda name: pl.BlockSpec((None, dk, rows), lambda h, g: (g, col[name] + h, 0))
        pspec = pl.BlockSpec((dk, 1), lambda h, g: (h, 0))
        in_specs = [tspec("hq"), tspec("hf"), pspec, pspec, pspec,
                    pl.BlockSpec((rows, LANES), lambda h, g: (g, col["hi"] + h)),
                    pl.BlockSpec((rows, LANES), lambda h, g: (g, col["hg"] + h))]
        args = (zt, zt, llb, l1m, oml, z, z)
    in_specs += [pl.BlockSpec((1, LANES), lambda h, g: (0, 0)),
                 pl.BlockSpec((nseq, 1, dk, LANES), lambda h, g: (g, h, 0, 0))]
    return pl.pallas_call(
        functools.partial(_rec_kernel, mode=mode, nseq=nseq, ntok=ntok, dk=dk),
        grid=(heads, ng_groups),
        in_specs=in_specs,
        out_specs=[pl.BlockSpec((rows, LANES), lambda h, g: (g, h)),
                   pl.BlockSpec((nseq, 1, dk, LANES), lambda h, g: (g, h, 0, 0))],
        out_shape=[jax.ShapeDtypeStruct((m, heads * LANES), BF16),
                   jax.ShapeDtypeStruct(s0.shape, F32)],
        scratch_shapes=[pltpu.VMEM((rows, LANES), F32)],
        compiler_params=_cparams(("parallel", "parallel")),
        name="rec_" + mode,
    )(*args, ng, s0)


def _lfpool_kernel(lf_ref, o_ref, *, nh):
    x = lf_ref[...]
    w = x.shape[1]
    lane = lax.broadcasted_iota(jnp.int32, x.shape, 1)
    incl = x
    sh = nh
    while sh < w:
        incl = incl + jnp.where(lane < w - sh, pltpu.roll(incl, w - sh, 1), 0.0)
        sh *= 2
    tot = jnp.where(lane < nh, incl, 0.0)
    sh = nh
    while sh < w:
        tot = tot + pltpu.roll(tot, sh, 1)
        sh *= 2
    o_ref[:, :w] = incl - x
    o_ref[:, w:] = tot


def _lfpool(clf, nh, rows):
    depth, n_pool, w = clf.shape
    return pl.pallas_call(
        functools.partial(_lfpool_kernel, nh=nh),
        grid=(depth, n_pool // rows),
        in_specs=[pl.BlockSpec((None, rows, w), lambda l, i: (l, i, 0))],
        out_specs=pl.BlockSpec((None, rows, 2 * w), lambda l, i: (l, i, 0)),
        out_shape=jax.ShapeDtypeStruct((depth, n_pool, 2 * w), F32),
        compiler_params=_cparams(("parallel", "parallel")),
        name="lfpool",
    )(clf)


def _paged_kernel(pt_ref, q_ref, kn_ref, vn_ref, zs_ref, bf_ref, mask_ref, *rest, npg, nh, dh, ntok, scale):
    k_refs = rest[:npg]
    v_refs = rest[npg:2 * npg]
    rt_ref, o_ref, lf_ref = rest[2 * npg:2 * npg + 3]
    m_s, l_s, acc_s, car_s = rest[2 * npg + 3:]
    g = pl.program_id(1)

    @pl.when(g == 0)
    def _():
        m_s[...] = jnp.full(m_s.shape, NEG, F32)
        l_s[...] = jnp.zeros_like(l_s)
        acc_s[...] = jnp.zeros_like(acc_s)
        car_s[...] = jnp.zeros_like(car_s)

    q = q_ref[0] * scale
    q_all = jnp.concatenate([q[:, h * dh:(h + 1) * dh] for h in range(nh)], axis=0).astype(BF16)
    w = mask_ref.shape[1]
    carry = car_s[...]
    s_pages = []
    for i in range(npg):
        rt = rt_ref[0, i:i + 1, :]
        bias = rt[:, :w] + carry
        carry = carry + rt[:, w:]
        s_pages.append(_dot_t(q_all, k_refs[i][...].astype(BF16)) + (mask_ref[...] + bias))
    car_s[...] = carry
    s = jnp.concatenate(s_pages, axis=1)
    m = m_s[...]
    m_new = jnp.maximum(m, jnp.max(s, axis=-1, keepdims=True))
    alpha = jnp.exp(m - m_new)
    p = jnp.exp(s - m_new)
    l_s[...] = alpha * l_s[...] + jnp.sum(p, axis=-1, keepdims=True)
    pv = jnp.zeros(acc_s.shape, F32)
    for i in range(npg):
        pv = pv + jnp.dot(p[:, i * w:(i + 1) * w].astype(BF16), v_refs[i][...].astype(BF16),
                          preferred_element_type=F32)
    acc_s[...] = alpha * acc_s[...] + pv
    m_s[...] = m_new

    @pl.when(g == pl.num_programs(1) - 1)
    def _():
        lfn = _log_sigmoid(zs_ref[0] + bf_ref[...])
        lf_ref[0] = lfn
        row = lax.broadcasted_iota(jnp.int32, lfn.shape, 0)
        cn = _local_cumsum(jnp.where(row < ntok, lfn, 0.0))
        trow = lax.broadcasted_iota(jnp.int32, (SUB, 1), 0)
        for h in range(nh):
            hs = slice(h * dh, (h + 1) * dh)
            rs = slice(h * SUB, (h + 1) * SUB)
            m, l, acc = m_s[rs, :], l_s[rs, :], acc_s[rs, :]
            for sp in range(ntok):
                logit = jnp.sum(q[:, hs] * kn_ref[0, sp:sp + 1, hs], axis=-1, keepdims=True)
                logit = logit - cn[sp:sp + 1, FF_OFF + h:FF_OFF + h + 1]
                logit = jnp.where(trow >= sp, logit, NEG)
                m_new = jnp.maximum(m, logit)
                alpha = jnp.exp(m - m_new)
                p = jnp.exp(logit - m_new)
                l = alpha * l + p
                acc = alpha * acc + p * vn_ref[0, sp:sp + 1, hs]
                m = m_new
            o_ref[0, :, hs] = (acc / l).astype(o_ref.dtype)


def _paged(layer, page_table, q, kn, vn, zs, bf_row, cache_k, cache_v, rt_seq, nh, dh, ntok, npg):
    nb = q.shape[0]
    w = cache_k.shape[2]
    npages = page_table.shape[1]
    ngrp = npages // npg
    mask = np.full((nh * SUB, w), NEG, np.float32)
    for h in range(nh):
        mask[h * SUB:(h + 1) * SUB, h::nh] = 0.0

    def page_of(b, g, pt, i):
        return pt[b, npages - 1 - (g * npg + i)]

    seq3 = lambda b, g, pt: (b, 0, 0)
    in_specs = [pl.BlockSpec((1, SUB, nh * dh), seq3)] * 3 + [
        pl.BlockSpec((1, SUB, LANES), seq3),
        pl.BlockSpec((1, LANES), lambda b, g, pt: (0, 0)),
        pl.BlockSpec(mask.shape, lambda b, g, pt: (0, 0))]
    kv_specs = [pl.BlockSpec((None, None, w, dh), lambda b, g, pt, i=i: (layer, page_of(b, g, pt, i), 0, 0))
                for i in range(npg)]
    in_specs += kv_specs + kv_specs
    in_specs += [pl.BlockSpec((1, npg, 2 * w), lambda b, g, pt: (b, g, 0))]
    grid_spec = pltpu.PrefetchScalarGridSpec(
        num_scalar_prefetch=1, grid=(nb, ngrp), in_specs=in_specs,
        out_specs=[pl.BlockSpec((1, SUB, nh * dh), seq3), pl.BlockSpec((1, SUB, LANES), seq3)],
        scratch_shapes=[pltpu.VMEM((nh * SUB, 1), F32), pltpu.VMEM((nh * SUB, 1), F32),
                        pltpu.VMEM((nh * SUB, dh), F32), pltpu.VMEM((1, w), F32)])
    return pl.pallas_call(
        functools.partial(_paged_kernel, npg=npg, nh=nh, dh=dh, ntok=ntok, scale=dh ** -0.5),
        grid_spec=grid_spec,
        out_shape=[jax.ShapeDtypeStruct((nb, SUB, nh * dh), F32),
                   jax.ShapeDtypeStruct((nb, SUB, LANES), F32)],
        compiler_params=_cparams(("parallel", "arbitrary")),
        name="paged",
    )(page_table, q, kn, vn, zs, bf_row, jnp.asarray(mask), *([cache_k] * npg), *([cache_v] * npg), rt_seq)


def _merge_kernel(x_ref, oa_ref, ob_ref, oc_ref, g1_ref, wg_ref, wa_ref, wb_ref, wc_ref, wo_ref, y_ref):
    x = x_ref[...]
    d = x.shape[1]
    xn = _rms(x, g1_ref[...]).astype(BF16)
    merged = jnp.zeros(x.shape, F32)
    for i, (o_ref, w_ref) in enumerate(((oa_ref, wa_ref), (ob_ref, wb_ref), (oc_ref, wc_ref))):
        gate = _sigmoid(jnp.dot(xn, wg_ref[:, i * d:(i + 1) * d], preferred_element_type=F32))
        merged = merged + gate * jnp.dot(o_ref[...], w_ref[...], preferred_element_type=F32)
    y_ref[...] = x + jnp.dot(merged.astype(BF16), wo_ref[...], preferred_element_type=F32)


def _merge(x, oa, ob, oc, g1, wg_all, layer, wa, wb, wc, wo, tm):
    m, d = x.shape
    const = lambda a: pl.BlockSpec(a.shape, lambda i: (0,) * a.ndim)
    rows = lambda a: pl.BlockSpec((tm, a.shape[1]), lambda i: (i, 0))
    return pl.pallas_call(
        _merge_kernel,
        grid=(m // tm,),
        in_specs=[rows(x), rows(oa), rows(ob), rows(oc), const(g1),
                  pl.BlockSpec((None,) + wg_all.shape[1:], lambda i: (layer, 0, 0)),
                  const(wa), const(wb), const(wc), const(wo)],
        out_specs=pl.BlockSpec((tm, d), lambda i: (i, 0)),
        out_shape=jax.ShapeDtypeStruct((m, d), F32),
        compiler_params=_cparams(("parallel",)),
        name="merge",
    )(x, oa, ob, oc, g1, wg_all, wa, wb, wc, wo)


def _ffn_kernel(x_ref, g2_ref, wg_ref, wu_ref, wd_ref, gf_ref, y_ref, *, final):
    x = x_ref[...]
    h = _rms(x, g2_ref[...]).astype(BF16)
    acc = x
    hidden = wg_ref.shape[1]
    for c0 in range(0, hidden, FFN_HIDDEN_CHUNK):
        cs = slice(c0, min(c0 + FFN_HIDDEN_CHUNK, hidden))
        a = jnp.dot(h, wg_ref[:, cs], preferred_element_type=F32)
        u = jnp.dot(h, wu_ref[:, cs], preferred_element_type=F32)
        acc = acc + jnp.dot((_silu(a) * u).astype(BF16), wd_ref[cs, :], preferred_element_type=F32)
    y_ref[...] = _rms(acc, gf_ref[...]) if final else acc


def _ffn(x, g2, wg, wu, wd, gf, tm, final):
    m, d = x.shape
    const = lambda a: pl.BlockSpec(a.shape, lambda i: (0,) * a.ndim)
    return pl.pallas_call(
        functools.partial(_ffn_kernel, final=final),
        grid=(m // tm,),
        in_specs=[pl.BlockSpec((tm, d), lambda i: (i, 0)), const(g2), const(wg), const(wu), const(wd), const(gf)],
        out_specs=pl.BlockSpec((tm, d), lambda i: (i, 0)),
        out_shape=jax.ShapeDtypeStruct((m, d), F32),
        compiler_params=_cparams(("parallel",)),
        name="ffn",
    )(x, g2, wg, wu, wd, gf)


def _pick(n, pref):
    for c in pref:
        if n % c == 0:
            return c
    return n


def kernel(x_prompt, x_sample, state_gla, cache_fox_k, cache_fox_v, cache_fox_logf, state_hgrn, page_table,
           norm1_g, w_in, gla_wg2, gla_bg, gla_norm_g, fox_bf, hg_lb_logits, hg_norm_g,
           w_branch_a, w_branch_b, w_branch_c, w_out, norm2_g, w_ffn_gate, w_ffn_up, w_ffn_down,
           final_norm_g):
    depth, d_model, _ = w_in.shape
    bp, tp, _ = x_prompt.shape
    nb, ntok, _ = x_sample.shape
    _, _, gh, gdk, gdv = state_gla.shape
    _, _, hh, hdk, hdv = state_hgrn.shape
    _, n_pool, page, fh, fdh = cache_fox_k.shape
    hidden = w_ffn_gate.shape[2]
    assert gdv == LANES and hdk == LANES and hdv == LANES and fdh == LANES and 2 * gdk == LANES
    assert ntok <= SUB and gh % 2 == 0

    gq_w, gv_w, f_w, h_w = gh * gdk, gh * gdv, fh * fdh, hh * hdk
    names = ["gq", "gk", "gv", "glr", "gr", "fq", "fk", "fv", "ff", "hq", "hf", "hi", "hg", "ga", "gb", "gc"]
    widths = [gq_w, gq_w, gv_w, GLA_GATE_RANK, gv_w, f_w, f_w, f_w, fh, h_w, h_w, h_w, h_w,
              d_model, d_model, d_model]
    starts = dict(zip(names, np.concatenate([[0], np.cumsum(widths)[:-1]]).tolist()))
    wid = dict(zip(names, widths))
    order = ["gq", "gk", "gv", "gr", "fq", "fk", "fv", "hq", "hf", "hi", "hg"]
    col, off = {}, 0
    for nm in order:
        col[nm] = off // LANES
        off += wid[nm]
    col["small_f"] = off // LANES
    col["small_g"] = off // LANES + 1
    n_used = off + 2 * LANES
    tn = min(14, n_used // LANES) * LANES
    n_pad = -(-n_used // tn) * tn

    blocks = [(starts[nm] + k, 0, LANES) for nm in order for k in range(0, wid[nm], LANES)]
    blocks += [(starts["ff"] - FF_OFF, FF_OFF, FF_OFF + fh),
               (starts["glr"] - GLR_OFF, GLR_OFF, GLR_OFF + GLA_GATE_RANK)]
    blocks += [(0, 0, 0)] * ((n_pad - n_used) // LANES)
    w_cols = jnp.transpose(w_in, (2, 0, 1))
    w_pad_all = _wprep(w_cols, blocks)
    w_gates_all = _wprep(w_cols, [(starts["ga"] + k, 0, LANES) for k in range(0, 3 * d_model, LANES)])

    lb_cum = jnp.cumsum(jax.nn.softmax(hg_lb_logits.astype(F32), axis=0), axis=0)
    hg_lb = lb_cum - lb_cum[:1]
    log_lb, log1m_lb, one_m_lb = jnp.log(hg_lb), jnp.log1p(-hg_lb), 1.0 - hg_lb


    xp = x_prompt.reshape(bp * tp, d_model)
    xs = jnp.pad(x_sample, ((0, 0), (0, SUB - ntok), (0, 0))).reshape(nb * SUB, d_model)
    ms = nb * SUB

    tm_p = _pick(bp * tp, (1024, 512, 256, 128))
    tt = _pick(tp, (1024, 512, 256, 128))
    fblk = _pick(tp, (1024, 512, 256))
    tblk = _pick(tp, (512, 256, 128))
    chunk = min(128, tblk)
    tm_e = _pick(bp * tp, (512, 256, 128))
    nseq = _pick(nb, (8, 4, 2, 1))
    npg = _pick(page_table.shape[1], (32, 16, 8, 4, 2, 1))

    ck = cache_fox_k.reshape(depth, n_pool, page * fh, fdh)
    cv = cache_fox_v.reshape(depth, n_pool, page * fh, fdh)
    rtot = _lfpool(cache_fox_logf.astype(F32).reshape(depth, n_pool, page * fh), fh, _pick(n_pool, (256, 128, 64, 32, 16, 8)))

    outs = {k: [] for k in ("gla_p", "gla_s", "k_p", "v_p", "lf_p", "k_s", "v_s", "lf_s", "hg_p", "hg_s")}

    for l in range(depth):
        g1 = norm1_g[l].reshape(1, d_model)
        g2 = norm2_g[l].reshape(1, d_model)
        bf_row = jnp.zeros((1, LANES), F32).at[0, FF_OFF:FF_OFF + fh].set(fox_bf[l])
        wg2p = jnp.zeros((LANES, gq_w), F32).at[GLR_OFF:GLR_OFF + GLA_GATE_RANK].set(gla_wg2[l])
        wg2_b = wg2p.astype(BF16)
        bg_row = gla_bg[l].reshape(1, gq_w)
        gng = gla_norm_g[l].reshape(1, LANES)
        hng = hg_norm_g[l].reshape(1, LANES)
        llb_u, l1m_u, oml_u = (a[l].reshape(1, h_w) for a in (log_lb, log1m_lb, one_m_lb))
        wa, wb, wc, wo = (w[l].astype(BF16) for w in (w_branch_a, w_branch_b, w_branch_c, w_out))
        wfg, wfu, wfd = (w[l].astype(BF16) for w in (w_ffn_gate, w_ffn_up, w_ffn_down))
        gf = final_norm_g.reshape(1, d_model)
        final = l == depth - 1

        z, fk, fv = _proj(xp, g1, w_pad_all, l, tm_p, tn, (col["fk"] * LANES, col["fv"] * LANES, fh, fdh))
        outs["k_p"].append(fk.reshape(bp, tp, fh, fdh))
        outs["v_p"].append(fv.reshape(bp, tp, fh, fdh))
        lf, ctok = _fox_prep(z, bf_row, bp, tp, col["small_f"], fh, tt)
        outs["lf_p"].append(lf)
        ob = _flash(z, ctok, bp, tp, fh, fdh, col["fq"], col["fk"], col["fv"], fblk)
        oa, sg = _scan_gla(z, bp, tp, gh // 2, GLA_UNITS_PER_STEP, col, wg2_b, bg_row, gng, tblk, chunk)
        outs["gla_p"].append(sg)
        oc, sh = _scan_hgrn(z, bp, tp, hh, HGRN_UNITS_PER_STEP, col, llb_u, l1m_u, oml_u, hng, tblk, chunk)
        outs["hg_p"].append(sh)
        x1 = _merge(xp, oa.reshape(bp * tp, -1), ob.reshape(bp * tp, -1), oc.reshape(bp * tp, -1),
                    g1, w_gates_all, l, wa, wb, wc, wo, tm_e)
        xp = _ffn(x1, g2, wfg, wfu, wfd, gf, tm_e, final)

        zs, = _proj(xs, g1, w_pad_all, l, ms, tn)
        zs3 = zs.reshape(nb, SUB, n_pad)
        take = lambda nm, w: zs3[:, :, col[nm] * LANES:col[nm] * LANES + w]
        fks, fvs = take("fk", f_w), take("fv", f_w)
        outs["k_s"].append(fks[:, :ntok].reshape(nb, ntok, fh, fdh))
        outs["v_s"].append(fvs[:, :ntok].reshape(nb, ntok, fh, fdh))
        small_s = take("small_f", LANES)
        rt_seq = jnp.take(rtot[l], page_table[:, ::-1], axis=0)
        obs, lfs = _paged(l, page_table, take("fq", f_w), fks, fvs, small_s, bf_row, ck, cv, rt_seq,
                          fh, fdh, ntok, npg)
        outs["lf_s"].append(lfs[:, :ntok, FF_OFF:FF_OFF + fh])
        obs = obs.astype(BF16)
        zc = zs3[:, :ntok].reshape(nb * ntok, n_pad)
        zt = zc.reshape(nb // nseq, nseq * ntok, n_pad).transpose(0, 2, 1)
        wg2t = wg2p.T.astype(BF16)
        oas, sgs = _rec("gla", zt, zc, col, (wg2t, gla_bg[l].reshape(-1, 1)), gng, state_gla[l],
                        nb, ntok, gh, gdk, nseq)
        outs["gla_s"].append(sgs)
        ocs, shs = _rec("hgrn", zt, zc, col, tuple(a[l].reshape(-1, 1) for a in (log_lb, log1m_lb, one_m_lb)),
                        hng, state_hgrn[l], nb, ntok, hh, hdk, nseq)
        outs["hg_s"].append(shs)
        pad_tok = lambda o: jnp.pad(o.reshape(nb, ntok, -1), ((0, 0), (0, SUB - ntok), (0, 0))).reshape(ms, -1)
        x1s = _merge(xs, pad_tok(oas), obs.reshape(ms, -1), pad_tok(ocs), g1, w_gates_all, l, wa, wb, wc, wo, ms)
        xs = _ffn(x1s, g2, wfg, wfu, wfd, gf, ms, final)

    st = lambda k: jnp.stack(outs[k])
    y_p = xp.reshape(bp, tp, d_model)
    y_s = xs.reshape(nb, SUB, d_model)[:, :ntok]
    return (y_p, y_s, st("gla_p"), st("gla_s"), st("k_p"), st("v_p"), st("lf_p"),
            st("k_s"), st("v_s"), st("lf_s"), st("hg_p"), st("hg_s"))
```

```python
import functools

import numpy as np
import jax
import jax.numpy as jnp
from jax import lax
from jax.experimental import pallas as pl
from jax.experimental.pallas import tpu as pltpu

F32 = jnp.float32
BF16 = jnp.bfloat16
EPS = 1e-6
NEG = -1e30
LANES = 128
SUB = 8
HI = lax.Precision.HIGHEST
VMEM_LIMIT = 56 * 1024 * 1024

FAST_BLOCK_DECAY = 60.0
GLA_UNITS_PER_STEP = 2
HGRN_UNITS_PER_STEP = 4
FFN_HIDDEN_CHUNK = 256
GLA_GATE_TEMP = 16.0
GLA_GATE_RANK = 16
FF_OFF = 0
GLR_OFF = 16


def _cparams(sem):
    return pltpu.CompilerParams(dimension_semantics=sem, vmem_limit_bytes=VMEM_LIMIT)


def _rms(x, g):
    return x * lax.rsqrt(jnp.mean(x * x, axis=-1, keepdims=True) + EPS) * g


def _sigmoid(x):
    return 1.0 / (1.0 + jnp.exp(-x))


def _log_sigmoid(x):
    return jnp.minimum(x, 0.0) - jnp.log(1.0 + jnp.exp(-jnp.abs(x)))


def _silu(x):
    return x * _sigmoid(x)


def _dot_t(a, b):
    return lax.dot_general(a, b, (((1,), (1,)), ((), ())), preferred_element_type=F32)


WPREP_BLOCKS_PER_STEP = 2


def _wprep_kernel(tbl_ref, *refs):
    *w_refs, o_ref = refs
    for b, w_ref in enumerate(w_refs):
        j = pl.program_id(0) * len(w_refs) + b
        col = lax.broadcasted_iota(jnp.int32, (w_ref.shape[0], w_ref.shape[2]), 0)
        keep = (col >= tbl_ref[1, j]) & (col < tbl_ref[2, j])
        for l in range(w_ref.shape[1]):
            o_ref[l, :, b * LANES:(b + 1) * LANES] = jnp.where(keep, w_ref[:, l, :], 0.0).T.astype(BF16)


def _wprep(w_cols, blocks):
    n_in, depth, d = w_cols.shape
    per = WPREP_BLOCKS_PER_STEP
    tbl = jnp.asarray(np.asarray(blocks, np.int32).T)
    assert all(0 <= b[0] and b[0] + LANES <= n_in for b in blocks) and len(blocks) % per == 0
    grid_spec = pltpu.PrefetchScalarGridSpec(
        num_scalar_prefetch=1, grid=(len(blocks) // per,),
        in_specs=[pl.BlockSpec((pl.Element(LANES), pl.Element(depth), pl.Element(d)),
                               lambda j, tbl, b=b: (tbl[0, j * per + b], 0, 0)) for b in range(per)],
        out_specs=pl.BlockSpec((depth, d, per * LANES), lambda j, tbl: (0, 0, j)))
    return pl.pallas_call(
        _wprep_kernel,
        grid_spec=grid_spec,
        out_shape=jax.ShapeDtypeStruct((depth, d, LANES * len(blocks)), BF16),
        compiler_params=_cparams(("parallel",)),
        name="wprep",
    )(tbl, *([w_cols] * per))


def _proj_kernel(x_ref, g_ref, w_ref, z_ref, *rest, kv):
    xn_ref = rest[-1]

    @pl.when(pl.program_id(1) == 0)
    def _():
        xn_ref[...] = _rms(x_ref[...], g_ref[...]).astype(BF16)

    zt = jnp.dot(xn_ref[...], w_ref[...], preferred_element_type=F32)
    z_ref[...] = zt
    if kv is not None:
        fk_ref, fv_ref = rest[:2]
        jkv, koff, voff, nh, dh = kv
        tm = zt.shape[0]

        @pl.when(pl.program_id(1) == jkv)
        def _():
            for h in range(nh):
                fk_ref[pl.ds(h, tm, stride=nh), :] = zt[:, koff + h * dh:koff + (h + 1) * dh]
                fv_ref[pl.ds(h, tm, stride=nh), :] = zt[:, voff + h * dh:voff + (h + 1) * dh]


def _proj(x, g, w_all, layer, tm, tn, kv_cols=None):
    m, d = x.shape
    n = w_all.shape[2]
    out_specs = [pl.BlockSpec((tm, tn), lambda i, j: (i, j))]
    out_shape = [jax.ShapeDtypeStruct((m, n), F32)]
    kv = None
    if kv_cols is not None:
        kc, vc, nh, dh = kv_cols
        assert kc // tn == (vc + nh * dh - 1) // tn
        kv = (kc // tn, kc % tn, vc % tn, nh, dh)
        out_specs += [pl.BlockSpec((tm * nh, dh), lambda i, j: (i, 0))] * 2
        out_shape += [jax.ShapeDtypeStruct((m * nh, dh), F32)] * 2
    return pl.pallas_call(
        functools.partial(_proj_kernel, kv=kv),
        grid=(m // tm, n // tn),
        in_specs=[pl.BlockSpec((tm, d), lambda i, j: (i, 0)),
                  pl.BlockSpec((1, d), lambda i, j: (0, 0)),
                  pl.BlockSpec((None, d, tn), lambda i, j: (layer, 0, j))],
        out_specs=out_specs,
        out_shape=out_shape,
        scratch_shapes=[pltpu.VMEM((tm, d), BF16)],
        compiler_params=_cparams(("parallel", "arbitrary")),
        name="proj",
    )(x, g, w_all)


def _fox_prep_kernel(zs_ref, bf_ref, tril_ref, lf_ref, c_ref, carry_ref, *, nh):
    @pl.when(pl.program_id(1) == 0)
    def _():
        carry_ref[...] = jnp.zeros_like(carry_ref)

    lf = _log_sigmoid(zs_ref[...] + bf_ref[...])
    lf_ref[0] = lf[:, FF_OFF:FF_OFF + nh]
    grp = tril_ref.shape[0]
    carry = carry_ref[...][:1]
    for g0 in range(0, lf.shape[0], grp):
        c = jnp.dot(tril_ref[...], lf[g0:g0 + grp], precision=HI, preferred_element_type=F32) + carry
        c_ref[0, g0:g0 + grp, :] = c
        carry = c[-1:]
    carry_ref[...] = jnp.broadcast_to(carry, carry_ref.shape)


def _fox_prep(z, bf_row, b, t, small_blk, nh, tt):
    grp = min(tt, LANES)
    tril = np.tril(np.ones((grp, grp), np.float32))
    nt = t // tt
    return pl.pallas_call(
        functools.partial(_fox_prep_kernel, nh=nh),
        grid=(b, nt),
        in_specs=[pl.BlockSpec((tt, LANES), lambda i, j: (i * nt + j, small_blk)),
                  pl.BlockSpec((1, LANES), lambda i, j: (0, 0)),
                  pl.BlockSpec((grp, grp), lambda i, j: (0, 0))],
        out_specs=[pl.BlockSpec((1, tt, nh), lambda i, j: (i, j, 0)),
                   pl.BlockSpec((1, tt, LANES), lambda i, j: (i, j, 0))],
        out_shape=[jax.ShapeDtypeStruct((b, t, nh), F32),
                   jax.ShapeDtypeStruct((b, t, LANES), F32)],
        scratch_shapes=[pltpu.VMEM((SUB, LANES), F32)],
        compiler_params=_cparams(("parallel", "arbitrary")),
        name="fox_prep",
    )(z, bf_row, jnp.asarray(tril))


BIAS_PIECES = 3
LOG2E = 1.4426950408889634


def _flash_kernel(q_ref, k_ref, v_ref, c_ref, eye_ref, place_ref, o_ref, kb_ref, vt_ref, sa_ref, sb_ref, *,
                  blk, scale, nh):
    h = pl.program_id(1)
    qi = pl.program_id(2)
    dh = q_ref.shape[1]

    @pl.when(qi == 0)
    def _():
        def prep(j, _):
            rows = pl.ds(pl.multiple_of(j * blk, blk), blk)
            kb_ref[rows, :dh] = k_ref[rows, :].astype(BF16)
            vt_ref[j] = _dot_t(eye_ref[...], v_ref[rows, :].astype(BF16)).astype(BF16)
            rem = c_ref[0, rows, :] * (-LOG2E)
            extra = jnp.zeros(rem.shape, F32)
            for piece in range(BIAS_PIECES):
                part = rem.astype(BF16)
                extra = extra + jnp.dot(part, place_ref[piece], preferred_element_type=F32)
                rem = rem - part.astype(F32)
            kb_ref[rows, dh:] = extra.astype(BF16)
            return 0

        lax.fori_loop(0, k_ref.shape[0] // blk, prep, 0)

    qblk = q_ref.shape[0]
    lane_q = lax.broadcasted_iota(jnp.int32, (qblk, dh), 1)
    own = (lane_q >= h * BIAS_PIECES) & (lane_q < (h + 1) * BIAS_PIECES)
    q = jnp.concatenate([(q_ref[...] * (scale * LOG2E)).astype(BF16),
                         jnp.where(own, 1.0, 0.0).astype(BF16)], axis=1)

    def scores(j):
        return _dot_t(kb_ref[pl.ds(pl.multiple_of(j * blk, blk), blk), :], q)

    def absorb(s_ref, j, carry, key_off):
        m, l, acc = carry
        s = s_ref[...]
        if key_off is not None:
            key = lax.broadcasted_iota(jnp.int32, (blk, qblk), 0) + key_off
            qry = lax.broadcasted_iota(jnp.int32, (blk, qblk), 1)
            s = jnp.where(key <= qry, s, NEG)
        m_new = jnp.maximum(m, jnp.max(s, axis=0, keepdims=True))
        alpha = jnp.exp2(m - m_new)
        p = jnp.exp2(s - m_new)
        l = alpha * l + jnp.sum(p, axis=0, keepdims=True)
        acc = alpha * acc + jnp.dot(vt_ref[j], p.astype(BF16), preferred_element_type=F32)
        return m_new, l, acc

    sa_ref[...] = scores(0)

    def pair(p, carry):
        j = 2 * p
        sb_ref[...] = scores(j + 1)
        carry = absorb(sa_ref, j, carry, None)
        sa_ref[...] = scores(j + 2)
        return absorb(sb_ref, j + 1, carry, None)

    init = (jnp.full((1, qblk), NEG, F32), jnp.zeros((1, qblk), F32), jnp.zeros((dh, qblk), F32))
    carry = lax.fori_loop(0, qi, pair, init)
    sb_ref[...] = scores(2 * qi + 1)
    carry = absorb(sa_ref, 2 * qi, carry, 0)
    m, l, acc = absorb(sb_ref, 2 * qi + 1, carry, blk)
    o_ref[0] = (acc / l).T.astype(o_ref.dtype)


def _flash(z, ctok, b, t, nh, dh, q_blk0, k_blk0, v_blk0, qblk):
    assert FF_OFF + nh <= LANES and nh * BIAS_PIECES <= dh and dh == LANES and qblk % 2 == 0
    blk = qblk // 2
    nq = t // qblk
    eye = np.eye(dh, dtype=np.float32)
    place = np.zeros((BIAS_PIECES, LANES, dh), np.float32)
    for piece in range(BIAS_PIECES):
        for hh in range(nh):
            place[piece, FF_OFF + hh, hh * BIAS_PIECES + piece] = 1.0
    return pl.pallas_call(
        functools.partial(_flash_kernel, blk=blk, scale=dh ** -0.5, nh=nh),
        grid=(b, nh, nq),
        in_specs=[pl.BlockSpec((qblk, dh), lambda i, h, j: (i * nq + j, q_blk0 + h)),
                  pl.BlockSpec((t, dh), lambda i, h, j: (i, k_blk0 + h)),
                  pl.BlockSpec((t, dh), lambda i, h, j: (i, v_blk0 + h)),
                  pl.BlockSpec((1, t, LANES), lambda i, h, j: (i, 0, 0)),
                  pl.BlockSpec(eye.shape, lambda i, h, j: (0, 0)),
                  pl.BlockSpec(place.shape, lambda i, h, j: (0, 0, 0))],
        out_specs=pl.BlockSpec((1, qblk, dh), lambda i, h, j: (i, j, h)),
        out_shape=jax.ShapeDtypeStruct((b, t, nh * dh), BF16),
        scratch_shapes=[pltpu.VMEM((t, 2 * dh), BF16), pltpu.VMEM((t // blk, dh, blk), BF16),
                        pltpu.VMEM((blk, qblk), F32), pltpu.VMEM((blk, qblk), F32)],
        compiler_params=_cparams(("parallel", "parallel", "arbitrary")),
        name="flash",
    )(z, z, z, ctok, jnp.asarray(eye, BF16), jnp.asarray(place, BF16))


def _local_cumsum(x):
    row = lax.broadcasted_iota(jnp.int32, x.shape, 0)
    for sh in (1, 2, 4):
        x = x + jnp.where(row >= sh, pltpu.roll(x, sh, 0), 0.0)
    return x


def _scan_chunk(q, k, la, vs, masks, states, ones_ws, sel, fast):
    c = q.shape[0]
    n = c // SUB
    assert n > 1
    nh = len(vs)
    sub_iota = lax.broadcasted_iota(jnp.int32, (SUB, LANES), 0)
    zero_blk = jnp.zeros((SUB, LANES), F32)

    r = [jnp.zeros((1, LANES), F32)]
    qt, kh, kt, p_rows = [], [], [], []
    for i in range(n):
        sl = slice(i * SUB, (i + 1) * SUB)
        qi, ki = q[sl], k[sl]
        li = _local_cumsum(la[sl])
        tot = li[SUB - 1:SUB]
        r.append(r[i] + tot)
        qt.append(qi * jnp.exp(li))
        if fast:
            kt.append(ki * jnp.exp(-li))
            kh.append(kt[i] * jnp.exp(tot))
        else:
            kh.append(ki * jnp.exp(tot - li))
            for t in range(SUB):
                d = jnp.where(sub_iota <= t, li[t:t + 1] - li, NEG)
                p_rows.append(jnp.exp(d) * (qi[t:t + 1] * ki))

    qbar = jnp.concatenate([qt[i] * jnp.exp(r[i]) for i in range(n)], axis=0)
    r_ends = jnp.concatenate(r[1:], axis=0)

    def rhs(i):
        g = jnp.exp(jnp.minimum(r[i] - r_ends, 0.0))
        blocks = [kh[j] * g[j:j + 1] if j + 1 < i else kh[j] for j in range(min(i, n))]
        if fast and i < n:
            blocks.append(kt[i])
        blocks += [zero_blk] * (n - len(blocks))
        return jnp.concatenate(blocks, axis=0)

    def mask(x, h):
        return x if masks[h] is None else x * masks[h]

    outs = []
    for h in range(nh):
        o = _dot_t(mask(qbar, h).astype(BF16), states[h].astype(BF16))
        if not fast:
            p_all = jnp.concatenate(p_rows, axis=0).astype(BF16)
            rr = jnp.dot(p_all, ones_ws[h], preferred_element_type=F32)
            vrep = jnp.concatenate([vs[h][i * SUB:(i + 1) * SUB] for i in range(n) for _ in range(SUB)], axis=0)
            o = o + jnp.dot(sel, (rr * vrep).astype(BF16), preferred_element_type=F32)
        outs.append(o)

    first = 0 if fast else 1
    a_rows = [[zero_blk[:, :c]] * first for _ in range(nh)]
    for i in range(first, n):
        lhs = jnp.concatenate([mask(qt[i], h) for h in range(nh)], axis=0).astype(BF16)
        a_i = _dot_t(lhs, rhs(i).astype(BF16))
        for h in range(nh):
            a_rows[h].append(a_i[h * SUB:(h + 1) * SUB])
    if fast:
        causal = lax.broadcasted_iota(jnp.int32, (c, c), 1) <= lax.broadcasted_iota(jnp.int32, (c, c), 0)
    for h in range(nh):
        a = jnp.concatenate(a_rows[h], axis=0)
        if fast:
            a = jnp.where(causal, a, 0.0)
        outs[h] = outs[h] + jnp.dot(a.astype(BF16), vs[h].astype(BF16), preferred_element_type=F32)

    k_end = rhs(n)
    new_states = []
    for h in range(nh):
        upd = lax.dot_general(vs[h].astype(BF16), mask(k_end, h).astype(BF16),
                              (((0,), (0,)), ((), ())), preferred_element_type=F32)
        new_states.append(states[h] * jnp.exp(r[n]) + upd)
    return outs, new_states


def _scan_kernel(*refs, mode, chunk, nh):
    if mode == "gla":
        (zq_ref, zk_ref, zv_ref, zs_ref, zr_ref, wg2_ref, bg_ref, ng_ref, ones_ref, sel_ref,
         o_ref, sout_ref, zg_s, st_s) = refs
    else:
        (zq_ref, zf_ref, zv_ref, zr_ref, llb_ref, l1m_ref, oml_ref, ng_ref, ones_ref, sel_ref,
         o_ref, sout_ref, st_s) = refs
    ti = pl.program_id(2)
    tblk = zq_ref.shape[0]
    nu = zq_ref.shape[1] // LANES

    @pl.when(ti == 0)
    def _():
        st_s[...] = jnp.zeros_like(st_s)

    if mode == "gla":
        dk = LANES // nh
        zg_s[...] = jnp.dot(zs_ref[...].astype(BF16), wg2_ref[...], preferred_element_type=F32) + bg_ref[...]
        la_low = _log_sigmoid(jnp.min(zg_s[...], axis=0, keepdims=True)) * (1.0 / GLA_GATE_TEMP)
        lane = lax.broadcasted_iota(jnp.int32, (1, LANES), 1)
        masks = [((lane >= h * dk) & (lane < (h + 1) * dk)).astype(F32) for h in range(nh)]
    else:
        hf_min = jnp.min(zf_ref[...], axis=0, keepdims=True)
        la_low = jnp.maximum(llb_ref[...], l1m_ref[...] + _log_sigmoid(hf_min))
        masks = [None]

    def features(sl, u):
        ul = slice(u * LANES, (u + 1) * LANES)
        if mode == "gla":
            return (zq_ref[sl, ul] * dk ** -0.5, zk_ref[sl, ul],
                    _log_sigmoid(zg_s[sl, ul]) * (1.0 / GLA_GATE_TEMP))
        hf = zf_ref[sl, ul]
        e = jnp.exp(-jnp.abs(hf))
        inv = 1.0 / (1.0 + e)
        sig_neg = jnp.where(hf > 0, e * inv, inv)
        lsig = jnp.minimum(hf, 0.0) - jnp.log(1.0 + e)
        a = llb_ref[:, ul]
        bb = l1m_ref[:, ul] + lsig
        la = jnp.maximum(a, bb) + jnp.log(1.0 + jnp.exp(-jnp.abs(a - bb)))
        return _silu(zq_ref[sl, ul]), oml_ref[:, ul] * sig_neg, la

    sel = sel_ref[...]
    ones_ws = [ones_ref[h] for h in range(nh)]

    def body(ci, _, fast):
        off = pl.multiple_of(ci * chunk, chunk)
        sl = pl.ds(off, chunk)
        for u in range(nu):
            hl = [slice((u * nh + h) * LANES, (u * nh + h + 1) * LANES) for h in range(nh)]
            vs = [zv_ref[sl, hl[h]] for h in range(nh)]
            states = [st_s[u * nh + h] for h in range(nh)]
            q, k, la = features(sl, u)
            outs, new_states = _scan_chunk(q, k, la, vs, masks, states, ones_ws, sel, fast)
            for h in range(nh):
                st_s[u * nh + h] = new_states[h]
                o = _rms(outs[h], ng_ref[...]) * _silu(zr_ref[sl, hl[h]])
                o_ref[0, sl, hl[h]] = o.astype(o_ref.dtype)
        return 0

    safe = jnp.min(la_low) * SUB >= -FAST_BLOCK_DECAY

    @pl.when(safe)
    def _():
        lax.fori_loop(0, tblk // chunk, functools.partial(body, fast=True), 0)

    @pl.when(jnp.logical_not(safe))
    def _():
        lax.fori_loop(0, tblk // chunk, functools.partial(body, fast=False), 0)

    @pl.when(ti == pl.num_programs(2) - 1)
    def _():
        dk_out = LANES // nh
        for u in range(nu):
            for h in range(nh):
                sout_ref[0, u * nh + h] = st_s[u * nh + h].T[h * dk_out:(h + 1) * dk_out, :]


def _scan_consts(chunk, nh):
    n = chunk // SUB
    sel = np.zeros((chunk, n * SUB * SUB), np.float32)
    for i in range(n):
        for t in range(SUB):
            sel[i * SUB + t, i * 64 + t * SUB:i * 64 + (t + 1) * SUB] = 1.0
    ones = np.zeros((nh, LANES, LANES), np.float32)
    dk = LANES // nh
    for h in range(nh):
        ones[h, h * dk:(h + 1) * dk, :] = 1.0
    return jnp.asarray(ones, BF16), jnp.asarray(sel, BF16)


def _lane_block(col, name, width):
    assert (col[name] * LANES) % width == 0
    return col[name] * LANES // width


def _scan_gla(z, b, t, units, nu, col, wg2p, bg, ng, tblk, chunk):
    nh = 2
    nt = t // tblk
    ones, sel = _scan_consts(chunk, nh)
    kw, vw = nu * LANES, nu * nh * LANES
    zspec = lambda name, w: pl.BlockSpec((tblk, w), lambda i, u, j: (i * nt + j, _lane_block(col, name, w) + u))
    in_specs = [
        zspec("gq", kw), zspec("gk", kw), zspec("gv", vw),
        pl.BlockSpec((tblk, LANES), lambda i, u, j: (i * nt + j, col["small_g"])),
        zspec("gr", vw),
        pl.BlockSpec((LANES, kw), lambda i, u, j: (0, u)),
        pl.BlockSpec((1, kw), lambda i, u, j: (0, u)),
        pl.BlockSpec((1, LANES), lambda i, u, j: (0, 0)),
        pl.BlockSpec(ones.shape, lambda i, u, j: (0, 0, 0)),
        pl.BlockSpec(sel.shape, lambda i, u, j: (0, 0)),
    ]
    dk = LANES // nh
    return pl.pallas_call(
        functools.partial(_scan_kernel, mode="gla", chunk=chunk, nh=nh),
        grid=(b, units // nu, nt),
        in_specs=in_specs,
        out_specs=[pl.BlockSpec((1, tblk, vw), lambda i, u, j: (i, j, u)),
                   pl.BlockSpec((1, nu * nh, dk, LANES), lambda i, u, j: (i, u, 0, 0))],
        out_shape=[jax.ShapeDtypeStruct((b, t, units * nh * LANES), BF16),
                   jax.ShapeDtypeStruct((b, units * nh, dk, LANES), F32)],
        scratch_shapes=[pltpu.VMEM((tblk, kw), F32), pltpu.VMEM((nu * nh, LANES, LANES), F32)],
        compiler_params=_cparams(("parallel", "parallel", "arbitrary")),
        name="scan_gla",
    )(z, z, z, z, z, wg2p, bg, ng, ones, sel)


def _scan_hgrn(z, b, t, units, nu, col, llb, l1m, oml, ng, tblk, chunk):
    nh = 1
    nt = t // tblk
    ones, sel = _scan_consts(chunk, nh)
    kw = nu * LANES
    zspec = lambda name: pl.BlockSpec((tblk, kw), lambda i, u, j: (i * nt + j, _lane_block(col, name, kw) + u))
    pspec = pl.BlockSpec((1, kw), lambda i, u, j: (0, u))
    in_specs = [zspec("hq"), zspec("hf"), zspec("hi"), zspec("hg"), pspec, pspec, pspec,
                pl.BlockSpec((1, LANES), lambda i, u, j: (0, 0)),
                pl.BlockSpec(ones.shape, lambda i, u, j: (0, 0, 0)),
                pl.BlockSpec(sel.shape, lambda i, u, j: (0, 0))]
    return pl.pallas_call(
        functools.partial(_scan_kernel, mode="hgrn", chunk=chunk, nh=nh),
        grid=(b, units // nu, nt),
        in_specs=in_specs,
        out_specs=[pl.BlockSpec((1, tblk, kw), lambda i, u, j: (i, j, u)),
                   pl.BlockSpec((1, nu, LANES, LANES), lambda i, u, j: (i, u, 0, 0))],
        out_shape=[jax.ShapeDtypeStruct((b, t, units * LANES), BF16),
                   jax.ShapeDtypeStruct((b, units, LANES, LANES), F32)],
        scratch_shapes=[pltpu.VMEM((nu, LANES, LANES), F32)],
        compiler_params=_cparams(("parallel", "parallel", "arbitrary")),
        name="scan_hgrn",
    )(z, z, z, z, llb, l1m, oml, ng, ones, sel)


def _rec_kernel(*refs, mode, nseq, ntok, dk):
    if mode == "gla":
        (qt_ref, kt_ref, st_ref, wg2t_ref, bgt_ref, zv_ref, zr_ref, ng_ref, s0_ref, o_ref, sout_ref, o_s) = refs
        qc = qt_ref[...] * dk ** -0.5
        kc = kt_ref[...]
        zg = jnp.dot(wg2t_ref[...], st_ref[...].astype(BF16), preferred_element_type=F32) + bgt_ref[...]
        ac = jnp.exp(_log_sigmoid(zg) * (1.0 / GLA_GATE_TEMP))
    else:
        (qt_ref, ft_ref, llb_ref, l1m_ref, oml_ref, zv_ref, zr_ref, ng_ref, s0_ref, o_ref, sout_ref, o_s) = refs
        hf = ft_ref[...]
        lsig = _log_sigmoid(hf)
        a = llb_ref[...]
        bb = l1m_ref[...] + lsig
        ac = jnp.exp(jnp.maximum(a, bb) + jnp.log(1.0 + jnp.exp(-jnp.abs(a - bb))))
        kc = oml_ref[...] * _sigmoid(-hf)
        qc = _silu(qt_ref[...])
    for sq in range(nseq):
        s = s0_ref[sq, 0]
        for t in range(ntok):
            j = sq * ntok + t
            vrow = zv_ref[j:j + 1, :]
            s = s * ac[:, j:j + 1] + kc[:, j:j + 1] * vrow
            o_s[j:j + 1, :] = jnp.sum(s * qc[:, j:j + 1], axis=0, keepdims=True)
        sout_ref[sq, 0] = s
    o = _rms(o_s[...], ng_ref[...]) * _silu(zr_ref[...])
    o_ref[...] = o.astype(o_ref.dtype)


def _rec(mode, zt, z, col, params, ng, s0, nb, ntok, heads, dk, nseq):
    m = nb * ntok
    rows = nseq * ntok
    ng_groups = nb // nseq
    if mode == "gla":
        wg2t, bgt = params
        per = LANES // dk
        tspec = lambda name: pl.BlockSpec((None, dk, rows), lambda h, g: (g, col[name] * per + h, 0))
        in_specs = [tspec("gq"), tspec("gk"),
                    pl.BlockSpec((None, LANES, rows), lambda h, g: (g, col["small_g"], 0)),
                    pl.BlockSpec((dk, LANES), lambda h, g: (h, 0)),
                    pl.BlockSpec((dk, 1), lambda h, g: (h, 0)),
                    pl.BlockSpec((rows, LANES), lambda h, g: (g, col["gv"] + h)),
                    pl.BlockSpec((rows, LANES), lambda h, g: (g, col["gr"] + h))]
        args = (zt, zt, zt, wg2t, bgt, z, z)
    else:
        llb, l1m, oml = params
        tspec = lambda name: pl.BlockSpec((None, dk, rows), lambda h, g: (g, col[name] + h, 0))
        pspec = pl.BlockSpec((dk, 1), lambda h, g: (h, 0))
        in_specs = [tspec("hq"), tspec("hf"), pspec, pspec, pspec,
                    pl.BlockSpec((rows, LANES), lambda h, g: (g, col["hi"] + h)),
                    pl.BlockSpec((rows, LANES), lambda h, g: (g, col["hg"] + h))]
        args = (zt, zt, llb, l1m, oml, z, z)
    in_specs += [pl.BlockSpec((1, LANES), lambda h, g: (0, 0)),
                 pl.BlockSpec((nseq, 1, dk, LANES), lambda h, g: (g, h, 0, 0))]
    return pl.pallas_call(
        functools.partial(_rec_kernel, mode=mode, nseq=nseq, ntok=ntok, dk=dk),
        grid=(heads, ng_groups),
        in_specs=in_specs,
        out_specs=[pl.BlockSpec((rows, LANES), lambda h, g: (g, h)),
                   pl.BlockSpec((nseq, 1, dk, LANES), lambda h, g: (g, h, 0, 0))],
        out_shape=[jax.ShapeDtypeStruct((m, heads * LANES), BF16),
                   jax.ShapeDtypeStruct(s0.shape, F32)],
        scratch_shapes=[pltpu.VMEM((rows, LANES), F32)],
        compiler_params=_cparams(("parallel", "parallel")),
        name="rec_" + mode,
    )(*args, ng, s0)


def _lfpool_kernel(lf_ref, o_ref, *, nh):
    x = lf_ref[...]
    w = x.shape[1]
    lane = lax.broadcasted_iota(jnp.int32, x.shape, 1)
    incl = x
    sh = nh
    while sh < w:
        incl = incl + jnp.where(lane < w - sh, pltpu.roll(incl, w - sh, 1), 0.0)
        sh *= 2
    tot = jnp.where(lane < nh, incl, 0.0)
    sh = nh
    while sh < w:
        tot = tot + pltpu.roll(tot, sh, 1)
        sh *= 2
    o_ref[:, :w] = incl - x
    o_ref[:, w:] = tot


def _lfpool(clf, nh, rows):
    depth, n_pool, w = clf.shape
    return pl.pallas_call(
        functools.partial(_lfpool_kernel, nh=nh),
        grid=(depth, n_pool // rows),
        in_specs=[pl.BlockSpec((None, rows, w), lambda l, i: (l, i, 0))],
        out_specs=pl.BlockSpec((None, rows, 2 * w), lambda l, i: (l, i, 0)),
        out_shape=jax.ShapeDtypeStruct((depth, n_pool, 2 * w), F32),
        compiler_params=_cparams(("parallel", "parallel")),
        name="lfpool",
    )(clf)


def _paged_kernel(pt_ref, q_ref, kn_ref, vn_ref, zs_ref, bf_ref, mask_ref, *rest, npg, nh, dh, ntok, scale):
    k_refs = rest[:npg]
    v_refs = rest[npg:2 * npg]
    rt_ref, o_ref, lf_ref = rest[2 * npg:2 * npg + 3]
    m_s, l_s, acc_s, car_s = rest[2 * npg + 3:]
    g = pl.program_id(1)

    @pl.when(g == 0)
    def _():
        m_s[...] = jnp.full(m_s.shape, NEG, F32)
        l_s[...] = jnp.zeros_like(l_s)
        acc_s[...] = jnp.zeros_like(acc_s)
        car_s[...] = jnp.zeros_like(car_s)

    q = q_ref[0] * scale
    q_all = jnp.concatenate([q[:, h * dh:(h + 1) * dh] for h in range(nh)], axis=0).astype(BF16)
    w = mask_ref.shape[1]
    carry = car_s[...]
    s_pages = []
    for i in range(npg):
        rt = rt_ref[0, i:i + 1, :]
        bias = rt[:, :w] + carry
        carry = carry + rt[:, w:]
        s_pages.append(_dot_t(q_all, k_refs[i][...].astype(BF16)) + (mask_ref[...] + bias))
    car_s[...] = carry
    s = jnp.concatenate(s_pages, axis=1)
    m = m_s[...]
    m_new = jnp.maximum(m, jnp.max(s, axis=-1, keepdims=True))
    alpha = jnp.exp(m - m_new)
    p = jnp.exp(s - m_new)
    l_s[...] = alpha * l_s[...] + jnp.sum(p, axis=-1, keepdims=True)
    pv = jnp.zeros(acc_s.shape, F32)
    for i in range(npg):
        pv = pv + jnp.dot(p[:, i * w:(i + 1) * w].astype(BF16), v_refs[i][...].astype(BF16),
                          preferred_element_type=F32)
    acc_s[...] = alpha * acc_s[...] + pv
    m_s[...] = m_new

    @pl.when(g == pl.num_programs(1) - 1)
    def _():
        lfn = _log_sigmoid(zs_ref[0] + bf_ref[...])
        lf_ref[0] = lfn
        row = lax.broadcasted_iota(jnp.int32, lfn.shape, 0)
        cn = _local_cumsum(jnp.where(row < ntok, lfn, 0.0))
        trow = lax.broadcasted_iota(jnp.int32, (SUB, 1), 0)
        for h in range(nh):
            hs = slice(h * dh, (h + 1) * dh)
            rs = slice(h * SUB, (h + 1) * SUB)
            m, l, acc = m_s[rs, :], l_s[rs, :], acc_s[rs, :]
            for sp in range(ntok):
                logit = jnp.sum(q[:, hs] * kn_ref[0, sp:sp + 1, hs], axis=-1, keepdims=True)
                logit = logit - cn[sp:sp + 1, FF_OFF + h:FF_OFF + h + 1]
                logit = jnp.where(trow >= sp, logit, NEG)
                m_new = jnp.maximum(m, logit)
                alpha = jnp.exp(m - m_new)
                p = jnp.exp(logit - m_new)
                l = alpha * l + p
                acc = alpha * acc + p * vn_ref[0, sp:sp + 1, hs]
                m = m_new
            o_ref[0, :, hs] = (acc / l).astype(o_ref.dtype)


def _paged(layer, page_table, q, kn, vn, zs, bf_row, cache_k, cache_v, rt_seq, nh, dh, ntok, npg):
    nb = q.shape[0]
    w = cache_k.shape[2]
    npages = page_table.shape[1]
    ngrp = npages // npg
    mask = np.full((nh * SUB, w), NEG, np.float32)
    for h in range(nh):
        mask[h * SUB:(h + 1) * SUB, h::nh] = 0.0

    def page_of(b, g, pt, i):
        return pt[b, npages - 1 - (g * npg + i)]

    seq3 = lambda b, g, pt: (b, 0, 0)
    in_specs = [pl.BlockSpec((1, SUB, nh * dh), seq3)] * 3 + [
        pl.BlockSpec((1, SUB, LANES), seq3),
        pl.BlockSpec((1, LANES), lambda b, g, pt: (0, 0)),
        pl.BlockSpec(mask.shape, lambda b, g, pt: (0, 0))]
    kv_specs = [pl.BlockSpec((None, None, w, dh), lambda b, g, pt, i=i: (layer, page_of(b, g, pt, i), 0, 0))
                for i in range(npg)]
    in_specs += kv_specs + kv_specs
    in_specs += [pl.BlockSpec((1, npg, 2 * w), lambda b, g, pt: (b, g, 0))]
    grid_spec = pltpu.PrefetchScalarGridSpec(
        num_scalar_prefetch=1, grid=(nb, ngrp), in_specs=in_specs,
        out_specs=[pl.BlockSpec((1, SUB, nh * dh), seq3), pl.BlockSpec((1, SUB, LANES), seq3)],
        scratch_shapes=[pltpu.VMEM((nh * SUB, 1), F32), pltpu.VMEM((nh * SUB, 1), F32),
                        pltpu.VMEM((nh * SUB, dh), F32), pltpu.VMEM((1, w), F32)])
    return pl.pallas_call(
        functools.partial(_paged_kernel, npg=npg, nh=nh, dh=dh, ntok=ntok, scale=dh ** -0.5),
        grid_spec=grid_spec,
        out_shape=[jax.ShapeDtypeStruct((nb, SUB, nh * dh), F32),
                   jax.ShapeDtypeStruct((nb, SUB, LANES), F32)],
        compiler_params=_cparams(("parallel", "arbitrary")),
        name="paged",
    )(page_table, q, kn, vn, zs, bf_row, jnp.asarray(mask), *([cache_k] * npg), *([cache_v] * npg), rt_seq)


def _merge_kernel(x_ref, oa_ref, ob_ref, oc_ref, g1_ref, wg_ref, wa_ref, wb_ref, wc_ref, wo_ref, y_ref):
    x = x_ref[...]
    d = x.shape[1]
    xn = _rms(x, g1_ref[...]).astype(BF16)
    merged = jnp.zeros(x.shape, F32)
    for i, (o_ref, w_ref) in enumerate(((oa_ref, wa_ref), (ob_ref, wb_ref), (oc_ref, wc_ref))):
        gate = _sigmoid(jnp.dot(xn, wg_ref[:, i * d:(i + 1) * d], preferred_element_type=F32))
        merged = merged + gate * jnp.dot(o_ref[...], w_ref[...], preferred_element_type=F32)
    y_ref[...] = x + jnp.dot(merged.astype(BF16), wo_ref[...], preferred_element_type=F32)


def _merge(x, oa, ob, oc, g1, wg_all, layer, wa, wb, wc, wo, tm):
    m, d = x.shape
    const = lambda a: pl.BlockSpec(a.shape, lambda i: (0,) * a.ndim)
    rows = lambda a: pl.BlockSpec((tm, a.shape[1]), lambda i: (i, 0))
    return pl.pallas_call(
        _merge_kernel,
        grid=(m // tm,),
        in_specs=[rows(x), rows(oa), rows(ob), rows(oc), const(g1),
                  pl.BlockSpec((None,) + wg_all.shape[1:], lambda i: (layer, 0, 0)),
                  const(wa), const(wb), const(wc), const(wo)],
        out_specs=pl.BlockSpec((tm, d), lambda i: (i, 0)),
        out_shape=jax.ShapeDtypeStruct((m, d), F32),
        compiler_params=_cparams(("parallel",)),
        name="merge",
    )(x, oa, ob, oc, g1, wg_all, wa, wb, wc, wo)


def _ffn_kernel(x_ref, g2_ref, wg_ref, wu_ref, wd_ref, gf_ref, y_ref, *, final):
    x = x_ref[...]
    h = _rms(x, g2_ref[...]).astype(BF16)
    acc = x
    hidden = wg_ref.shape[1]
    for c0 in range(0, hidden, FFN_HIDDEN_CHUNK):
        cs = slice(c0, min(c0 + FFN_HIDDEN_CHUNK, hidden))
        a = jnp.dot(h, wg_ref[:, cs], preferred_element_type=F32)
        u = jnp.dot(h, wu_ref[:, cs], preferred_element_type=F32)
        acc = acc + jnp.dot((_silu(a) * u).astype(BF16), wd_ref[cs, :], preferred_element_type=F32)
    y_ref[...] = _rms(acc, gf_ref[...]) if final else acc


def _ffn(x, g2, wg, wu, wd, gf, tm, final):
    m, d = x.shape
    const = lambda a: pl.BlockSpec(a.shape, lambda i: (0,) * a.ndim)
    return pl.pallas_call(
        functools.partial(_ffn_kernel, final=final),
        grid=(m // tm,),
        in_specs=[pl.BlockSpec((tm, d), lambda i: (i, 0)), const(g2), const(wg), const(wu), const(wd), const(gf)],
        out_specs=pl.BlockSpec((tm, d), lambda i: (i, 0)),
        out_shape=jax.ShapeDtypeStruct((m, d), F32),
        compiler_params=_cparams(("parallel",)),
        name="ffn",
    )(x, g2, wg, wu, wd, gf)


def _pick(n, pref):
    for c in pref:
        if n % c == 0:
            return c
    return n


def kernel(x_prompt, x_sample, state_gla, cache_fox_k, cache_fox_v, cache_fox_logf, state_hgrn, page_table,
           norm1_g, w_in, gla_wg2, gla_bg, gla_norm_g, fox_bf, hg_lb_logits, hg_norm_g,
           w_branch_a, w_branch_b, w_branch_c, w_out, norm2_g, w_ffn_gate, w_ffn_up, w_ffn_down,
           final_norm_g):
    depth, d_model, _ = w_in.shape
    bp, tp, _ = x_prompt.shape
    nb, ntok, _ = x_sample.shape
    _, _, gh, gdk, gdv = state_gla.shape
    _, _, hh, hdk, hdv = state_hgrn.shape
    _, n_pool, page, fh, fdh = cache_fox_k.shape
    hidden = w_ffn_gate.shape[2]
    assert gdv == LANES and hdk == LANES and hdv == LANES and fdh == LANES and 2 * gdk == LANES
    assert ntok <= SUB and gh % 2 == 0

    gq_w, gv_w, f_w, h_w = gh * gdk, gh * gdv, fh * fdh, hh * hdk
    names = ["gq", "gk", "gv", "glr", "gr", "fq", "fk", "fv", "ff", "hq", "hf", "hi", "hg", "ga", "gb", "gc"]
    widths = [gq_w, gq_w, gv_w, GLA_GATE_RANK, gv_w, f_w, f_w, f_w, fh, h_w, h_w, h_w, h_w,
              d_model, d_model, d_model]
    starts = dict(zip(names, np.concatenate([[0], np.cumsum(widths)[:-1]]).tolist()))
    wid = dict(zip(names, widths))
    order = ["gq", "gk", "gv", "gr", "fq", "fk", "fv", "hq", "hf", "hi", "hg"]
    col, off = {}, 0
    for nm in order:
        col[nm] = off // LANES
        off += wid[nm]
    col["small_f"] = off // LANES
    col["small_g"] = off // LANES + 1
    n_used = off + 2 * LANES
    tn = min(14, n_used // LANES) * LANES
    n_pad = -(-n_used // tn) * tn

    blocks = [(starts[nm] + k, 0, LANES) for nm in order for k in range(0, wid[nm], LANES)]
    blocks += [(starts["ff"] - FF_OFF, FF_OFF, FF_OFF + fh),
               (starts["glr"] - GLR_OFF, GLR_OFF, GLR_OFF + GLA_GATE_RANK)]
    blocks += [(0, 0, 0)] * ((n_pad - n_used) // LANES)
    w_cols = jnp.transpose(w_in, (2, 0, 1))
    w_pad_all = _wprep(w_cols, blocks)
    w_gates_all = _wprep(w_cols, [(starts["ga"] + k, 0, LANES) for k in range(0, 3 * d_model, LANES)])

    lb_cum = jnp.cumsum(jax.nn.softmax(hg_lb_logits.astype(F32), axis=0), axis=0)
    hg_lb = lb_cum - lb_cum[:1]
    log_lb, log1m_lb, one_m_lb = jnp.log(hg_lb), jnp.log1p(-hg_lb), 1.0 - hg_lb


    xp = x_prompt.reshape(bp * tp, d_model)
    xs = jnp.pad(x_sample, ((0, 0), (0, SUB - ntok), (0, 0))).reshape(nb * SUB, d_model)
    ms = nb * SUB

    tm_p = _pick(bp * tp, (1024, 512, 256, 128))
    tt = _pick(tp, (1024, 512, 256, 128))
    fblk = _pick(tp, (1024, 512, 256))
    tblk = _pick(tp, (1024, 512, 256, 128))
    chunk = min(128, tblk)
    tm_e = _pick(bp * tp, (512, 256, 128))
    nseq = _pick(nb, (8, 4, 2, 1))
    npg = _pick(page_table.shape[1], (32, 16, 8, 4, 2, 1))

    ck = cache_fox_k.reshape(depth, n_pool, page * fh, fdh)
    cv = cache_fox_v.reshape(depth, n_pool, page * fh, fdh)
    rtot = _lfpool(cache_fox_logf.astype(F32).reshape(depth, n_pool, page * fh), fh, _pick(n_pool, (256, 128, 64, 32, 16, 8)))

    outs = {k: [] for k in ("gla_p", "gla_s", "k_p", "v_p", "lf_p", "k_s", "v_s", "lf_s", "hg_p", "hg_s")}

    for l in range(depth):
        g1 = norm1_g[l].reshape(1, d_model)
        g2 = norm2_g[l].reshape(1, d_model)
        bf_row = jnp.zeros((1, LANES), F32).at[0, FF_OFF:FF_OFF + fh].set(fox_bf[l])
        wg2p = jnp.zeros((LANES, gq_w), F32).at[GLR_OFF:GLR_OFF + GLA_GATE_RANK].set(gla_wg2[l])
        wg2_b = wg2p.astype(BF16)
        bg_row = gla_bg[l].reshape(1, gq_w)
        gng = gla_norm_g[l].reshape(1, LANES)
        hng = hg_norm_g[l].reshape(1, LANES)
        llb_u, l1m_u, oml_u = (a[l].reshape(1, h_w) for a in (log_lb, log1m_lb, one_m_lb))
        wa, wb, wc, wo = (w[l].astype(BF16) for w in (w_branch_a, w_branch_b, w_branch_c, w_out))
        wfg, wfu, wfd = (w[l].astype(BF16) for w in (w_ffn_gate, w_ffn_up, w_ffn_down))
        gf = final_norm_g.reshape(1, d_model)
        final = l == depth - 1

        z, fk, fv = _proj(xp, g1, w_pad_all, l, tm_p, tn, (col["fk"] * LANES, col["fv"] * LANES, fh, fdh))
        outs["k_p"].append(fk.reshape(bp, tp, fh, fdh))
        outs["v_p"].append(fv.reshape(bp, tp, fh, fdh))
        lf, ctok = _fox_prep(z, bf_row, bp, tp, col["small_f"], fh, tt)
        outs["lf_p"].append(lf)
        ob = _flash(z, ctok, bp, tp, fh, fdh, col["fq"], col["fk"], col["fv"], fblk)
        oa, sg = _scan_gla(z, bp, tp, gh // 2, GLA_UNITS_PER_STEP, col, wg2_b, bg_row, gng, tblk, chunk)
        outs["gla_p"].append(sg)
        oc, sh = _scan_hgrn(z, bp, tp, hh, HGRN_UNITS_PER_STEP, col, llb_u, l1m_u, oml_u, hng, tblk, chunk)
        outs["hg_p"].append(sh)
        x1 = _merge(xp, oa.reshape(bp * tp, -1), ob.reshape(bp * tp, -1), oc.reshape(bp * tp, -1),
                    g1, w_gates_all, l, wa, wb, wc, wo, tm_e)
        xp = _ffn(x1, g2, wfg, wfu, wfd, gf, tm_e, final)

        zs, = _proj(xs, g1, w_pad_all, l, ms, tn)
        zs3 = zs.reshape(nb, SUB, n_pad)
        take = lambda nm, w: zs3[:, :, col[nm] * LANES:col[nm] * LANES + w]
        fks, fvs = take("fk", f_w), take("fv", f_w)
        outs["k_s"].append(fks[:, :ntok].reshape(nb, ntok, fh, fdh))
        outs["v_s"].append(fvs[:, :ntok].reshape(nb, ntok, fh, fdh))
        small_s = take("small_f", LANES)
        rt_seq = jnp.take(rtot[l], page_table[:, ::-1], axis=0)
        obs, lfs = _paged(l, page_table, take("fq", f_w), fks, fvs, small_s, bf_row, ck, cv, rt_seq,
                          fh, fdh, ntok, npg)
        outs["lf_s"].append(lfs[:, :ntok, FF_OFF:FF_OFF + fh])
        obs = obs.astype(BF16)
        zc = zs3[:, :ntok].reshape(nb * ntok, n_pad)
        zt = zc.reshape(nb // nseq, nseq * ntok, n_pad).transpose(0, 2, 1)
        wg2t = wg2p.T.astype(BF16)
        oas, sgs = _rec("gla", zt, zc, col, (wg2t, gla_bg[l].reshape(-1, 1)), gng, state_gla[l],
                        nb, ntok, gh, gdk, nseq)
        outs["gla_s"].append(sgs)
        ocs, shs = _rec("hgrn", zt, zc, col, tuple(a[l].reshape(-1, 1) for a in (log_lb, log1m_lb, one_m_lb)),
                        hng, state_hgrn[l], nb, ntok, hh, hdk, nseq)
        outs["hg_s"].append(shs)
        pad_tok = lambda o: jnp.pad(o.reshape(nb, ntok, -1), ((0, 0), (0, SUB - ntok), (0, 0))).reshape(ms, -1)
        x1s = _merge(xs, pad_tok(oas), obs.reshape(ms, -1), pad_tok(ocs), g1, w_gates_all, l, wa, wb, wc, wo, ms)
        xs = _ffn(x1s, g2, wfg, wfu, wfd, gf, ms, final)

    st = lambda k: jnp.stack(outs[k])
    y_p = xp.reshape(bp, tp, d_model)
    y_s = xs.reshape(nb, SUB, d_model)[:, :ntok]
    return (y_p, y_s, st("gla_p"), st("gla_s"), st("k_p"), st("v_p"), st("lf_p"),
            st("k_s"), st("v_s"), st("lf_s"), st("hg_p"), st("hg_s"))
```

```python
import functools

import numpy as np
import jax
import jax.numpy as jnp
from jax import lax
from jax.experimental import pallas as pl
from jax.experimental.pallas import tpu as pltpu

F32 = jnp.float32
BF16 = jnp.bfloat16
EPS = 1e-6
NEG = -1e30
LANES = 128
SUB = 8
HI = lax.Precision.HIGHEST
VMEM_LIMIT = 56 * 1024 * 1024

FAST_BLOCK_DECAY = 60.0
GLA_UNITS_PER_STEP = 2
HGRN_UNITS_PER_STEP = 4
FFN_HIDDEN_CHUNK = 256
GLA_GATE_TEMP = 16.0
GLA_GATE_RANK = 16
FF_OFF = 0
GLR_OFF = 16


def _cparams(sem):
    return pltpu.CompilerParams(dimension_semantics=sem, vmem_limit_bytes=VMEM_LIMIT)


def _rms(x, g):
    return x * lax.rsqrt(jnp.mean(x * x, axis=-1, keepdims=True) + EPS) * g


def _sigmoid(x):
    return 1.0 / (1.0 + jnp.exp(-x))


def _log_sigmoid(x):
    return jnp.minimum(x, 0.0) - jnp.log(1.0 + jnp.exp(-jnp.abs(x)))


def _silu(x):
    return x * _sigmoid(x)


def _dot_t(a, b):
    return lax.dot_general(a, b, (((1,), (1,)), ((), ())), preferred_element_type=F32)


WPREP_BLOCKS_PER_STEP = 2


def _wprep_kernel(tbl_ref, *refs):
    *w_refs, o_ref = refs
    for b, w_ref in enumerate(w_refs):
        j = pl.program_id(0) * len(w_refs) + b
        col = lax.broadcasted_iota(jnp.int32, (w_ref.shape[0], w_ref.shape[2]), 0)
        keep = (col >= tbl_ref[1, j]) & (col < tbl_ref[2, j])
        for l in range(w_ref.shape[1]):
            o_ref[l, :, b * LANES:(b + 1) * LANES] = jnp.where(keep, w_ref[:, l, :], 0.0).T.astype(BF16)


def _wprep(w_cols, blocks):
    n_in, depth, d = w_cols.shape
    per = WPREP_BLOCKS_PER_STEP
    tbl = jnp.asarray(np.asarray(blocks, np.int32).T)
    assert all(0 <= b[0] and b[0] + LANES <= n_in for b in blocks) and len(blocks) % per == 0
    grid_spec = pltpu.PrefetchScalarGridSpec(
        num_scalar_prefetch=1, grid=(len(blocks) // per,),
        in_specs=[pl.BlockSpec((pl.Element(LANES), pl.Element(depth), pl.Element(d)),
                               lambda j, tbl, b=b: (tbl[0, j * per + b], 0, 0)) for b in range(per)],
        out_specs=pl.BlockSpec((depth, d, per * LANES), lambda j, tbl: (0, 0, j)))
    return pl.pallas_call(
        _wprep_kernel,
        grid_spec=grid_spec,
        out_shape=jax.ShapeDtypeStruct((depth, d, LANES * len(blocks)), BF16),
        compiler_params=_cparams(("parallel",)),
        name="wprep",
    )(tbl, *([w_cols] * per))


def _proj_kernel(x_ref, g_ref, w_ref, z_ref, *rest, kv):
    xn_ref = rest[-1]

    @pl.when(pl.program_id(1) == 0)
    def _():
        xn_ref[...] = _rms(x_ref[...], g_ref[...]).astype(BF16)

    zt = jnp.dot(xn_ref[...], w_ref[...], preferred_element_type=F32)
    z_ref[...] = zt
    if kv is not None:
        fk_ref, fv_ref = rest[:2]
        jkv, koff, voff, nh, dh = kv
        tm = zt.shape[0]

        @pl.when(pl.program_id(1) == jkv)
        def _():
            for h in range(nh):
                fk_ref[pl.ds(h, tm, stride=nh), :] = zt[:, koff + h * dh:koff + (h + 1) * dh]
                fv_ref[pl.ds(h, tm, stride=nh), :] = zt[:, voff + h * dh:voff + (h + 1) * dh]


def _proj(x, g, w_all, layer, tm, tn, kv_cols=None):
    m, d = x.shape
    n = w_all.shape[2]
    out_specs = [pl.BlockSpec((tm, tn), lambda i, j: (i, j))]
    out_shape = [jax.ShapeDtypeStruct((m, n), F32)]
    kv = None
    if kv_cols is not None:
        kc, vc, nh, dh = kv_cols
        assert kc // tn == (vc + nh * dh - 1) // tn
        kv = (kc // tn, kc % tn, vc % tn, nh, dh)
        out_specs += [pl.BlockSpec((tm * nh, dh), lambda i, j: (i, 0))] * 2
        out_shape += [jax.ShapeDtypeStruct((m * nh, dh), F32)] * 2
    return pl.pallas_call(
        functools.partial(_proj_kernel, kv=kv),
        grid=(m // tm, n // tn),
        in_specs=[pl.BlockSpec((tm, d), lambda i, j: (i, 0)),
                  pl.BlockSpec((1, d), lambda i, j: (0, 0)),
                  pl.BlockSpec((None, d, tn), lambda i, j: (layer, 0, j))],
        out_specs=out_specs,
        out_shape=out_shape,
        scratch_shapes=[pltpu.VMEM((tm, d), BF16)],
        compiler_params=_cparams(("parallel", "arbitrary")),
        name="proj",
    )(x, g, w_all)


def _fox_prep_kernel(zs_ref, bf_ref, tril_ref, lf_ref, c_ref, carry_ref, *, nh):
    @pl.when(pl.program_id(1) == 0)
    def _():
        carry_ref[...] = jnp.zeros_like(carry_ref)

    lf = _log_sigmoid(zs_ref[...] + bf_ref[...])
    lf_ref[0] = lf[:, FF_OFF:FF_OFF + nh]
    grp = tril_ref.shape[0]
    carry = carry_ref[...][:1]
    for g0 in range(0, lf.shape[0], grp):
        c = jnp.dot(tril_ref[...], lf[g0:g0 + grp], precision=HI, preferred_element_type=F32) + carry
        c_ref[0, g0:g0 + grp, :] = c
        carry = c[-1:]
    carry_ref[...] = jnp.broadcast_to(carry, carry_ref.shape)


def _fox_prep(z, bf_row, b, t, small_blk, nh, tt):
    grp = min(tt, LANES)
    tril = np.tril(np.ones((grp, grp), np.float32))
    nt = t // tt
    return pl.pallas_call(
        functools.partial(_fox_prep_kernel, nh=nh),
        grid=(b, nt),
        in_specs=[pl.BlockSpec((tt, LANES), lambda i, j: (i * nt + j, small_blk)),
                  pl.BlockSpec((1, LANES), lambda i, j: (0, 0)),
                  pl.BlockSpec((grp, grp), lambda i, j: (0, 0))],
        out_specs=[pl.BlockSpec((1, tt, nh), lambda i, j: (i, j, 0)),
                   pl.BlockSpec((1, tt, LANES), lambda i, j: (i, j, 0))],
        out_shape=[jax.ShapeDtypeStruct((b, t, nh), F32),
                   jax.ShapeDtypeStruct((b, t, LANES), F32)],
        scratch_shapes=[pltpu.VMEM((SUB, LANES), F32)],
        compiler_params=_cparams(("parallel", "arbitrary")),
        name="fox_prep",
    )(z, bf_row, jnp.asarray(tril))


BIAS_PIECES = 3
LOG2E = 1.4426950408889634


def _flash_kernel(q_ref, k_ref, v_ref, c_ref, eye_ref, place_ref, o_ref, kb_ref, vt_ref, sa_ref, sb_ref, *,
                  blk, scale, nh):
    h = pl.program_id(1)
    qi = pl.program_id(2)
    dh = q_ref.shape[1]

    @pl.when(qi == 0)
    def _():
        def prep(j, _):
            rows = pl.ds(pl.multiple_of(j * blk, blk), blk)
            kb_ref[rows, :dh] = k_ref[rows, :].astype(BF16)
            vt_ref[j] = _dot_t(eye_ref[...], v_ref[rows, :].astype(BF16)).astype(BF16)
            rem = c_ref[0, rows, :] * (-LOG2E)
            extra = jnp.zeros(rem.shape, F32)
            for piece in range(BIAS_PIECES):
                part = rem.astype(BF16)
                extra = extra + jnp.dot(part, place_ref[piece], preferred_element_type=F32)
                rem = rem - part.astype(F32)
            kb_ref[rows, dh:] = extra.astype(BF16)
            return 0

        lax.fori_loop(0, k_ref.shape[0] // blk, prep, 0)

    qblk = q_ref.shape[0]
    lane_q = lax.broadcasted_iota(jnp.int32, (qblk, dh), 1)
    own = (lane_q >= h * BIAS_PIECES) & (lane_q < (h + 1) * BIAS_PIECES)
    q = jnp.concatenate([(q_ref[...] * (scale * LOG2E)).astype(BF16),
                         jnp.where(own, 1.0, 0.0).astype(BF16)], axis=1)

    def scores(j):
        return _dot_t(kb_ref[pl.ds(pl.multiple_of(j * blk, blk), blk), :], q)

    def absorb(s_ref, j, carry, key_off):
        m, l, acc = carry
        s = s_ref[...]
        if key_off is not None:
            key = lax.broadcasted_iota(jnp.int32, (blk, qblk), 0) + key_off
            qry = lax.broadcasted_iota(jnp.int32, (blk, qblk), 1)
            s = jnp.where(key <= qry, s, NEG)
        m_new = jnp.maximum(m, jnp.max(s, axis=0, keepdims=True))
        alpha = jnp.exp2(m - m_new)
        p = jnp.exp2(s - m_new)
        l = alpha * l + jnp.sum(p, axis=0, keepdims=True)
        acc = alpha * acc + jnp.dot(vt_ref[j], p.astype(BF16), preferred_element_type=F32)
        return m_new, l, acc

    sa_ref[...] = scores(0)

    def pair(p, carry):
        j = 2 * p
        sb_ref[...] = scores(j + 1)
        carry = absorb(sa_ref, j, carry, None)
        sa_ref[...] = scores(j + 2)
        return absorb(sb_ref, j + 1, carry, None)

    init = (jnp.full((1, qblk), NEG, F32), jnp.zeros((1, qblk), F32), jnp.zeros((dh, qblk), F32))
    carry = lax.fori_loop(0, qi, pair, init)
    sb_ref[...] = scores(2 * qi + 1)
    carry = absorb(sa_ref, 2 * qi, carry, 0)
    m, l, acc = absorb(sb_ref, 2 * qi + 1, carry, blk)
    o_ref[0] = (acc / l).T.astype(o_ref.dtype)


def _flash(z, ctok, b, t, nh, dh, q_blk0, k_blk0, v_blk0, qblk):
    assert FF_OFF + nh <= LANES and nh * BIAS_PIECES <= dh and dh == LANES and qblk % 2 == 0
    blk = qblk // 2
    nq = t // qblk
    eye = np.eye(dh, dtype=np.float32)
    place = np.zeros((BIAS_PIECES, LANES, dh), np.float32)
    for piece in range(BIAS_PIECES):
        for hh in range(nh):
            place[piece, FF_OFF + hh, hh * BIAS_PIECES + piece] = 1.0
    return pl.pallas_call(
        functools.partial(_flash_kernel, blk=blk, scale=dh ** -0.5, nh=nh),
        grid=(b, nh, nq),
        in_specs=[pl.BlockSpec((qblk, dh), lambda i, h, j: (i * nq + j, q_blk0 + h)),
                  pl.BlockSpec((t, dh), lambda i, h, j: (i, k_blk0 + h)),
                  pl.BlockSpec((t, dh), lambda i, h, j: (i, v_blk0 + h)),
                  pl.BlockSpec((1, t, LANES), lambda i, h, j: (i, 0, 0)),
                  pl.BlockSpec(eye.shape, lambda i, h, j: (0, 0)),
                  pl.BlockSpec(place.shape, lambda i, h, j: (0, 0, 0))],
        out_specs=pl.BlockSpec((1, qblk, dh), lambda i, h, j: (i, j, h)),
        out_shape=jax.ShapeDtypeStruct((b, t, nh * dh), BF16),
        scratch_shapes=[pltpu.VMEM((t, 2 * dh), BF16), pltpu.VMEM((t // blk, dh, blk), BF16),
                        pltpu.VMEM((blk, qblk), F32), pltpu.VMEM((blk, qblk), F32)],
        compiler_params=_cparams(("parallel", "parallel", "arbitrary")),
        name="flash",
    )(z, z, z, ctok, jnp.asarray(eye, BF16), jnp.asarray(place, BF16))


def _local_cumsum(x):
    row = lax.broadcasted_iota(jnp.int32, x.shape, 0)
    for sh in (1, 2, 4):
        x = x + jnp.where(row >= sh, pltpu.roll(x, sh, 0), 0.0)
    return x


def _scan_chunk(q, k, la, vs, masks, states, ones_ws, sel, fast):
    c = q.shape[0]
    n = c // SUB
    assert n > 1
    nh = len(vs)
    sub_iota = lax.broadcasted_iota(jnp.int32, (SUB, LANES), 0)
    zero_blk = jnp.zeros((SUB, LANES), F32)

    r = [jnp.zeros((1, LANES), F32)]
    qt, kh, kt, p_rows = [], [], [], []
    for i in range(n):
        sl = slice(i * SUB, (i + 1) * SUB)
        qi, ki = q[sl], k[sl]
        li = _local_cumsum(la[sl])
        tot = li[SUB - 1:SUB]
        r.append(r[i] + tot)
        qt.append(qi * jnp.exp(li))
        if fast:
            kt.append(ki * jnp.exp(-li))
            kh.append(kt[i] * jnp.exp(tot))
        else:
            kh.append(ki * jnp.exp(tot - li))
            for t in range(SUB):
                d = jnp.where(sub_iota <= t, li[t:t + 1] - li, NEG)
                p_rows.append(jnp.exp(d) * (qi[t:t + 1] * ki))

    qbar = jnp.concatenate([qt[i] * jnp.exp(r[i]) for i in range(n)], axis=0)
    r_ends = jnp.concatenate(r[1:], axis=0)

    def rhs(i):
        g = jnp.exp(jnp.minimum(r[i] - r_ends, 0.0))
        blocks = [kh[j] * g[j:j + 1] if j + 1 < i else kh[j] for j in range(min(i, n))]
        if fast and i < n:
            blocks.append(kt[i])
        blocks += [zero_blk] * (n - len(blocks))
        return jnp.concatenate(blocks, axis=0)

    def mask(x, h):
        return x if masks[h] is None else x * masks[h]

    outs = []
    for h in range(nh):
        o = _dot_t(mask(qbar, h).astype(BF16), states[h].astype(BF16))
        if not fast:
            p_all = jnp.concatenate(p_rows, axis=0).astype(BF16)
            rr = jnp.dot(p_all, ones_ws[h], preferred_element_type=F32)
            vrep = jnp.concatenate([vs[h][i * SUB:(i + 1) * SUB] for i in range(n) for _ in range(SUB)], axis=0)
            o = o + jnp.dot(sel, (rr * vrep).astype(BF16), preferred_element_type=F32)
        outs.append(o)

    first = 0 if fast else 1
    a_rows = [[zero_blk[:, :c]] * first for _ in range(nh)]
    for i in range(first, n):
        lhs = jnp.concatenate([mask(qt[i], h) for h in range(nh)], axis=0).astype(BF16)
        a_i = _dot_t(lhs, rhs(i).astype(BF16))
        for h in range(nh):
            a_rows[h].append(a_i[h * SUB:(h + 1) * SUB])
    if fast:
        causal = lax.broadcasted_iota(jnp.int32, (c, c), 1) <= lax.broadcasted_iota(jnp.int32, (c, c), 0)
    for h in range(nh):
        a = jnp.concatenate(a_rows[h], axis=0)
        if fast:
            a = jnp.where(causal, a, 0.0)
        outs[h] = outs[h] + jnp.dot(a.astype(BF16), vs[h].astype(BF16), preferred_element_type=F32)

    k_end = rhs(n)
    new_states = []
    for h in range(nh):
        upd = lax.dot_general(vs[h].astype(BF16), mask(k_end, h).astype(BF16),
                              (((0,), (0,)), ((), ())), preferred_element_type=F32)
        new_states.append(states[h] * jnp.exp(r[n]) + upd)
    return outs, new_states


def _scan_kernel(*refs, mode, chunk, nh):
    if mode == "gla":
        (zq_ref, zk_ref, zv_ref, zs_ref, zr_ref, wg2_ref, bg_ref, ng_ref, ones_ref, sel_ref,
         o_ref, sout_ref, zg_s, st_s) = refs
    else:
        (zq_ref, zf_ref, zv_ref, zr_ref, llb_ref, l1m_ref, oml_ref, ng_ref, ones_ref, sel_ref,
         o_ref, sout_ref, st_s) = refs
    ti = pl.program_id(2)
    tblk = zq_ref.shape[0]
    nu = zq_ref.shape[1] // LANES

    @pl.when(ti == 0)
    def _():
        st_s[...] = jnp.zeros_like(st_s)

    if mode == "gla":
        dk = LANES // nh
        zg_s[...] = jnp.dot(zs_ref[...].astype(BF16), wg2_ref[...], preferred_element_type=F32) + bg_ref[...]
        la_low = _log_sigmoid(jnp.min(zg_s[...], axis=0, keepdims=True)) * (1.0 / GLA_GATE_TEMP)
        lane = lax.broadcasted_iota(jnp.int32, (1, LANES), 1)
        masks = [((lane >= h * dk) & (lane < (h + 1) * dk)).astype(F32) for h in range(nh)]
    else:
        hf_min = jnp.min(zf_ref[...], axis=0, keepdims=True)
        la_low = jnp.maximum(llb_ref[...], l1m_ref[...] + _log_sigmoid(hf_min))
        masks = [None]

    def features(sl, u):
        ul = slice(u * LANES, (u + 1) * LANES)
        if mode == "gla":
            return (zq_ref[sl, ul] * dk ** -0.5, zk_ref[sl, ul],
                    _log_sigmoid(zg_s[sl, ul]) * (1.0 / GLA_GATE_TEMP))
        hf = zf_ref[sl, ul]
        e = jnp.exp(-jnp.abs(hf))
        inv = 1.0 / (1.0 + e)
        sig_neg = jnp.where(hf > 0, e * inv, inv)
        lsig = jnp.minimum(hf, 0.0) - jnp.log(1.0 + e)
        a = llb_ref[:, ul]
        bb = l1m_ref[:, ul] + lsig
        la = jnp.maximum(a, bb) + jnp.log(1.0 + jnp.exp(-jnp.abs(a - bb)))
        return _silu(zq_ref[sl, ul]), oml_ref[:, ul] * sig_neg, la

    sel = sel_ref[...]
    ones_ws = [ones_ref[h] for h in range(nh)]

    def body(ci, _, fast):
        off = pl.multiple_of(ci * chunk, chunk)
        sl = pl.ds(off, chunk)
        for u in range(nu):
            hl = [slice((u * nh + h) * LANES, (u * nh + h + 1) * LANES) for h in range(nh)]
            vs = [zv_ref[sl, hl[h]] for h in range(nh)]
            states = [st_s[u * nh + h] for h in range(nh)]
            q, k, la = features(sl, u)
            outs, new_states = _scan_chunk(q, k, la, vs, masks, states, ones_ws, sel, fast)
            for h in range(nh):
                st_s[u * nh + h] = new_states[h]
                o = _rms(outs[h], ng_ref[...]) * _silu(zr_ref[sl, hl[h]])
                o_ref[0, sl, hl[h]] = o.astype(o_ref.dtype)
        return 0

    safe = jnp.min(la_low) * SUB >= -FAST_BLOCK_DECAY

    @pl.when(safe)
    def _():
        lax.fori_loop(0, tblk // chunk, functools.partial(body, fast=True), 0,
                      unroll=2 if (tblk // chunk) % 2 == 0 else 1)

    @pl.when(jnp.logical_not(safe))
    def _():
        lax.fori_loop(0, tblk // chunk, functools.partial(body, fast=False), 0)

    @pl.when(ti == pl.num_programs(2) - 1)
    def _():
        dk_out = LANES // nh
        for u in range(nu):
            for h in range(nh):
                sout_ref[0, u * nh + h] = st_s[u * nh + h].T[h * dk_out:(h + 1) * dk_out, :]


def _scan_consts(chunk, nh):
    n = chunk // SUB
    sel = np.zeros((chunk, n * SUB * SUB), np.float32)
    for i in range(n):
        for t in range(SUB):
            sel[i * SUB + t, i * 64 + t * SUB:i * 64 + (t + 1) * SUB] = 1.0
    ones = np.zeros((nh, LANES, LANES), np.float32)
    dk = LANES // nh
    for h in range(nh):
        ones[h, h * dk:(h + 1) * dk, :] = 1.0
    return jnp.asarray(ones, BF16), jnp.asarray(sel, BF16)


def _lane_block(col, name, width):
    assert (col[name] * LANES) % width == 0
    return col[name] * LANES // width


def _scan_gla(z, b, t, units, nu, col, wg2p, bg, ng, tblk, chunk):
    nh = 2
    nt = t // tblk
    ones, sel = _scan_consts(chunk, nh)
    kw, vw = nu * LANES, nu * nh * LANES
    zspec = lambda name, w: pl.BlockSpec((tblk, w), lambda i, u, j: (i * nt + j, _lane_block(col, name, w) + u))
    in_specs = [
        zspec("gq", kw), zspec("gk", kw), zspec("gv", vw),
        pl.BlockSpec((tblk, LANES), lambda i, u, j: (i * nt + j, col["small_g"])),
        zspec("gr", vw),
        pl.BlockSpec((LANES, kw), lambda i, u, j: (0, u)),
        pl.BlockSpec((1, kw), lambda i, u, j: (0, u)),
        pl.BlockSpec((1, LANES), lambda i, u, j: (0, 0)),
        pl.BlockSpec(ones.shape, lambda i, u, j: (0, 0, 0)),
        pl.BlockSpec(sel.shape, lambda i, u, j: (0, 0)),
    ]
    dk = LANES // nh
    return pl.pallas_call(
        functools.partial(_scan_kernel, mode="gla", chunk=chunk, nh=nh),
        grid=(b, units // nu, nt),
        in_specs=in_specs,
        out_specs=[pl.BlockSpec((1, tblk, vw), lambda i, u, j: (i, j, u)),
                   pl.BlockSpec((1, nu * nh, dk, LANES), lambda i, u, j: (i, u, 0, 0))],
        out_shape=[jax.ShapeDtypeStruct((b, t, units * nh * LANES), BF16),
                   jax.ShapeDtypeStruct((b, units * nh, dk, LANES), F32)],
        scratch_shapes=[pltpu.VMEM((tblk, kw), F32), pltpu.VMEM((nu * nh, LANES, LANES), F32)],
        compiler_params=_cparams(("parallel", "parallel", "arbitrary")),
        name="scan_gla",
    )(z, z, z, z, z, wg2p, bg, ng, ones, sel)


def _scan_hgrn(z, b, t, units, nu, col, llb, l1m, oml, ng, tblk, chunk):
    nh = 1
    nt = t // tblk
    ones, sel = _scan_consts(chunk, nh)
    kw = nu * LANES
    zspec = lambda name: pl.BlockSpec((tblk, kw), lambda i, u, j: (i * nt + j, _lane_block(col, name, kw) + u))
    pspec = pl.BlockSpec((1, kw), lambda i, u, j: (0, u))
    in_specs = [zspec("hq"), zspec("hf"), zspec("hi"), zspec("hg"), pspec, pspec, pspec,
                pl.BlockSpec((1, LANES), lambda i, u, j: (0, 0)),
                pl.BlockSpec(ones.shape, lambda i, u, j: (0, 0, 0)),
                pl.BlockSpec(sel.shape, lambda i, u, j: (0, 0))]
    return pl.pallas_call(
        functools.partial(_scan_kernel, mode="hgrn", chunk=chunk, nh=nh),
        grid=(b, units // nu, nt),
        in_specs=in_specs,
        out_specs=[pl.BlockSpec((1, tblk, kw), lambda i, u, j: (i, j, u)),
                   pl.BlockSpec((1, nu, LANES, LANES), lambda i, u, j: (i, u, 0, 0))],
        out_shape=[jax.ShapeDtypeStruct((b, t, units * LANES), BF16),
                   jax.ShapeDtypeStruct((b, units, LANES, LANES), F32)],
        scratch_shapes=[pltpu.VMEM((nu, LANES, LANES), F32)],
        compiler_params=_cparams(("parallel", "parallel", "arbitrary")),
        name="scan_hgrn",
    )(z, z, z, z, llb, l1m, oml, ng, ones, sel)


def _rec_kernel(*refs, mode, nseq, ntok, dk):
    if mode == "gla":
        (qt_ref, kt_ref, st_ref, wg2t_ref, bgt_ref, zv_ref, zr_ref, ng_ref, s0_ref, o_ref, sout_ref, o_s) = refs
        qc = qt_ref[...] * dk ** -0.5
        kc = kt_ref[...]
        zg = jnp.dot(wg2t_ref[...], st_ref[...].astype(BF16), preferred_element_type=F32) + bgt_ref[...]
        ac = jnp.exp(_log_sigmoid(zg) * (1.0 / GLA_GATE_TEMP))
    else:
        (qt_ref, ft_ref, llb_ref, l1m_ref, oml_ref, zv_ref, zr_ref, ng_ref, s0_ref, o_ref, sout_ref, o_s) = refs
        hf = ft_ref[...]
        lsig = _log_sigmoid(hf)
        a = llb_ref[...]
        bb = l1m_ref[...] + lsig
        ac = jnp.exp(jnp.maximum(a, bb) + jnp.log(1.0 + jnp.exp(-jnp.abs(a - bb))))
        kc = oml_ref[...] * _sigmoid(-hf)
        qc = _silu(qt_ref[...])
    for sq in range(nseq):
        s = s0_ref[sq, 0]
        for t in range(ntok):
            j = sq * ntok + t
            vrow = zv_ref[j:j + 1, :]
            s = s * ac[:, j:j + 1] + kc[:, j:j + 1] * vrow
            o_s[j:j + 1, :] = jnp.sum(s * qc[:, j:j + 1], axis=0, keepdims=True)
        sout_ref[sq, 0] = s
    o = _rms(o_s[...], ng_ref[...]) * _silu(zr_ref[...])
    o_ref[...] = o.astype(o_ref.dtype)


def _rec(mode, zt, z, col, params, ng, s0, nb, ntok, heads, dk, nseq):
    m = nb * ntok
    rows = nseq * ntok
    ng_groups = nb // nseq
    if mode == "gla":
        wg2t, bgt = params
        per = LANES // dk
        tspec = lambda name: pl.BlockSpec((None, dk, rows), lambda h, g: (g, col[name] * per + h, 0))
        in_specs = [tspec("gq"), tspec("gk"),
                    pl.BlockSpec((None, LANES, rows), lambda h, g: (g, col["small_g"], 0)),
                    pl.BlockSpec((dk, LANES), lambda h, g: (h, 0)),
                    pl.BlockSpec((dk, 1), lambda h, g: (h, 0)),
                    pl.BlockSpec((rows, LANES), lambda h, g: (g, col["gv"] + h)),
                    pl.BlockSpec((rows, LANES), lambda h, g: (g, col["gr"] + h))]
        args = (zt, zt, zt, wg2t, bgt, z, z)
    else:
        llb, l1m, oml = params
        tspec = lambda name: pl.BlockSpec((None, dk, rows), lambda h, g: (g, col[name] + h, 0))
        pspec = pl.BlockSpec((dk, 1), lambda h, g: (h, 0))
        in_specs = [tspec("hq"), tspec("hf"), pspec, pspec, pspec,
                    pl.BlockSpec((rows, LANES), lambda h, g: (g, col["hi"] + h)),
                    pl.BlockSpec((rows, LANES), lambda h, g: (g, col["hg"] + h))]
        args = (zt, zt, llb, l1m, oml, z, z)
    in_specs += [pl.BlockSpec((1, LANES), lambda h, g: (0, 0)),
                 pl.BlockSpec((nseq, 1, dk, LANES), lambda h, g: (g, h, 0, 0))]
    return pl.pallas_call(
        functools.partial(_rec_kernel, mode=mode, nseq=nseq, ntok=ntok, dk=dk),
        grid=(heads, ng_groups),
        in_specs=in_specs,
        out_specs=[pl.BlockSpec((rows, LANES), lambda h, g: (g, h)),
                   pl.BlockSpec((nseq, 1, dk, LANES), lambda h, g: (g, h, 0, 0))],
        out_shape=[jax.ShapeDtypeStruct((m, heads * LANES), BF16),
                   jax.ShapeDtypeStruct(s0.shape, F32)],
        scratch_shapes=[pltpu.VMEM((rows, LANES), F32)],
        compiler_params=_cparams(("parallel", "parallel")),
        name="rec_" + mode,
    )(*args, ng, s0)


def _lfpool_kernel(lf_ref, o_ref, *, nh):
    x = lf_ref[...]
    w = x.shape[1]
    lane = lax.broadcasted_iota(jnp.int32, x.shape, 1)
    incl = x
    sh = nh
    while sh < w:
        incl = incl + jnp.where(lane < w - sh, pltpu.roll(incl, w - sh, 1), 0.0)
        sh *= 2
    tot = jnp.where(lane < nh, incl, 0.0)
    sh = nh
    while sh < w:
        tot = tot + pltpu.roll(tot, sh, 1)
        sh *= 2
    o_ref[:, :w] = incl - x
    o_ref[:, w:] = tot


def _lfpool(clf, nh, rows):
    depth, n_pool, w = clf.shape
    return pl.pallas_call(
        functools.partial(_lfpool_kernel, nh=nh),
        grid=(depth, n_pool // rows),
        in_specs=[pl.BlockSpec((None, rows, w), lambda l, i: (l, i, 0))],
        out_specs=pl.BlockSpec((None, rows, 2 * w), lambda l, i: (l, i, 0)),
        out_shape=jax.ShapeDtypeStruct((depth, n_pool, 2 * w), F32),
        compiler_params=_cparams(("parallel", "parallel")),
        name="lfpool",
    )(clf)


def _paged_kernel(pt_ref, q_ref, kn_ref, vn_ref, zs_ref, bf_ref, mask_ref, *rest, npg, nh, dh, ntok, scale):
    k_refs = rest[:npg]
    v_refs = rest[npg:2 * npg]
    rt_ref, o_ref, lf_ref = rest[2 * npg:2 * npg + 3]
    m_s, l_s, acc_s, car_s = rest[2 * npg + 3:]
    g = pl.program_id(1)

    @pl.when(g == 0)
    def _():
        m_s[...] = jnp.full(m_s.shape, NEG, F32)
        l_s[...] = jnp.zeros_like(l_s)
        acc_s[...] = jnp.zeros_like(acc_s)
        car_s[...] = jnp.zeros_like(car_s)

    q = q_ref[0] * scale
    q_all = jnp.concatenate([q[:, h * dh:(h + 1) * dh] for h in range(nh)], axis=0).astype(BF16)
    w = mask_ref.shape[1]
    carry = car_s[...]
    s_pages = []
    for i in range(npg):
        rt = rt_ref[0, i:i + 1, :]
        bias = rt[:, :w] + carry
        carry = carry + rt[:, w:]
        s_pages.append(_dot_t(q_all, k_refs[i][...].astype(BF16)) + (mask_ref[...] + bias))
    car_s[...] = carry
    s = jnp.concatenate(s_pages, axis=1)
    m = m_s[...]
    m_new = jnp.maximum(m, jnp.max(s, axis=-1, keepdims=True))
    alpha = jnp.exp(m - m_new)
    p = jnp.exp(s - m_new)
    l_s[...] = alpha * l_s[...] + jnp.sum(p, axis=-1, keepdims=True)
    pv = jnp.zeros(acc_s.shape, F32)
    for i in range(npg):
        pv = pv + jnp.dot(p[:, i * w:(i + 1) * w].astype(BF16), v_refs[i][...].astype(BF16),
                          preferred_element_type=F32)
    acc_s[...] = alpha * acc_s[...] + pv
    m_s[...] = m_new

    @pl.when(g == pl.num_programs(1) - 1)
    def _():
        lfn = _log_sigmoid(zs_ref[0] + bf_ref[...])
        lf_ref[0] = lfn
        row = lax.broadcasted_iota(jnp.int32, lfn.shape, 0)
        cn = _local_cumsum(jnp.where(row < ntok, lfn, 0.0))
        trow = lax.broadcasted_iota(jnp.int32, (SUB, 1), 0)
        for h in range(nh):
            hs = slice(h * dh, (h + 1) * dh)
            rs = slice(h * SUB, (h + 1) * SUB)
            m, l, acc = m_s[rs, :], l_s[rs, :], acc_s[rs, :]
            for sp in range(ntok):
                logit = jnp.sum(q[:, hs] * kn_ref[0, sp:sp + 1, hs], axis=-1, keepdims=True)
                logit = logit - cn[sp:sp + 1, FF_OFF + h:FF_OFF + h + 1]
                logit = jnp.where(trow >= sp, logit, NEG)
                m_new = jnp.maximum(m, logit)
                alpha = jnp.exp(m - m_new)
                p = jnp.exp(logit - m_new)
                l = alpha * l + p
                acc = alpha * acc + p * vn_ref[0, sp:sp + 1, hs]
                m = m_new
            o_ref[0, :, hs] = (acc / l).astype(o_ref.dtype)


def _paged(layer, page_table, q, kn, vn, zs, bf_row, cache_k, cache_v, rt_seq, nh, dh, ntok, npg):
    nb = q.shape[0]
    w = cache_k.shape[2]
    npages = page_table.shape[1]
    ngrp = npages // npg
    mask = np.full((nh * SUB, w), NEG, np.float32)
    for h in range(nh):
        mask[h * SUB:(h + 1) * SUB, h::nh] = 0.0

    def page_of(b, g, pt, i):
        return pt[b, npages - 1 - (g * npg + i)]

    seq3 = lambda b, g, pt: (b, 0, 0)
    in_specs = [pl.BlockSpec((1, SUB, nh * dh), seq3)] * 3 + [
        pl.BlockSpec((1, SUB, LANES), seq3),
        pl.BlockSpec((1, LANES), lambda b, g, pt: (0, 0)),
        pl.BlockSpec(mask.shape, lambda b, g, pt: (0, 0))]
    kv_specs = [pl.BlockSpec((None, None, w, dh), lambda b, g, pt, i=i: (layer, page_of(b, g, pt, i), 0, 0))
                for i in range(npg)]
    in_specs += kv_specs + kv_specs
    in_specs += [pl.BlockSpec((1, npg, 2 * w), lambda b, g, pt: (b, g, 0))]
    grid_spec = pltpu.PrefetchScalarGridSpec(
        num_scalar_prefetch=1, grid=(nb, ngrp), in_specs=in_specs,
        out_specs=[pl.BlockSpec((1, SUB, nh * dh), seq3), pl.BlockSpec((1, SUB, LANES), seq3)],
        scratch_shapes=[pltpu.VMEM((nh * SUB, 1), F32), pltpu.VMEM((nh * SUB, 1), F32),
                        pltpu.VMEM((nh * SUB, dh), F32), pltpu.VMEM((1, w), F32)])
    return pl.pallas_call(
        functools.partial(_paged_kernel, npg=npg, nh=nh, dh=dh, ntok=ntok, scale=dh ** -0.5),
        grid_spec=grid_spec,
        out_shape=[jax.ShapeDtypeStruct((nb, SUB, nh * dh), F32),
                   jax.ShapeDtypeStruct((nb, SUB, LANES), F32)],
        compiler_params=_cparams(("parallel", "arbitrary")),
        name="paged",
    )(page_table, q, kn, vn, zs, bf_row, jnp.asarray(mask), *([cache_k] * npg), *([cache_v] * npg), rt_seq)


def _merge_kernel(x_ref, oa_ref, ob_ref, oc_ref, g1_ref, wg_ref, wa_ref, wb_ref, wc_ref, wo_ref, y_ref):
    x = x_ref[...]
    d = x.shape[1]
    xn = _rms(x, g1_ref[...]).astype(BF16)
    merged = jnp.zeros(x.shape, F32)
    for i, (o_ref, w_ref) in enumerate(((oa_ref, wa_ref), (ob_ref, wb_ref), (oc_ref, wc_ref))):
        gate = _sigmoid(jnp.dot(xn, wg_ref[:, i * d:(i + 1) * d], preferred_element_type=F32))
        merged = merged + gate * jnp.dot(o_ref[...], w_ref[...], preferred_element_type=F32)
    y_ref[...] = x + jnp.dot(merged.astype(BF16), wo_ref[...], preferred_element_type=F32)


def _merge(x, oa, ob, oc, g1, wg_all, layer, wa, wb, wc, wo, tm):
    m, d = x.shape
    const = lambda a: pl.BlockSpec(a.shape, lambda i: (0,) * a.ndim)
    rows = lambda a: pl.BlockSpec((tm, a.shape[1]), lambda i: (i, 0))
    return pl.pallas_call(
        _merge_kernel,
        grid=(m // tm,),
        in_specs=[rows(x), rows(oa), rows(ob), rows(oc), const(g1),
                  pl.BlockSpec((None,) + wg_all.shape[1:], lambda i: (layer, 0, 0)),
                  const(wa), const(wb), const(wc), const(wo)],
        out_specs=pl.BlockSpec((tm, d), lambda i: (i, 0)),
        out_shape=jax.ShapeDtypeStruct((m, d), F32),
        compiler_params=_cparams(("parallel",)),
        name="merge",
    )(x, oa, ob, oc, g1, wg_all, wa, wb, wc, wo)


def _ffn_kernel(x_ref, g2_ref, wg_ref, wu_ref, wd_ref, gf_ref, y_ref, *, final):
    x = x_ref[...]
    h = _rms(x, g2_ref[...]).astype(BF16)
    acc = x
    hidden = wg_ref.shape[1]
    for c0 in range(0, hidden, FFN_HIDDEN_CHUNK):
        cs = slice(c0, min(c0 + FFN_HIDDEN_CHUNK, hidden))
        a = jnp.dot(h, wg_ref[:, cs], preferred_element_type=F32)
        u = jnp.dot(h, wu_ref[:, cs], preferred_element_type=F32)
        acc = acc + jnp.dot((_silu(a) * u).astype(BF16), wd_ref[cs, :], preferred_element_type=F32)
    y_ref[...] = _rms(acc, gf_ref[...]) if final else acc


def _ffn(x, g2, wg, wu, wd, gf, tm, final):
    m, d = x.shape
    const = lambda a: pl.BlockSpec(a.shape, lambda i: (0,) * a.ndim)
    return pl.pallas_call(
        functools.partial(_ffn_kernel, final=final),
        grid=(m // tm,),
        in_specs=[pl.BlockSpec((tm, d), lambda i: (i, 0)), const(g2), const(wg), const(wu), const(wd), const(gf)],
        out_specs=pl.BlockSpec((tm, d), lambda i: (i, 0)),
        out_shape=jax.ShapeDtypeStruct((m, d), F32),
        compiler_params=_cparams(("parallel",)),
        name="ffn",
    )(x, g2, wg, wu, wd, gf)


def _pick(n, pref):
    for c in pref:
        if n % c == 0:
            return c
    return n


def kernel(x_prompt, x_sample, state_gla, cache_fox_k, cache_fox_v, cache_fox_logf, state_hgrn, page_table,
           norm1_g, w_in, gla_wg2, gla_bg, gla_norm_g, fox_bf, hg_lb_logits, hg_norm_g,
           w_branch_a, w_branch_b, w_branch_c, w_out, norm2_g, w_ffn_gate, w_ffn_up, w_ffn_down,
           final_norm_g):
    depth, d_model, _ = w_in.shape
    bp, tp, _ = x_prompt.shape
    nb, ntok, _ = x_sample.shape
    _, _, gh, gdk, gdv = state_gla.shape
    _, _, hh, hdk, hdv = state_hgrn.shape
    _, n_pool, page, fh, fdh = cache_fox_k.shape
    hidden = w_ffn_gate.shape[2]
    assert gdv == LANES and hdk == LANES and hdv == LANES and fdh == LANES and 2 * gdk == LANES
    assert ntok <= SUB and gh % 2 == 0

    gq_w, gv_w, f_w, h_w = gh * gdk, gh * gdv, fh * fdh, hh * hdk
    names = ["gq", "gk", "gv", "glr", "gr", "fq", "fk", "fv", "ff", "hq", "hf", "hi", "hg", "ga", "gb", "gc"]
    widths = [gq_w, gq_w, gv_w, GLA_GATE_RANK, gv_w, f_w, f_w, f_w, fh, h_w, h_w, h_w, h_w,
              d_model, d_model, d_model]
    starts = dict(zip(names, np.concatenate([[0], np.cumsum(widths)[:-1]]).tolist()))
    wid = dict(zip(names, widths))
    order = ["gq", "gk", "gv", "gr", "fq", "fk", "fv", "hq", "hf", "hi", "hg"]
    col, off = {}, 0
    for nm in order:
        col[nm] = off // LANES
        off += wid[nm]
    col["small_f"] = off // LANES
    col["small_g"] = off // LANES + 1
    n_used = off + 2 * LANES
    tn = min(14, n_used // LANES) * LANES
    n_pad = -(-n_used // tn) * tn

    blocks = [(starts[nm] + k, 0, LANES) for nm in order for k in range(0, wid[nm], LANES)]
    blocks += [(starts["ff"] - FF_OFF, FF_OFF, FF_OFF + fh),
               (starts["glr"] - GLR_OFF, GLR_OFF, GLR_OFF + GLA_GATE_RANK)]
    blocks += [(0, 0, 0)] * ((n_pad - n_used) // LANES)
    w_cols = jnp.transpose(w_in, (2, 0, 1))
    w_pad_all = _wprep(w_cols, blocks)
    w_gates_all = _wprep(w_cols, [(starts["ga"] + k, 0, LANES) for k in range(0, 3 * d_model, LANES)])

    lb_cum = jnp.cumsum(jax.nn.softmax(hg_lb_logits.astype(F32), axis=0), axis=0)
    hg_lb = lb_cum - lb_cum[:1]
    log_lb, log1m_lb, one_m_lb = jnp.log(hg_lb), jnp.log1p(-hg_lb), 1.0 - hg_lb


    xp = x_prompt.reshape(bp * tp, d_model)
    xs = jnp.pad(x_sample, ((0, 0), (0, SUB - ntok), (0, 0))).reshape(nb * SUB, d_model)
    ms = nb * SUB

    tm_p = _pick(bp * tp, (1024, 512, 256, 128))
    tt = _pick(tp, (1024, 512, 256, 128))
    fblk = _pick(tp, (1024, 512, 256))
    tblk = _pick(tp, (1024, 512, 256, 128))
    chunk = min(128, tblk)
    tm_e = _pick(bp * tp, (512, 256, 128))
    nseq = _pick(nb, (8, 4, 2, 1))
    npg = _pick(page_table.shape[1], (32, 16, 8, 4, 2, 1))

    ck = cache_fox_k.reshape(depth, n_pool, page * fh, fdh)
    cv = cache_fox_v.reshape(depth, n_pool, page * fh, fdh)
    rtot = _lfpool(cache_fox_logf.astype(F32).reshape(depth, n_pool, page * fh), fh, _pick(n_pool, (256, 128, 64, 32, 16, 8)))

    outs = {k: [] for k in ("gla_p", "gla_s", "k_p", "v_p", "lf_p", "k_s", "v_s", "lf_s", "hg_p", "hg_s")}

    for l in range(depth):
        g1 = norm1_g[l].reshape(1, d_model)
        g2 = norm2_g[l].reshape(1, d_model)
        bf_row = jnp.zeros((1, LANES), F32).at[0, FF_OFF:FF_OFF + fh].set(fox_bf[l])
        wg2p = jnp.zeros((LANES, gq_w), F32).at[GLR_OFF:GLR_OFF + GLA_GATE_RANK].set(gla_wg2[l])
        wg2_b = wg2p.astype(BF16)
        bg_row = gla_bg[l].reshape(1, gq_w)
        gng = gla_norm_g[l].reshape(1, LANES)
        hng = hg_norm_g[l].reshape(1, LANES)
        llb_u, l1m_u, oml_u = (a[l].reshape(1, h_w) for a in (log_lb, log1m_lb, one_m_lb))
        wa, wb, wc, wo = (w[l].astype(BF16) for w in (w_branch_a, w_branch_b, w_branch_c, w_out))
        wfg, wfu, wfd = (w[l].astype(BF16) for w in (w_ffn_gate, w_ffn_up, w_ffn_down))
        gf = final_norm_g.reshape(1, d_model)
        final = l == depth - 1

        z, fk, fv = _proj(xp, g1, w_pad_all, l, tm_p, tn, (col["fk"] * LANES, col["fv"] * LANES, fh, fdh))
        outs["k_p"].append(fk.reshape(bp, tp, fh, fdh))
        outs["v_p"].append(fv.reshape(bp, tp, fh, fdh))
        lf, ctok = _fox_prep(z, bf_row, bp, tp, col["small_f"], fh, tt)
        outs["lf_p"].append(lf)
        ob = _flash(z, ctok, bp, tp, fh, fdh, col["fq"], col["fk"], col["fv"], fblk)
        oa, sg = _scan_gla(z, bp, tp, gh // 2, GLA_UNITS_PER_STEP, col, wg2_b, bg_row, gng, tblk, chunk)
        outs["gla_p"].append(sg)
        oc, sh = _scan_hgrn(z, bp, tp, hh, HGRN_UNITS_PER_STEP, col, llb_u, l1m_u, oml_u, hng, tblk, chunk)
        outs["hg_p"].append(sh)
        x1 = _merge(xp, oa.reshape(bp * tp, -1), ob.reshape(bp * tp, -1), oc.reshape(bp * tp, -1),
                    g1, w_gates_all, l, wa, wb, wc, wo, tm_e)
        xp = _ffn(x1, g2, wfg, wfu, wfd, gf, tm_e, final)

        zs, = _proj(xs, g1, w_pad_all, l, ms, tn)
        zs3 = zs.reshape(nb, SUB, n_pad)
        take = lambda nm, w: zs3[:, :, col[nm] * LANES:col[nm] * LANES + w]
        fks, fvs = take("fk", f_w), take("fv", f_w)
        outs["k_s"].append(fks[:, :ntok].reshape(nb, ntok, fh, fdh))
        outs["v_s"].append(fvs[:, :ntok].reshape(nb, ntok, fh, fdh))
        small_s = take("small_f", LANES)
        rt_seq = jnp.take(rtot[l], page_table[:, ::-1], axis=0)
        obs, lfs = _paged(l, page_table, take("fq", f_w), fks, fvs, small_s, bf_row, ck, cv, rt_seq,
                          fh, fdh, ntok, npg)
        outs["lf_s"].append(lfs[:, :ntok, FF_OFF:FF_OFF + fh])
        obs = obs.astype(BF16)
        zc = zs3[:, :ntok].reshape(nb * ntok, n_pad)
        zt = zc.reshape(nb // nseq, nseq * ntok, n_pad).transpose(0, 2, 1)
        wg2t = wg2p.T.astype(BF16)
        oas, sgs = _rec("gla", zt, zc, col, (wg2t, gla_bg[l].reshape(-1, 1)), gng, state_gla[l],
                        nb, ntok, gh, gdk, nseq)
        outs["gla_s"].append(sgs)
        ocs, shs = _rec("hgrn", zt, zc, col, tuple(a[l].reshape(-1, 1) for a in (log_lb, log1m_lb, one_m_lb)),
                        hng, state_hgrn[l], nb, ntok, hh, hdk, nseq)
        outs["hg_s"].append(shs)
        pad_tok = lambda o: jnp.pad(o.reshape(nb, ntok, -1), ((0, 0), (0, SUB - ntok), (0, 0))).reshape(ms, -1)
        x1s = _merge(xs, pad_tok(oas), obs.reshape(ms, -1), pad_tok(ocs), g1, w_gates_all, l, wa, wb, wc, wo, ms)
        xs = _ffn(x1s, g2, wfg, wfu, wfd, gf, ms, final)

    st = lambda k: jnp.stack(outs[k])
    y_p = xp.reshape(bp, tp, d_model)
    y_s = xs.reshape(nb, SUB, d_model)[:, :ntok]
    return (y_p, y_s, st("gla_p"), st("gla_s"), st("k_p"), st("v_p"), st("lf_p"),
            st("k_s"), st("v_s"), st("lf_s"), st("hg_p"), st("hg_s"))
```
